```python
import jax, jax.numpy as jnp
from jax import lax
import numpy as np

D_MODEL = 1024
BATCH = 8
SEQ = 2048
DEPTH = 2
DEC_BATCH = 128
DEC_SEQ = 8
PAST_LEN = 8192
PAGE_SIZE = 128

HEAD_DIM = 64
N_HEADS = 8
N_KV_HEADS = 2
GROUP = N_HEADS // N_KV_HEADS
ATTN_DIM = N_HEADS * HEAD_DIM
KV_DIM = N_KV_HEADS * HEAD_DIM
WINDOW = 128
BLOCK = WINDOW
CONV_DIM = D_MODEL // 2
CONV_WIDTH = 3
D_FF = 4 * D_MODEL
N_BRANCH = 2
IN_DIM = ATTN_DIM + 2 * KV_DIM + 3 * CONV_DIM + N_BRANCH * D_MODEL
CACHE_LEN = min(WINDOW, PAST_LEN)
EPS = 1e-6

kernel_name = "hybrid_swa_sink_shortconv_step"


def rms_norm(x, g):
    xf = x.astype(jnp.float32)
    y = xf * lax.rsqrt(jnp.mean(xf * xf, axis=-1, keepdims=True) + EPS)
    return (y * g.astype(jnp.float32)).astype(x.dtype)


def split_projection(z):
    outs = []
    off = 0
    for width in (ATTN_DIM, KV_DIM, KV_DIM, CONV_DIM, CONV_DIM, CONV_DIM, N_BRANCH * D_MODEL):
        outs.append(z[..., off:off + width])
        off += width
    return outs


def attend(q, k, v, mask, sinks):
    s = jnp.einsum('...qkgd,...skd->...kgqs', q, k).astype(jnp.float32) * (HEAD_DIM ** -0.5)
    s = jnp.where(mask, s, -jnp.inf)
    sink = jnp.broadcast_to(sinks.astype(jnp.float32)[:, :, None, None], s.shape[:-1] + (1,))
    p = jax.nn.softmax(jnp.concatenate([s, sink], axis=-1), axis=-1)[..., :-1]
    return jnp.einsum('...kgqs,...skd->...qkgd', p.astype(v.dtype), v)


def window_attention_prompt(q, k, v, sinks):
    n, s = q.shape[:2]
    nb = s // BLOCK
    qb = q.reshape(n, nb, BLOCK, N_KV_HEADS, GROUP, HEAD_DIM)

    def with_prev(t):
        tb = t.reshape(n, nb, BLOCK, N_KV_HEADS, HEAD_DIM)
        prev = jnp.pad(tb[:, :-1], ((0, 0), (1, 0), (0, 0), (0, 0), (0, 0)))
        return jnp.concatenate([prev, tb], axis=2)

    kk, vv = with_prev(k), with_prev(v)
    rel = jnp.arange(BLOCK)[:, None] + BLOCK - jnp.arange(2 * BLOCK)[None, :]
    band = (rel >= 0) & (rel < WINDOW)
    kpos = (jnp.arange(nb)[:, None] - 1) * BLOCK + jnp.arange(2 * BLOCK)[None, :]
    mask = (band[None] & (kpos >= 0)[:, None, :])[:, None, None]
    o = attend(qb, kk, vv, mask, sinks)
    return o.reshape(n, s, ATTN_DIM)


def window_attention_sample(q, k_new, v_new, k_buf, v_buf, sinks):
    n, t = q.shape[:2]
    l = k_buf.shape[1]
    kk = jnp.concatenate([k_buf, k_new], axis=1)
    vv = jnp.concatenate([v_buf, v_new], axis=1)
    rel = (l + jnp.arange(t))[:, None] - jnp.arange(l + t)[None, :]
    mask = (rel >= 0) & (rel < WINDOW)
    o = attend(q, kk, vv, mask, sinks)
    return o.reshape(n, t, ATTN_DIM), kk[:, t:], vv[:, t:]


def causal_short_conv(u, u_prev, w):
    t = u.shape[1]
    up = jnp.concatenate([u_prev, u], axis=1)
    z = w[0] * up[:, 0:t]
    for i in range(1, CONV_WIDTH):
        z = z + w[i] * up[:, i:i + t]
    return z, up[:, up.shape[1] - (CONV_WIDTH - 1):]


def decoder_layer(x, buf_k, buf_v, buf_conv, g_mix_pre, g_mix_post, g_mlp_pre, g_mlp_post,
                  w_in, attn_sinks, conv_w, w_attn_o, w_conv_o, w_out, w_up, w_down):
    n, t, _ = x.shape
    h = rms_norm(x, g_mix_pre)
    q, k, v, b_gate, c_gate, u_in, gate_logits = split_projection(h @ w_in)
    q = q.reshape(n, t, N_KV_HEADS, GROUP, HEAD_DIM)
    k = k.reshape(n, t, N_KV_HEADS, HEAD_DIM)
    v = v.reshape(n, t, N_KV_HEADS, HEAD_DIM)
    sinks = attn_sinks.reshape(N_KV_HEADS, GROUP)
    u = c_gate * u_in
    if buf_k is None:
        attn = window_attention_prompt(q, k, v, sinks)
        new_k = k[:, t - CACHE_LEN:]
        new_v = v[:, t - CACHE_LEN:]
        conv_prev = jnp.zeros((n, CONV_WIDTH - 1, CONV_DIM), u.dtype)
    else:
        attn, new_k, new_v = window_attention_sample(q, k, v, buf_k, buf_v, sinks)
        conv_prev = buf_conv
    z, new_conv = causal_short_conv(u, conv_prev, conv_w)
    attn_branch = attn @ w_attn_o
    conv_branch = (b_gate * z) @ w_conv_o
    gates = jax.nn.sigmoid(gate_logits).reshape(n, t, N_BRANCH, D_MODEL)
    mixed = (gates[..., 0, :] * attn_branch + gates[..., 1, :] * conv_branch) @ w_out
    x = x + rms_norm(mixed, g_mix_post)
    hm = rms_norm(x, g_mlp_pre)
    ff = jnp.square(jax.nn.relu(hm @ w_up)) @ w_down
    x = x + rms_norm(ff, g_mlp_post)
    return x, new_k, new_v, new_conv


def setup_inputs(seed: int = 0) -> dict:
    key = jax.random.key(seed)
    ks = jax.random.split(key, 20)
    f32 = jnp.float32

    def nrm(k, shape, scale):
        return jax.random.normal(k, shape, f32) * scale

    def gain(k):
        return 1.0 + 0.01 * jax.random.normal(k, (DEPTH, D_MODEL), f32)

    return {
        'x_prompt': nrm(ks[0], (BATCH, SEQ, D_MODEL), 1.0),
        'x_sample': nrm(ks[1], (DEC_BATCH, DEC_SEQ, D_MODEL), 1.0),
        'cache_k': nrm(ks[2], (DEPTH, DEC_BATCH, CACHE_LEN, N_KV_HEADS, HEAD_DIM), 1.0),
        'cache_v': nrm(ks[3], (DEPTH, DEC_BATCH, CACHE_LEN, N_KV_HEADS, HEAD_DIM), 1.0),
        'state_conv': nrm(ks[4], (DEPTH, DEC_BATCH, CONV_WIDTH - 1, CONV_DIM), 1.0),
        'g_mix_pre': gain(ks[5]),
        'g_mix_post': gain(ks[6]),
        'g_mlp_pre': gain(ks[7]),
        'g_mlp_post': gain(ks[8]),
        'w_in': nrm(ks[9], (DEPTH, D_MODEL, IN_DIM), D_MODEL ** -0.5),
        'attn_sinks': nrm(ks[10], (DEPTH, N_HEADS), 0.5),
        'conv_w': nrm(ks[11], (DEPTH, CONV_WIDTH, CONV_DIM), CONV_WIDTH ** -0.5),
        'w_attn_o': nrm(ks[12], (DEPTH, ATTN_DIM, D_MODEL), ATTN_DIM ** -0.5),
        'w_conv_o': nrm(ks[13], (DEPTH, CONV_DIM, D_MODEL), CONV_DIM ** -0.5),
        'w_out': nrm(ks[14], (DEPTH, D_MODEL, D_MODEL), D_MODEL ** -0.5),
        'w_up': nrm(ks[15], (DEPTH, D_MODEL, D_FF), D_MODEL ** -0.5),
        'w_down': nrm(ks[16], (DEPTH, D_FF, D_MODEL), D_FF ** -0.5),
    }


def reference(x_prompt, x_sample, cache_k, cache_v, state_conv, g_mix_pre, g_mix_post,
              g_mlp_pre, g_mlp_post, w_in, attn_sinks, conv_w, w_attn_o, w_conv_o, w_out,
              w_up, w_down):
    yp, ys = x_prompt, x_sample
    kp, vp, cp, kd, vd, cd = [], [], [], [], [], []
    for l in range(DEPTH):
        params = (g_mix_pre[l], g_mix_post[l], g_mlp_pre[l], g_mlp_post[l], w_in[l],
                  attn_sinks[l], conv_w[l], w_attn_o[l], w_conv_o[l], w_out[l], w_up[l], w_down[l])
        yp, nk, nv, nc = decoder_layer(yp, None, None, None, *params)
        kp.append(nk); vp.append(nv); cp.append(nc)
        ys, nk, nv, nc = decoder_layer(ys, cache_k[l], cache_v[l], state_conv[l], *params)
        kd.append(nk); vd.append(nv); cd.append(nc)
    new_k_prompt = jnp.stack(kp)
    new_v_prompt = jnp.stack(vp)
    new_conv_prompt = jnp.stack(cp)
    new_k_sample = jnp.stack(kd)
    new_v_sample = jnp.stack(vd)
    new_conv_sample = jnp.stack(cd)
    return (yp, ys, new_k_prompt, new_v_prompt, new_conv_prompt, new_k_sample, new_v_sample, new_conv_sample)
```

```python
import functools

import jax
import jax.numpy as jnp
from jax import lax
from jax.experimental import pallas as pl
from jax.experimental.pallas import tpu as pltpu

_D = 1024
_HD = 64
_NH = 8
_ATT = _NH * _HD
_KVD = 128
_CONV = 512
_DFF = 4096
_WIN = 128
_EPS = 1e-6
_LANES = 128
_SUB = 8

_OFF_Q = 0
_OFF_KV = _ATT
_OFF_B = _OFF_KV + 2 * _KVD
_OFF_C = _OFF_B + _CONV
_OFF_U = _OFF_C + _CONV
_OFF_GA = _OFF_U + _CONV
_OFF_GC = _OFF_GA + _D
_IN_DIM = _OFF_GC + _D

_PROMPT_TM = 256
_SAMPLE_SB = 32
_MLP_TM = 256
_VMEM_LIMIT = 56 * 1024 * 1024

_BF = jnp.bfloat16
_F32 = jnp.float32


def _rms(x, g):
    return x * lax.rsqrt(jnp.mean(x * x, axis=-1, keepdims=True) + _EPS) * g


def _proj(h_bf, win_ref, off, width):
    return jnp.dot(h_bf, win_ref[:, off:off + width], preferred_element_type=_F32)


def _dup_halves(val, lo):
    rolled = pltpu.roll(val, _HD, axis=1)
    return jnp.where(lo, val, rolled), jnp.where(lo, rolled, val)


def _band_mask(nq, nk):
    qi = lax.broadcasted_iota(jnp.int32, (nq, nk), 0)
    kj = lax.broadcasted_iota(jnp.int32, (nq, nk), 1)
    return (kj > qi) & (kj <= qi + _WIN), kj


def _attend_group(q4, kd, vd, bias, sinks, nq):
    s = lax.dot_general(q4, kd, (((1,), (1,)), ((), ())), preferred_element_type=_F32)
    es, rden = [], []
    for j in range(4):
        sj = s[j * nq:(j + 1) * nq] + bias
        m = jnp.maximum(jnp.max(sj, axis=-1, keepdims=True), sinks[j])
        e = jnp.exp(sj - m)
        den = jnp.sum(e, axis=-1, keepdims=True) + jnp.exp(sinks[j] - m)
        es.append(e)
        rden.append(1.0 / den)
    e_all = jnp.concatenate(es, axis=0).astype(_BF)
    o = jnp.dot(e_all, vd, preferred_element_type=_F32)
    return [o[j * nq:(j + 1) * nq] * rden[j] for j in range(4)]


def _mix_and_project(x, h, attn_bf, u_shift2, u_shift1, u, win_ref, convw_ref, wao_ref, wco_ref, wout_ref,
                     gpost_ref):
    attn_b = jnp.dot(attn_bf, wao_ref[...], preferred_element_type=_F32)
    mixed = jax.nn.sigmoid(_proj(h, win_ref, _OFF_GA, _D)) * attn_b
    cw = convw_ref[...]
    z = cw[0:1] * u_shift2
    z = z + cw[1:2] * u_shift1
    z = z + cw[2:3] * u
    bz = (_proj(h, win_ref, _OFF_B, _CONV) * z).astype(_BF)
    conv_b = jnp.dot(bz, wco_ref[...], preferred_element_type=_F32)
    mixed = mixed + jax.nn.sigmoid(_proj(h, win_ref, _OFF_GC, _D)) * conv_b
    mo = jnp.dot(mixed.astype(_BF), wout_ref[...], preferred_element_type=_F32)
    return x + _rms(mo, gpost_ref[...])


def _mixer_prompt_kernel(sink_ref, x_ref, gpre_ref, gpost_ref, convw_ref, win_ref, wao_ref, wco_ref, wout_ref,
                         y_ref, nk_ref, nv_ref, nc_ref,
                         kdup, vdup, q4s, ubuf, attn_scr, *, tm):
    s = pl.program_id(1)
    last = pl.num_programs(1) - 1

    @pl.when(s == 0)
    def _():
        kdup[:, 0:_WIN, :] = jnp.zeros((2, _WIN, _LANES), _BF)
        vdup[:, 0:_WIN, :] = jnp.zeros((2, _WIN, _LANES), _BF)
        ubuf[0:_SUB, :] = jnp.zeros((_SUB, _CONV), _F32)

    x = x_ref[...]
    h = _rms(x, gpre_ref[...]).astype(_BF)
    lo = lax.broadcasted_iota(jnp.int32, (tm, _LANES), 1) < _HD

    kv = _proj(h, win_ref, _OFF_KV, 2 * _KVD)
    k = kv[:, :_KVD]
    v = kv[:, _KVD:]
    for val, dst in ((k, kdup), (v, vdup)):
        d0, d1 = _dup_halves(val, lo)
        dst[0, _WIN:_WIN + tm, :] = d0.astype(_BF)
        dst[1, _WIN:_WIN + tm, :] = d1.astype(_BF)

    q = _proj(h, win_ref, _OFF_Q, _ATT) * (_HD ** -0.5)
    for c in range(4):
        qc = q[:, c * _LANES:(c + 1) * _LANES]
        q4s[c, 0] = jnp.where(lo, qc, 0.0).astype(_BF)
        q4s[c, 1] = jnp.where(lo, 0.0, qc).astype(_BF)

    band, kj = _band_mask(_WIN, 2 * _WIN)
    neg = jnp.float32(-jnp.inf)
    bias = jnp.where(band, 0.0, neg)
    bias_first = jnp.where(band & ((kj >= _WIN) | (s > 0)), 0.0, neg)
    lo_q = lax.broadcasted_iota(jnp.int32, (_WIN, _LANES), 1) < _HD
    for qb in range(tm // _WIN):
        r0 = qb * _WIN
        b = bias_first if qb == 0 else bias
        for g in range(2):
            q4 = jnp.concatenate([q4s[2 * g, 0, r0:r0 + _WIN, :], q4s[2 * g, 1, r0:r0 + _WIN, :],
                                  q4s[2 * g + 1, 0, r0:r0 + _WIN, :], q4s[2 * g + 1, 1, r0:r0 + _WIN, :]], axis=0)
            sinks = [sink_ref[4 * g + j] for j in range(4)]
            o = _attend_group(q4, kdup[g, r0:r0 + 2 * _WIN, :], vdup[g, r0:r0 + 2 * _WIN, :], b, sinks, _WIN)
            attn_scr[r0:r0 + _WIN, (2 * g) * _LANES:(2 * g + 1) * _LANES] = jnp.where(lo_q, o[0], o[1]).astype(_BF)
            attn_scr[r0:r0 + _WIN, (2 * g + 1) * _LANES:(2 * g + 2) * _LANES] = (
                jnp.where(lo_q, o[2], o[3]).astype(_BF))

    kdup[:, 0:_WIN, :] = kdup[:, tm:tm + _WIN, :]
    vdup[:, 0:_WIN, :] = vdup[:, tm:tm + _WIN, :]

    u = _proj(h, win_ref, _OFF_C, _CONV) * _proj(h, win_ref, _OFF_U, _CONV)
    ubuf[_SUB:_SUB + tm, :] = u
    y_ref[...] = _mix_and_project(x, h, attn_scr[...], ubuf[_SUB - 2:_SUB - 2 + tm, :],
                                  ubuf[_SUB - 1:_SUB - 1 + tm, :], u,
                                  win_ref, convw_ref, wao_ref, wco_ref, wout_ref, gpost_ref)
    ubuf[0:_SUB, :] = ubuf[tm:tm + _SUB, :]

    @pl.when(s == last)
    def _():
        nk_ref[...] = k[tm - _WIN:tm, :]
        nv_ref[...] = v[tm - _WIN:tm, :]
        nc_ref[...] = u[tm - 2:tm, :]


def _mixer_sample_kernel(sink_ref, x_ref, gpre_ref, gpost_ref, convw_ref, ck_ref, cv_ref, st_ref,
                         win_ref, wao_ref, wco_ref, wout_ref,
                         y_ref, nk_ref, nv_ref, nc_ref,
                         q4s, knew, vnew, ubuf, attn_scr, *, sb, t):
    tr = sb * t
    x = x_ref[...]
    h = _rms(x, gpre_ref[...]).astype(_BF)
    lo = lax.broadcasted_iota(jnp.int32, (tr, _LANES), 1) < _HD

    kv = _proj(h, win_ref, _OFF_KV, 2 * _KVD)
    k = kv[:, :_KVD]
    v = kv[:, _KVD:]
    knew[...] = k
    vnew[...] = v
    q = _proj(h, win_ref, _OFF_Q, _ATT) * (_HD ** -0.5)
    for c in range(4):
        qc = q[:, c * _LANES:(c + 1) * _LANES]
        q4s[c, 0] = jnp.where(lo, qc, 0.0)
        q4s[c, 1] = jnp.where(lo, 0.0, qc)

    nk_ref[:, 0:_WIN - t, :] = ck_ref[:, t:_WIN, :]
    nv_ref[:, 0:_WIN - t, :] = cv_ref[:, t:_WIN, :]
    nk_ref[:, _WIN - t:_WIN, :] = k.reshape(sb, t, _KVD)
    nv_ref[:, _WIN - t:_WIN, :] = v.reshape(sb, t, _KVD)

    nkeys = 2 * _WIN
    band, _ = _band_mask(t, nkeys)
    bias = jnp.where(band, 0.0, jnp.float32(-jnp.inf))
    lo_k = lax.broadcasted_iota(jnp.int32, (nkeys, _LANES), 1) < _HD
    lo_q = lax.broadcasted_iota(jnp.int32, (t, _LANES), 1) < _HD
    pad = jnp.zeros((nkeys - _WIN - t, _LANES), _F32)

    def per_seq(b, carry):
        r = pl.multiple_of(b * t, t)
        kfull = jnp.concatenate([ck_ref[b], knew[pl.ds(r, t), :], pad], axis=0)
        vfull = jnp.concatenate([cv_ref[b], vnew[pl.ds(r, t), :], pad], axis=0)
        kds = _dup_halves(kfull, lo_k)
        vds = _dup_halves(vfull, lo_k)
        for g in range(2):
            q4 = jnp.concatenate([q4s[2 * g, 0, pl.ds(r, t), :], q4s[2 * g, 1, pl.ds(r, t), :],
                                  q4s[2 * g + 1, 0, pl.ds(r, t), :], q4s[2 * g + 1, 1, pl.ds(r, t), :]],
                                 axis=0).astype(_BF)
            sinks = [sink_ref[4 * g + j] for j in range(4)]
            o = _attend_group(q4, kds[g].astype(_BF), vds[g].astype(_BF), bias, sinks, t)
            attn_scr[pl.ds(r, t), (2 * g) * _LANES:(2 * g + 1) * _LANES] = jnp.where(lo_q, o[0], o[1])
            attn_scr[pl.ds(r, t), (2 * g + 1) * _LANES:(2 * g + 2) * _LANES] = jnp.where(lo_q, o[2], o[3])
        return carry

    lax.fori_loop(0, sb, per_seq, 0)

    u = _proj(h, win_ref, _OFF_C, _CONV) * _proj(h, win_ref, _OFF_U, _CONV)
    u3 = u.reshape(sb, t, _CONV)
    ubuf[:, _SUB - 2:_SUB, :] = st_ref[...]
    ubuf[:, _SUB:_SUB + t, :] = u3
    us2 = ubuf[:, _SUB - 2:_SUB - 2 + t, :].reshape(tr, _CONV)
    us1 = ubuf[:, _SUB - 1:_SUB - 1 + t, :].reshape(tr, _CONV)
    nc_ref[...] = u3[:, t - 2:t, :]
    y_ref[...] = _mix_and_project(x, h, attn_scr[...].astype(_BF), us2, us1, u,
                                  win_ref, convw_ref, wao_ref, wco_ref, wout_ref, gpost_ref)


def _mlp_kernel(x_ref, gpre_ref, gpost_ref, wup_ref, wdown_ref, y_ref):
    x = x_ref[...]
    hm = _rms(x, gpre_ref[...]).astype(_BF)
    acc = None
    for j in range(_DFF // _D):
        hj = jnp.dot(hm, wup_ref[:, j * _D:(j + 1) * _D], preferred_element_type=_F32)
        hj = jnp.square(jnp.maximum(hj, 0.0)).astype(_BF)
        part = jnp.dot(hj, wdown_ref[j * _D:(j + 1) * _D, :], preferred_element_type=_F32)
        acc = part if acc is None else acc + part
    y_ref[...] = x + _rms(acc, gpost_ref[...])


def _full(shape):
    return pl.BlockSpec(shape, lambda *_: (0,) * len(shape))


_SMEM_SPEC = pl.BlockSpec(memory_space=pltpu.SMEM)


def _mixer_prompt(x, sinks, gpre, gpost, convw, win, wao, wco, wout):
    n, seq, _ = x.shape
    tm = _PROMPT_TM
    assert seq % tm == 0 and tm % _WIN == 0 and seq >= _WIN
    tile = lambda b, s: (b, s, 0)
    per_b = lambda b, s: (b, 0, 0)
    return pl.pallas_call(
        functools.partial(_mixer_prompt_kernel, tm=tm),
        grid=(n, seq // tm),
        in_specs=[_SMEM_SPEC,
                  pl.BlockSpec((None, tm, _D), tile),
                  _full((1, _D)), _full((1, _D)), _full((3, _CONV)),
                  _full((_D, _IN_DIM)), _full((_ATT, _D)), _full((_CONV, _D)), _full((_D, _D))],
        out_specs=[pl.BlockSpec((None, tm, _D), tile),
                   pl.BlockSpec((None, _WIN, _KVD), per_b),
                   pl.BlockSpec((None, _WIN, _KVD), per_b),
                   pl.BlockSpec((None, 2, _CONV), per_b)],
        out_shape=[jax.ShapeDtypeStruct((n, seq, _D), _F32),
                   jax.ShapeDtypeStruct((n, _WIN, _KVD), _F32),
                   jax.ShapeDtypeStruct((n, _WIN, _KVD), _F32),
                   jax.ShapeDtypeStruct((n, 2, _CONV), _F32)],
        scratch_shapes=[pltpu.VMEM((2, _WIN + tm, _LANES), _BF),
                        pltpu.VMEM((2, _WIN + tm, _LANES), _BF),
                        pltpu.VMEM((4, 2, tm, _LANES), _BF),
                        pltpu.VMEM((_SUB + tm, _CONV), _F32),
                        pltpu.VMEM((tm, _ATT), _BF)],
        compiler_params=pltpu.CompilerParams(dimension_semantics=("parallel", "arbitrary"),
                                             vmem_limit_bytes=_VMEM_LIMIT),
        name="mixer_prompt",
    )(sinks, x, gpre, gpost, convw, win, wao, wco, wout)


def _mixer_sample(x, ck, cv, st, sinks, gpre, gpost, convw, win, wao, wco, wout):
    n, t, _ = x.shape
    sb = _SAMPLE_SB
    assert n % sb == 0 and t == _SUB and ck.shape == (n, _WIN, _KVD)
    tr = sb * t
    rows = lambda i: (i, 0)
    seqs = lambda i: (i, 0, 0)
    y, nk, nv, nc = pl.pallas_call(
        functools.partial(_mixer_sample_kernel, sb=sb, t=t),
        grid=(n // sb,),
        in_specs=[_SMEM_SPEC,
                  pl.BlockSpec((tr, _D), rows),
                  _full((1, _D)), _full((1, _D)), _full((3, _CONV)),
                  pl.BlockSpec((sb, _WIN, _KVD), seqs), pl.BlockSpec((sb, _WIN, _KVD), seqs),
                  pl.BlockSpec((sb, 2, _CONV), seqs),
                  _full((_D, _IN_DIM)), _full((_ATT, _D)), _full((_CONV, _D)), _full((_D, _D))],
        out_specs=[pl.BlockSpec((tr, _D), rows),
                   pl.BlockSpec((sb, _WIN, _KVD), seqs), pl.BlockSpec((sb, _WIN, _KVD), seqs),
                   pl.BlockSpec((sb, 2, _CONV), seqs)],
        out_shape=[jax.ShapeDtypeStruct((n * t, _D), _F32),
                   jax.ShapeDtypeStruct((n, _WIN, _KVD), _F32),
                   jax.ShapeDtypeStruct((n, _WIN, _KVD), _F32),
                   jax.ShapeDtypeStruct((n, 2, _CONV), _F32)],
        scratch_shapes=[pltpu.VMEM((4, 2, tr, _LANES), _F32),
                        pltpu.VMEM((tr, _KVD), _F32),
                        pltpu.VMEM((tr, _KVD), _F32),
                        pltpu.VMEM((sb, 2 * _SUB, _CONV), _F32),
                        pltpu.VMEM((tr, _ATT), _F32)],
        compiler_params=pltpu.CompilerParams(dimension_semantics=("arbitrary",),
                                             vmem_limit_bytes=_VMEM_LIMIT),
        name="mixer_sample",
    )(sinks, x.reshape(n * t, _D), gpre, gpost, convw, ck, cv, st, win, wao, wco, wout)
    return y.reshape(n, t, _D), nk, nv, nc


def _mlp(x, gpre, gpost, wup, wdown):
    shape = x.shape
    x2 = x.reshape(-1, _D)
    nt = x2.shape[0]
    tm = _MLP_TM
    assert nt % tm == 0
    rows = lambda i: (i, 0)
    y = pl.pallas_call(
        _mlp_kernel,
        grid=(nt // tm,),
        in_specs=[pl.BlockSpec((tm, _D), rows), _full((1, _D)), _full((1, _D)),
                  _full((_D, _DFF)), _full((_DFF, _D))],
        out_specs=pl.BlockSpec((tm, _D), rows),
        out_shape=jax.ShapeDtypeStruct((nt, _D), _F32),
        compiler_params=pltpu.CompilerParams(dimension_semantics=("parallel",),
                                             vmem_limit_bytes=_VMEM_LIMIT),
        name="mlp",
    )(x2, gpre, gpost, wup, wdown)
    return y.reshape(shape)


def kernel(x_prompt, x_sample, cache_k, cache_v, state_conv, g_mix_pre, g_mix_post, g_mlp_pre, g_mlp_post,
           w_in, attn_sinks, conv_w, w_attn_o, w_conv_o, w_out, w_up, w_down):
    depth = w_in.shape[0]
    n_dec = x_sample.shape[0]
    ck_all = cache_k.reshape(depth, n_dec, _WIN, _KVD)
    cv_all = cache_v.reshape(depth, n_dec, _WIN, _KVD)
    yp, ys = x_prompt, x_sample
    kp, vp, cp, kd, vd, cd = [], [], [], [], [], []
    for l in range(depth):
        win, wao, wco, wout = (w[l].astype(_BF) for w in (w_in, w_attn_o, w_conv_o, w_out))
        wup, wdown = w_up[l].astype(_BF), w_down[l].astype(_BF)
        gpre, gpost = g_mix_pre[l][None], g_mix_post[l][None]
        gmpre, gmpost = g_mlp_pre[l][None], g_mlp_post[l][None]
        mix_args = (attn_sinks[l], gpre, gpost, conv_w[l], win, wao, wco, wout)

        yp, nk, nv, nc = _mixer_prompt(yp, *mix_args)
        yp = _mlp(yp, gmpre, gmpost, wup, wdown)
        kp.append(nk); vp.append(nv); cp.append(nc)

        ys, nk, nv, nc = _mixer_sample(ys, ck_all[l], cv_all[l], state_conv[l], *mix_args)
        ys = _mlp(ys, gmpre, gmpost, wup, wdown)
        kd.append(nk); vd.append(nv); cd.append(nc)

    kv_shape = lambda n: (depth, n, _WIN, 2, _HD)
    n_p = x_prompt.shape[0]
    return (yp, ys,
            jnp.stack(kp).reshape(kv_shape(n_p)), jnp.stack(vp).reshape(kv_shape(n_p)), jnp.stack(cp),
            jnp.stack(kd).reshape(kv_shape(n_dec)), jnp.stack(vd).reshape(kv_shape(n_dec)), jnp.stack(cd))
```

```python
import functools

import jax
import jax.numpy as jnp
from jax import lax
from jax.experimental import pallas as pl
from jax.experimental.pallas import tpu as pltpu

_D = 1024
_HD = 64
_NH = 8
_ATT = _NH * _HD
_KVD = 128
_CONV = 512
_DFF = 4096
_WIN = 128
_EPS = 1e-6
_LANES = 128
_SUB = 8
_BF_ROWS = 16

_OFF_Q = 0
_OFF_KV = _ATT
_OFF_B = _OFF_KV + 2 * _KVD
_OFF_C = _OFF_B + _CONV
_OFF_U = _OFF_C + _CONV
_OFF_GA = _OFF_U + _CONV
_OFF_GC = _OFF_GA + _D
_IN_DIM = _OFF_GC + _D

_PROMPT_TM = 256
_SAMPLE_SB = 32
_SAMPLE_UNROLL = 4
_MLP_TM = 256
_VMEM_LIMIT = 56 * 1024 * 1024

_BF = jnp.bfloat16
_F32 = jnp.float32


def _rms(x, g):
    return x * lax.rsqrt(jnp.mean(x * x, axis=-1, keepdims=True) + _EPS) * g


def _proj(h_bf, win_ref, off, width):
    return jnp.dot(h_bf, win_ref[:, off:off + width], preferred_element_type=_F32)


def _dup_halves(val, lo):
    rolled = pltpu.roll(val, _HD, axis=1)
    return jnp.where(lo, val, rolled), jnp.where(lo, rolled, val)


def _band_mask(qi, kj):
    return (kj > qi) & (kj <= qi + _WIN)


def _dot_nt(a, b):
    return lax.dot_general(a, b, (((1,), (1,)), ((), ())), preferred_element_type=_F32)


def _attend_group(q4, kd, vd, bias, sinks, nq):
    s = _dot_nt(q4, kd)
    es, rden = [], []
    for j in range(4):
        sj = s[j * nq:(j + 1) * nq] + bias
        m = jnp.maximum(jnp.max(sj, axis=-1, keepdims=True), sinks[j])
        e = jnp.exp(sj - m)
        den = jnp.sum(e, axis=-1, keepdims=True) + jnp.exp(sinks[j] - m)
        es.append(e)
        rden.append(1.0 / den)
    e_all = jnp.concatenate(es, axis=0).astype(_BF)
    o = jnp.dot(e_all, vd, preferred_element_type=_F32)
    return [o[j * nq:(j + 1) * nq] * rden[j] for j in range(4)]


def _mix_and_project(x, h, attn_bf, u_shift2, u_shift1, u, win_ref, convw_ref, wao_ref, wco_ref, wout_ref,
                     gpost_ref):
    attn_b = jnp.dot(attn_bf, wao_ref[...], preferred_element_type=_F32)
    mixed = jax.nn.sigmoid(_proj(h, win_ref, _OFF_GA, _D)) * attn_b
    cw = convw_ref[...]
    z = cw[0:1] * u_shift2
    z = z + cw[1:2] * u_shift1
    z = z + cw[2:3] * u
    bz = (_proj(h, win_ref, _OFF_B, _CONV) * z).astype(_BF)
    conv_b = jnp.dot(bz, wco_ref[...], preferred_element_type=_F32)
    mixed = mixed + jax.nn.sigmoid(_proj(h, win_ref, _OFF_GC, _D)) * conv_b
    mo = jnp.dot(mixed.astype(_BF), wout_ref[...], preferred_element_type=_F32)
    return x + _rms(mo, gpost_ref[...])


def _mixer_prompt_kernel(sink_ref, x_ref, gpre_ref, gpost_ref, convw_ref, win_ref, wao_ref, wco_ref, wout_ref,
                         *rest, tm, layer, n_alias):
    y_ref, nk_ref, nv_ref, nc_ref, kdup, vdup, q4s, ubuf, attn_scr = rest[n_alias:]
    s = pl.program_id(1)
    last = pl.num_programs(1) - 1

    @pl.when(s == 0)
    def _():
        kdup[:, 0:_WIN, :] = jnp.zeros((2, _WIN, _LANES), _BF)
        vdup[:, 0:_WIN, :] = jnp.zeros((2, _WIN, _LANES), _BF)
        ubuf[0:_SUB, :] = jnp.zeros((_SUB, _CONV), _F32)

    x = x_ref[...]
    h = _rms(x, gpre_ref[...]).astype(_BF)
    lo = lax.broadcasted_iota(jnp.int32, (tm, _LANES), 1) < _HD

    kv = _proj(h, win_ref, _OFF_KV, 2 * _KVD)
    k = kv[:, :_KVD]
    v = kv[:, _KVD:]
    for val, dst in ((k, kdup), (v, vdup)):
        d0, d1 = _dup_halves(val, lo)
        dst[0, _WIN:_WIN + tm, :] = d0.astype(_BF)
        dst[1, _WIN:_WIN + tm, :] = d1.astype(_BF)

    q = _proj(h, win_ref, _OFF_Q, _ATT) * (_HD ** -0.5)
    for c in range(4):
        qc = q[:, c * _LANES:(c + 1) * _LANES]
        q4s[c, 0] = jnp.where(lo, qc, 0.0).astype(_BF)
        q4s[c, 1] = jnp.where(lo, 0.0, qc).astype(_BF)

    qi = lax.broadcasted_iota(jnp.int32, (_WIN, 2 * _WIN), 0)
    kj = lax.broadcasted_iota(jnp.int32, (_WIN, 2 * _WIN), 1)
    band = _band_mask(qi, kj)
    neg = jnp.float32(-jnp.inf)
    bias = jnp.where(band, 0.0, neg)
    bias_first = jnp.where(band & ((kj >= _WIN) | (s > 0)), 0.0, neg)
    lo_q = lax.broadcasted_iota(jnp.int32, (_WIN, _LANES), 1) < _HD
    for qb in range(tm // _WIN):
        r0 = qb * _WIN
        b = bias_first if qb == 0 else bias
        for g in range(2):
            q4 = jnp.concatenate([q4s[2 * g, 0, r0:r0 + _WIN, :], q4s[2 * g, 1, r0:r0 + _WIN, :],
                                  q4s[2 * g + 1, 0, r0:r0 + _WIN, :], q4s[2 * g + 1, 1, r0:r0 + _WIN, :]], axis=0)
            sinks = [sink_ref[layer, 4 * g + j] for j in range(4)]
            o = _attend_group(q4, kdup[g, r0:r0 + 2 * _WIN, :], vdup[g, r0:r0 + 2 * _WIN, :], b, sinks, _WIN)
            attn_scr[r0:r0 + _WIN, (2 * g) * _LANES:(2 * g + 1) * _LANES] = jnp.where(lo_q, o[0], o[1]).astype(_BF)
            attn_scr[r0:r0 + _WIN, (2 * g + 1) * _LANES:(2 * g + 2) * _LANES] = (
                jnp.where(lo_q, o[2], o[3]).astype(_BF))

    kdup[:, 0:_WIN, :] = kdup[:, tm:tm + _WIN, :]
    vdup[:, 0:_WIN, :] = vdup[:, tm:tm + _WIN, :]

    u = _proj(h, win_ref, _OFF_C, _CONV) * _proj(h, win_ref, _OFF_U, _CONV)
    ubuf[_SUB:_SUB + tm, :] = u
    y_ref[...] = _mix_and_project(x, h, attn_scr[...], ubuf[_SUB - 2:_SUB - 2 + tm, :],
                                  ubuf[_SUB - 1:_SUB - 1 + tm, :], u,
                                  win_ref, convw_ref, wao_ref, wco_ref, wout_ref, gpost_ref)
    ubuf[0:_SUB, :] = ubuf[tm:tm + _SUB, :]

    @pl.when(s == last)
    def _():
        nk_ref[...] = k[tm - _WIN:tm, :]
        nv_ref[...] = v[tm - _WIN:tm, :]
        nc_ref[...] = u[tm - 2:tm, :]


def _mixer_sample_kernel(sink_ref, x_ref, gpre_ref, gpost_ref, convw_ref, ck_ref, cv_ref, st_ref,
                         win_ref, wao_ref, wco_ref, wout_ref, *rest, sb, t, layer, n_alias):
    y_ref, nk_ref, nv_ref, nc_ref, q_scr, knew, vnew, s_scr, e_scr, o_scr, ubuf = rest[n_alias:]
    tr = sb * t
    nkeys = _WIN + _BF_ROWS
    x = x_ref[...]
    h = _rms(x, gpre_ref[...]).astype(_BF)
    lo = lax.broadcasted_iota(jnp.int32, (tr, _LANES), 1) < _HD

    kv = _proj(h, win_ref, _OFF_KV, 2 * _KVD)
    k = kv[:, :_KVD]
    v = kv[:, _KVD:]
    knew[...] = k
    vnew[...] = v
    q = _proj(h, win_ref, _OFF_Q, _ATT) * (_HD ** -0.5)
    pieces = []
    for hd in range(_NH):
        qc = q[:, (hd // 2) * _LANES:(hd // 2 + 1) * _LANES]
        piece = jnp.where(lo, qc, 0.0) if hd % 2 == 0 else jnp.where(lo, 0.0, qc)
        pieces.append(piece.reshape(sb, t, _LANES))
    q_scr[...] = jnp.concatenate(pieces, axis=1).astype(_BF)

    nk_ref[:, 0:_WIN - t, :] = ck_ref[:, t:_WIN, :]
    nv_ref[:, 0:_WIN - t, :] = cv_ref[:, t:_WIN, :]
    nk_ref[:, _WIN - t:_WIN, :] = k.reshape(sb, t, _KVD)
    nv_ref[:, _WIN - t:_WIN, :] = v.reshape(sb, t, _KVD)

    lo_k = lax.broadcasted_iota(jnp.int32, (nkeys, _LANES), 1) < _HD
    pad = jnp.zeros((nkeys - _WIN - t, _LANES), _F32)
    half = _NH * t // 2

    def dup_keys(cache_ref, new_ref, b):
        r = pl.multiple_of(b * t, t)
        full = jnp.concatenate([cache_ref[b], new_ref[pl.ds(r, t), :], pad], axis=0)
        return [d.astype(_BF) for d in _dup_halves(full, lo_k)]

    def qk_body(b, carry):
        kds = dup_keys(ck_ref, knew, b)
        for g in range(2):
            s_scr[b, g * half:(g + 1) * half, :] = _dot_nt(q_scr[b, g * half:(g + 1) * half, :], kds[g])
        return carry

    lax.fori_loop(0, sb, qk_body, 0, unroll=_SAMPLE_UNROLL)

    rows = _NH * t
    qi = lax.broadcasted_iota(jnp.int32, (rows, nkeys), 0) & (t - 1)
    kj = lax.broadcasted_iota(jnp.int32, (rows, nkeys), 1)
    bias = jnp.where(_band_mask(qi, kj), 0.0, jnp.float32(-jnp.inf))
    sink_col = jnp.concatenate([jnp.full((t, 1), sink_ref[layer, hd], _F32) for hd in range(_NH)], axis=0)
    s = s_scr[...] + bias[None]
    m = jnp.maximum(jnp.max(s, axis=-1, keepdims=True), sink_col[None])
    e = jnp.exp(s - m)
    rden = 1.0 / (jnp.sum(e, axis=-1, keepdims=True) + jnp.exp(sink_col[None] - m))
    e_scr[...] = e.astype(_BF)

    def pv_body(b, carry):
        vds = dup_keys(cv_ref, vnew, b)
        for g in range(2):
            o_scr[b, g * half:(g + 1) * half, :] = jnp.dot(e_scr[b, g * half:(g + 1) * half, :], vds[g],
                                                           preferred_element_type=_F32)
        return carry

    lax.fori_loop(0, sb, pv_body, 0, unroll=_SAMPLE_UNROLL)

    o = o_scr[...] * rden
    lo3 = lax.broadcasted_iota(jnp.int32, (sb, t, _LANES), 2) < _HD
    attn = jnp.concatenate(
        [jnp.where(lo3, o[:, (2 * c) * t:(2 * c + 1) * t, :], o[:, (2 * c + 1) * t:(2 * c + 2) * t, :]).reshape(tr, _LANES)
         for c in range(_NH // 2)], axis=1)

    u = _proj(h, win_ref, _OFF_C, _CONV) * _proj(h, win_ref, _OFF_U, _CONV)
    u3 = u.reshape(sb, t, _CONV)
    ubuf[:, _SUB - 2:_SUB, :] = st_ref[...]
    ubuf[:, _SUB:_SUB + t, :] = u3
    us2 = ubuf[:, _SUB - 2:_SUB - 2 + t, :].reshape(tr, _CONV)
    us1 = ubuf[:, _SUB - 1:_SUB - 1 + t, :].reshape(tr, _CONV)
    nc_ref[...] = u3[:, t - 2:t, :]
    y_ref[...] = _mix_and_project(x, h, attn.astype(_BF), us2, us1, u,
                                  win_ref, convw_ref, wao_ref, wco_ref, wout_ref, gpost_ref)


def _mlp_kernel(x_ref, gpre_ref, gpost_ref, wup_ref, wdown_ref, y_ref):
    x = x_ref[...]
    hm = _rms(x, gpre_ref[...]).astype(_BF)
    acc = None
    for j in range(_DFF // _D):
        hj = jnp.dot(hm, wup_ref[:, j * _D:(j + 1) * _D], preferred_element_type=_F32)
        hj = jnp.square(jnp.maximum(hj, 0.0)).astype(_BF)
        part = jnp.dot(hj, wdown_ref[j * _D:(j + 1) * _D, :], preferred_element_type=_F32)
        acc = part if acc is None else acc + part
    y_ref[...] = x + _rms(acc, gpost_ref[...])


def _layer_spec(shape, layer):
    return pl.BlockSpec((None,) + shape, lambda *_: (layer,) + (0,) * len(shape))


_SMEM_SPEC = pl.BlockSpec(memory_space=pltpu.SMEM)
_ANY_SPEC = pl.BlockSpec(memory_space=pl.ANY)


def _mixer_weight_specs(layer):
    return [_layer_spec((_D, _IN_DIM), layer), _layer_spec((_ATT, _D), layer),
            _layer_spec((_CONV, _D), layer), _layer_spec((_D, _D), layer)]


def _alias_args(prev, n_inputs):
    if prev is None:
        return [], [], {}
    return list(prev), [_ANY_SPEC] * len(prev), {n_inputs + i: 1 + i for i in range(len(prev))}


def _mixer_prompt(x, layer, prev, sinks, gpre, gpost, convw, win, wao, wco, wout):
    n, seq, _ = x.shape
    depth = win.shape[0]
    tm = _PROMPT_TM
    assert seq % tm == 0 and tm % _WIN == 0 and seq >= _WIN
    tile = lambda b, s: (b, s, 0)
    per_b = lambda b, s: (layer, b, 0, 0)
    inputs = [sinks, x, gpre, gpost, convw, win, wao, wco, wout]
    alias_in, alias_specs, aliases = _alias_args(prev, len(inputs))
    out = pl.pallas_call(
        functools.partial(_mixer_prompt_kernel, tm=tm, layer=layer, n_alias=len(alias_in)),
        grid=(n, seq // tm),
        in_specs=[_SMEM_SPEC,
                  pl.BlockSpec((None, tm, _D), tile),
                  _layer_spec((1, _D), layer), _layer_spec((1, _D), layer), _layer_spec((3, _CONV), layer)]
                 + _mixer_weight_specs(layer) + alias_specs,
        out_specs=[pl.BlockSpec((None, tm, _D), tile),
                   pl.BlockSpec((None, None, _WIN, _KVD), per_b),
                   pl.BlockSpec((None, None, _WIN, _KVD), per_b),
                   pl.BlockSpec((None, None, 2, _CONV), per_b)],
        out_shape=[jax.ShapeDtypeStruct((n, seq, _D), _F32),
                   jax.ShapeDtypeStruct((depth, n, _WIN, _KVD), _F32),
                   jax.ShapeDtypeStruct((depth, n, _WIN, _KVD), _F32),
                   jax.ShapeDtypeStruct((depth, n, 2, _CONV), _F32)],
        scratch_shapes=[pltpu.VMEM((2, _WIN + tm, _LANES), _BF),
                        pltpu.VMEM((2, _WIN + tm, _LANES), _BF),
                        pltpu.VMEM((4, 2, tm, _LANES), _BF),
                        pltpu.VMEM((_SUB + tm, _CONV), _F32),
                        pltpu.VMEM((tm, _ATT), _BF)],
        input_output_aliases=aliases,
        compiler_params=pltpu.CompilerParams(dimension_semantics=("parallel", "arbitrary"),
                                             vmem_limit_bytes=_VMEM_LIMIT),
        name="mixer_prompt",
    )(*inputs, *alias_in)
    return out[0], tuple(out[1:])


def _mixer_sample(x, layer, prev, ck, cv, st, sinks, gpre, gpost, convw, win, wao, wco, wout):
    n, t, _ = x.shape
    depth = win.shape[0]
    sb = _SAMPLE_SB
    assert n % sb == 0 and t == _SUB and ck.shape == (depth, n, _WIN, _KVD)
    tr = sb * t
    nkeys = _WIN + _BF_ROWS
    rows = lambda i: (i, 0)
    seqs = lambda i: (layer, i, 0, 0)
    cache_spec = pl.BlockSpec((None, sb, _WIN, _KVD), seqs)
    state_spec = pl.BlockSpec((None, sb, 2, _CONV), seqs)
    inputs = [sinks, x.reshape(n * t, _D), gpre, gpost, convw, ck, cv, st, win, wao, wco, wout]
    alias_in, alias_specs, aliases = _alias_args(prev, len(inputs))
    out = pl.pallas_call(
        functools.partial(_mixer_sample_kernel, sb=sb, t=t, layer=layer, n_alias=len(alias_in)),
        grid=(n // sb,),
        in_specs=[_SMEM_SPEC,
                  pl.BlockSpec((tr, _D), rows),
                  _layer_spec((1, _D), layer), _layer_spec((1, _D), layer), _layer_spec((3, _CONV), layer),
                  cache_spec, cache_spec, state_spec]
                 + _mixer_weight_specs(layer) + alias_specs,
        out_specs=[pl.BlockSpec((tr, _D), rows), cache_spec, cache_spec, state_spec],
        out_shape=[jax.ShapeDtypeStruct((n * t, _D), _F32),
                   jax.ShapeDtypeStruct((depth, n, _WIN, _KVD), _F32),
                   jax.ShapeDtypeStruct((depth, n, _WIN, _KVD), _F32),
                   jax.ShapeDtypeStruct((depth, n, 2, _CONV), _F32)],
        scratch_shapes=[pltpu.VMEM((sb, _NH * t, _LANES), _BF),
                        pltpu.VMEM((tr, _KVD), _F32),
                        pltpu.VMEM((tr, _KVD), _F32),
                        pltpu.VMEM((sb, _NH * t, nkeys), _F32),
                        pltpu.VMEM((sb, _NH * t, nkeys), _BF),
                        pltpu.VMEM((sb, _NH * t, _LANES), _F32),
                        pltpu.VMEM((sb, 2 * _SUB, _CONV), _F32)],
        input_output_aliases=aliases,
        compiler_params=pltpu.CompilerParams(dimension_semantics=("arbitrary",),
                                             vmem_limit_bytes=_VMEM_LIMIT),
        name="mixer_sample",
    )(*inputs, *alias_in)
    return out[0].reshape(n, t, _D), tuple(out[1:])


def _mlp(x, layer, gpre, gpost, wup, wdown):
    shape = x.shape
    x2 = x.reshape(-1, _D)
    nt = x2.shape[0]
    tm = _MLP_TM
    assert nt % tm == 0
    rows = lambda i: (i, 0)
    y = pl.pallas_call(
        _mlp_kernel,
        grid=(nt // tm,),
        in_specs=[pl.BlockSpec((tm, _D), rows), _layer_spec((1, _D), layer), _layer_spec((1, _D), layer),
                  _layer_spec((_D, _DFF), layer), _layer_spec((_DFF, _D), layer)],
        out_specs=pl.BlockSpec((tm, _D), rows),
        out_shape=jax.ShapeDtypeStruct((nt, _D), _F32),
        compiler_params=pltpu.CompilerParams(dimension_semantics=("parallel",),
                                             vmem_limit_bytes=_VMEM_LIMIT),
        name="mlp",
    )(x2, gpre, gpost, wup, wdown)
    return y.reshape(shape)


def kernel(x_prompt, x_sample, cache_k, cache_v, state_conv, g_mix_pre, g_mix_post, g_mlp_pre, g_mlp_post,
           w_in, attn_sinks, conv_w, w_attn_o, w_conv_o, w_out, w_up, w_down):
    depth = w_in.shape[0]
    n_p, n_dec = x_prompt.shape[0], x_sample.shape[0]
    ck = cache_k.reshape(depth, n_dec, _WIN, _KVD)
    cv = cache_v.reshape(depth, n_dec, _WIN, _KVD)
    win, wao, wco, wout, wup, wdown = (w.astype(_BF) for w in (w_in, w_attn_o, w_conv_o, w_out, w_up, w_down))
    gains = [g.reshape(depth, 1, _D) for g in (g_mix_pre, g_mix_post, g_mlp_pre, g_mlp_post)]
    mix_args = (attn_sinks, gains[0], gains[1], conv_w, win, wao, wco, wout)

    yp, ys = x_prompt, x_sample
    caches_p = caches_s = None
    for l in range(depth):
        yp, caches_p = _mixer_prompt(yp, l, caches_p, *mix_args)
        yp = _mlp(yp, l, gains[2], gains[3], wup, wdown)
        ys, caches_s = _mixer_sample(ys, l, caches_s, ck, cv, state_conv, *mix_args)
        ys = _mlp(ys, l, gains[2], gains[3], wup, wdown)

    kv5 = lambda a: a.reshape(a.shape[:3] + (2, _HD))
    return (yp, ys, kv5(caches_p[0]), kv5(caches_p[1]), caches_p[2],
            kv5(caches_s[0]), kv5(caches_s[1]), caches_s[2])
```

```python
import functools

import jax
import jax.numpy as jnp
from jax import lax
from jax.experimental import pallas as pl
from jax.experimental.pallas import tpu as pltpu

_D = 1024
_HD = 64
_NH = 8
_ATT = _NH * _HD
_KVD = 128
_CONV = 512
_DFF = 4096
_WIN = 128
_EPS = 1e-6
_LANES = 128
_SUB = 8
_BF_ROWS = 16

_OFF_Q = 0
_OFF_KV = _ATT
_OFF_B = _OFF_KV + 2 * _KVD
_OFF_C = _OFF_B + _CONV
_OFF_U = _OFF_C + _CONV
_OFF_GA = _OFF_U + _CONV
_OFF_GC = _OFF_GA + _D
_IN_DIM = _OFF_GC + _D

_PROMPT_TM = 512
_SAMPLE_SB = 32
_SAMPLE_UNROLL = 4
_MLP_TM = 512
_VMEM_LIMIT = 56 * 1024 * 1024

_BF = jnp.bfloat16
_F32 = jnp.float32


def _rms(x, g):
    return x * lax.rsqrt(jnp.mean(x * x, axis=-1, keepdims=True) + _EPS) * g


def _proj(h_bf, win_ref, off, width):
    return jnp.dot(h_bf, win_ref[:, off:off + width], preferred_element_type=_F32)


def _dup_halves(val, lo):
    rolled = pltpu.roll(val, _HD, axis=1)
    return jnp.where(lo, val, rolled), jnp.where(lo, rolled, val)


def _band_mask(qi, kj):
    return (kj > qi) & (kj <= qi + _WIN)


def _dot_nt(a, b):
    return lax.dot_general(a, b, (((1,), (1,)), ((), ())), preferred_element_type=_F32)


def _attend_group(q4, kd, vd, bias, sinks, nq):
    s = _dot_nt(q4, kd)
    es, rden = [], []
    for j in range(4):
        sj = s[j * nq:(j + 1) * nq] + bias
        m = jnp.maximum(jnp.max(sj, axis=-1, keepdims=True), sinks[j])
        e = jnp.exp(sj - m)
        den = jnp.sum(e, axis=-1, keepdims=True) + jnp.exp(sinks[j] - m)
        es.append(e)
        rden.append(1.0 / den)
    e_all = jnp.concatenate(es, axis=0).astype(_BF)
    o = jnp.dot(e_all, vd, preferred_element_type=_F32)
    return [o[j * nq:(j + 1) * nq] * rden[j] for j in range(4)]


def _mix_and_project(x, h, attn_bf, u_shift2, u_shift1, u, win_ref, convw_ref, wao_ref, wco_ref, wout_ref,
                     gpost_ref):
    attn_b = jnp.dot(attn_bf, wao_ref[...], preferred_element_type=_F32)
    mixed = jax.nn.sigmoid(_proj(h, win_ref, _OFF_GA, _D)) * attn_b
    cw = convw_ref[...]
    z = cw[0:1] * u_shift2
    z = z + cw[1:2] * u_shift1
    z = z + cw[2:3] * u
    bz = (_proj(h, win_ref, _OFF_B, _CONV) * z).astype(_BF)
    conv_b = jnp.dot(bz, wco_ref[...], preferred_element_type=_F32)
    mixed = mixed + jax.nn.sigmoid(_proj(h, win_ref, _OFF_GC, _D)) * conv_b
    mo = jnp.dot(mixed.astype(_BF), wout_ref[...], preferred_element_type=_F32)
    return x + _rms(mo, gpost_ref[...])


def _mixer_prompt_kernel(sink_ref, x_ref, gpre_ref, gpost_ref, convw_ref, win_ref, wao_ref, wco_ref, wout_ref,
                         *rest, tm, layer, n_alias):
    y_ref, nk_ref, nv_ref, nc_ref, kdup, vdup, q4s, ubuf, attn_scr = rest[n_alias:]
    s = pl.program_id(1)
    last = pl.num_programs(1) - 1

    @pl.when(s == 0)
    def _():
        kdup[:, 0:_WIN, :] = jnp.zeros((2, _WIN, _LANES), _BF)
        vdup[:, 0:_WIN, :] = jnp.zeros((2, _WIN, _LANES), _BF)
        ubuf[0:_SUB, :] = jnp.zeros((_SUB, _CONV), _F32)

    x = x_ref[...]
    h = _rms(x, gpre_ref[...]).astype(_BF)
    lo = lax.broadcasted_iota(jnp.int32, (tm, _LANES), 1) < _HD

    kv = _proj(h, win_ref, _OFF_KV, 2 * _KVD)
    k = kv[:, :_KVD]
    v = kv[:, _KVD:]
    for val, dst in ((k, kdup), (v, vdup)):
        d0, d1 = _dup_halves(val, lo)
        dst[0, _WIN:_WIN + tm, :] = d0.astype(_BF)
        dst[1, _WIN:_WIN + tm, :] = d1.astype(_BF)

    q = _proj(h, win_ref, _OFF_Q, _ATT) * (_HD ** -0.5)
    for c in range(4):
        qc = q[:, c * _LANES:(c + 1) * _LANES]
        q4s[c, 0] = jnp.where(lo, qc, 0.0).astype(_BF)
        q4s[c, 1] = jnp.where(lo, 0.0, qc).astype(_BF)

    qi = lax.broadcasted_iota(jnp.int32, (_WIN, 2 * _WIN), 0)
    kj = lax.broadcasted_iota(jnp.int32, (_WIN, 2 * _WIN), 1)
    band = _band_mask(qi, kj)
    neg = jnp.float32(-jnp.inf)
    bias = jnp.where(band, 0.0, neg)
    bias_first = jnp.where(band & ((kj >= _WIN) | (s > 0)), 0.0, neg)
    lo_q = lax.broadcasted_iota(jnp.int32, (_WIN, _LANES), 1) < _HD
    for qb in range(tm // _WIN):
        r0 = qb * _WIN
        b = bias_first if qb == 0 else bias
        for g in range(2):
            q4 = jnp.concatenate([q4s[2 * g, 0, r0:r0 + _WIN, :], q4s[2 * g, 1, r0:r0 + _WIN, :],
                                  q4s[2 * g + 1, 0, r0:r0 + _WIN, :], q4s[2 * g + 1, 1, r0:r0 + _WIN, :]], axis=0)
            sinks = [sink_ref[layer, 4 * g + j] for j in range(4)]
            o = _attend_group(q4, kdup[g, r0:r0 + 2 * _WIN, :], vdup[g, r0:r0 + 2 * _WIN, :], b, sinks, _WIN)
            attn_scr[r0:r0 + _WIN, (2 * g) * _LANES:(2 * g + 1) * _LANES] = jnp.where(lo_q, o[0], o[1]).astype(_BF)
            attn_scr[r0:r0 + _WIN, (2 * g + 1) * _LANES:(2 * g + 2) * _LANES] = (
                jnp.where(lo_q, o[2], o[3]).astype(_BF))

    kdup[:, 0:_WIN, :] = kdup[:, tm:tm + _WIN, :]
    vdup[:, 0:_WIN, :] = vdup[:, tm:tm + _WIN, :]

    u = _proj(h, win_ref, _OFF_C, _CONV) * _proj(h, win_ref, _OFF_U, _CONV)
    ubuf[_SUB:_SUB + tm, :] = u
    y_ref[...] = _mix_and_project(x, h, attn_scr[...], ubuf[_SUB - 2:_SUB - 2 + tm, :],
                                  ubuf[_SUB - 1:_SUB - 1 + tm, :], u,
                                  win_ref, convw_ref, wao_ref, wco_ref, wout_ref, gpost_ref)
    ubuf[0:_SUB, :] = ubuf[tm:tm + _SUB, :]

    @pl.when(s == last)
    def _():
        nk_ref[...] = k[tm - _WIN:tm, :]
        nv_ref[...] = v[tm - _WIN:tm, :]
        nc_ref[...] = u[tm - 2:tm, :]


def _mixer_sample_kernel(sink_ref, x_ref, gpre_ref, gpost_ref, convw_ref, ck_ref, cv_ref, st_ref,
                         win_ref, wao_ref, wco_ref, wout_ref, *rest, sb, t, layer, n_alias):
    y_ref, nk_ref, nv_ref, nc_ref, q_scr, knew, vnew, s_scr, e_scr, o_scr, ubuf = rest[n_alias:]
    tr = sb * t
    nkeys = _WIN + _BF_ROWS
    x = x_ref[...]
    h = _rms(x, gpre_ref[...]).astype(_BF)
    lo = lax.broadcasted_iota(jnp.int32, (tr, _LANES), 1) < _HD

    kv = _proj(h, win_ref, _OFF_KV, 2 * _KVD)
    k = kv[:, :_KVD]
    v = kv[:, _KVD:]
    knew[...] = k
    vnew[...] = v
    q = _proj(h, win_ref, _OFF_Q, _ATT) * (_HD ** -0.5)
    pieces = []
    for hd in range(_NH):
        qc = q[:, (hd // 2) * _LANES:(hd // 2 + 1) * _LANES]
        piece = jnp.where(lo, qc, 0.0) if hd % 2 == 0 else jnp.where(lo, 0.0, qc)
        pieces.append(piece.reshape(sb, t, _LANES))
    q_scr[...] = jnp.concatenate(pieces, axis=1).astype(_BF)

    nk_ref[:, 0:_WIN - t, :] = ck_ref[:, t:_WIN, :]
    nv_ref[:, 0:_WIN - t, :] = cv_ref[:, t:_WIN, :]
    nk_ref[:, _WIN - t:_WIN, :] = k.reshape(sb, t, _KVD)
    nv_ref[:, _WIN - t:_WIN, :] = v.reshape(sb, t, _KVD)

    lo_k = lax.broadcasted_iota(jnp.int32, (nkeys, _LANES), 1) < _HD
    pad = jnp.zeros((nkeys - _WIN - t, _LANES), _F32)
    half = _NH * t // 2

    def dup_keys(cache_ref, new_ref, b):
        r = pl.multiple_of(b * t, t)
        full = jnp.concatenate([cache_ref[b], new_ref[pl.ds(r, t), :], pad], axis=0)
        return [d.astype(_BF) for d in _dup_halves(full, lo_k)]

    def qk_body(b, carry):
        kds = dup_keys(ck_ref, knew, b)
        for g in range(2):
            s_scr[b, g * half:(g + 1) * half, :] = _dot_nt(q_scr[b, g * half:(g + 1) * half, :], kds[g])
        return carry

    lax.fori_loop(0, sb, qk_body, 0, unroll=_SAMPLE_UNROLL)

    rows = _NH * t
    qi = lax.broadcasted_iota(jnp.int32, (rows, nkeys), 0) & (t - 1)
    kj = lax.broadcasted_iota(jnp.int32, (rows, nkeys), 1)
    bias = jnp.where(_band_mask(qi, kj), 0.0, jnp.float32(-jnp.inf))
    sink_col = jnp.concatenate([jnp.full((t, 1), sink_ref[layer, hd], _F32) for hd in range(_NH)], axis=0)
    s = s_scr[...] + bias[None]
    m = jnp.maximum(jnp.max(s, axis=-1, keepdims=True), sink_col[None])
    e = jnp.exp(s - m)
    rden = 1.0 / (jnp.sum(e, axis=-1, keepdims=True) + jnp.exp(sink_col[None] - m))
    e_scr[...] = e.astype(_BF)

    def pv_body(b, carry):
        vds = dup_keys(cv_ref, vnew, b)
        for g in range(2):
            o_scr[b, g * half:(g + 1) * half, :] = jnp.dot(e_scr[b, g * half:(g + 1) * half, :], vds[g],
                                                           preferred_element_type=_F32)
        return carry

    lax.fori_loop(0, sb, pv_body, 0, unroll=_SAMPLE_UNROLL)

    o = o_scr[...] * rden
    lo3 = lax.broadcasted_iota(jnp.int32, (sb, t, _LANES), 2) < _HD
    attn = jnp.concatenate(
        [jnp.where(lo3, o[:, (2 * c) * t:(2 * c + 1) * t, :], o[:, (2 * c + 1) * t:(2 * c + 2) * t, :]).reshape(tr, _LANES)
         for c in range(_NH // 2)], axis=1)

    u = _proj(h, win_ref, _OFF_C, _CONV) * _proj(h, win_ref, _OFF_U, _CONV)
    u3 = u.reshape(sb, t, _CONV)
    ubuf[:, _SUB - 2:_SUB, :] = st_ref[...]
    ubuf[:, _SUB:_SUB + t, :] = u3
    us2 = ubuf[:, _SUB - 2:_SUB - 2 + t, :].reshape(tr, _CONV)
    us1 = ubuf[:, _SUB - 1:_SUB - 1 + t, :].reshape(tr, _CONV)
    nc_ref[...] = u3[:, t - 2:t, :]
    y_ref[...] = _mix_and_project(x, h, attn.astype(_BF), us2, us1, u,
                                  win_ref, convw_ref, wao_ref, wco_ref, wout_ref, gpost_ref)


def _mlp_kernel(x_ref, gpre_ref, gpost_ref, wup_ref, wdown_ref, y_ref):
    x = x_ref[...]
    hm = _rms(x, gpre_ref[...]).astype(_BF)
    acc = None
    for j in range(_DFF // _D):
        hj = jnp.dot(hm, wup_ref[:, j * _D:(j + 1) * _D], preferred_element_type=_F32)
        hj = jnp.square(jnp.maximum(hj, 0.0)).astype(_BF)
        part = jnp.dot(hj, wdown_ref[j * _D:(j + 1) * _D, :], preferred_element_type=_F32)
        acc = part if acc is None else acc + part
    y_ref[...] = x + _rms(acc, gpost_ref[...])


def _layer_spec(shape, layer):
    return pl.BlockSpec((None,) + shape, lambda *_: (layer,) + (0,) * len(shape))


_SMEM_SPEC = pl.BlockSpec(memory_space=pltpu.SMEM)
_ANY_SPEC = pl.BlockSpec(memory_space=pl.ANY)


def _mixer_weight_specs(layer):
    return [_layer_spec((_D, _IN_DIM), layer), _layer_spec((_ATT, _D), layer),
            _layer_spec((_CONV, _D), layer), _layer_spec((_D, _D), layer)]


def _alias_args(prev, n_inputs):
    if prev is None:
        return [], [], {}
    return list(prev), [_ANY_SPEC] * len(prev), {n_inputs + i: 1 + i for i in range(len(prev))}


def _mixer_prompt(x, layer, prev, sinks, gpre, gpost, convw, win, wao, wco, wout):
    n, seq, _ = x.shape
    depth = win.shape[0]
    tm = _PROMPT_TM
    assert seq % tm == 0 and tm % _WIN == 0 and seq >= _WIN
    tile = lambda b, s: (b, s, 0)
    per_b = lambda b, s: (layer, b, 0, 0)
    inputs = [sinks, x, gpre, gpost, convw, win, wao, wco, wout]
    alias_in, alias_specs, aliases = _alias_args(prev, len(inputs))
    out = pl.pallas_call(
        functools.partial(_mixer_prompt_kernel, tm=tm, layer=layer, n_alias=len(alias_in)),
        grid=(n, seq // tm),
        in_specs=[_SMEM_SPEC,
                  pl.BlockSpec((None, tm, _D), tile),
                  _layer_spec((1, _D), layer), _layer_spec((1, _D), layer), _layer_spec((3, _CONV), layer)]
                 + _mixer_weight_specs(layer) + alias_specs,
        out_specs=[pl.BlockSpec((None, tm, _D), tile),
                   pl.BlockSpec((None, None, _WIN, _KVD), per_b),
                   pl.BlockSpec((None, None, _WIN, _KVD), per_b),
                   pl.BlockSpec((None, None, 2, _CONV), per_b)],
        out_shape=[jax.ShapeDtypeStruct((n, seq, _D), _F32),
                   jax.ShapeDtypeStruct((depth, n, _WIN, _KVD), _F32),
                   jax.ShapeDtypeStruct((depth, n, _WIN, _KVD), _F32),
                   jax.ShapeDtypeStruct((depth, n, 2, _CONV), _F32)],
        scratch_shapes=[pltpu.VMEM((2, _WIN + tm, _LANES), _BF),
                        pltpu.VMEM((2, _WIN + tm, _LANES), _BF),
                        pltpu.VMEM((4, 2, tm, _LANES), _BF),
                        pltpu.VMEM((_SUB + tm, _CONV), _F32),
                        pltpu.VMEM((tm, _ATT), _BF)],
        input_output_aliases=aliases,
        compiler_params=pltpu.CompilerParams(dimension_semantics=("parallel", "arbitrary"),
                                             vmem_limit_bytes=_VMEM_LIMIT),
        name="mixer_prompt",
    )(*inputs, *alias_in)
    return out[0], tuple(out[1:])


def _mixer_sample(x, layer, prev, ck, cv, st, sinks, gpre, gpost, convw, win, wao, wco, wout):
    n, t, _ = x.shape
    depth = win.shape[0]
    sb = _SAMPLE_SB
    assert n % sb == 0 and t == _SUB and ck.shape == (depth, n, _WIN, _KVD)
    tr = sb * t
    nkeys = _WIN + _BF_ROWS
    rows = lambda i: (i, 0)
    seqs = lambda i: (layer, i, 0, 0)
    cache_spec = pl.BlockSpec((None, sb, _WIN, _KVD), seqs)
    state_spec = pl.BlockSpec((None, sb, 2, _CONV), seqs)
    inputs = [sinks, x.reshape(n * t, _D), gpre, gpost, convw, ck, cv, st, win, wao, wco, wout]
    alias_in, alias_specs, aliases = _alias_args(prev, len(inputs))
    out = pl.pallas_call(
        functools.partial(_mixer_sample_kernel, sb=sb, t=t, layer=layer, n_alias=len(alias_in)),
        grid=(n // sb,),
        in_specs=[_SMEM_SPEC,
                  pl.BlockSpec((tr, _D), rows),
                  _layer_spec((1, _D), layer), _layer_spec((1, _D), layer), _layer_spec((3, _CONV), layer),
                  cache_spec, cache_spec, state_spec]
                 + _mixer_weight_specs(layer) + alias_specs,
        out_specs=[pl.BlockSpec((tr, _D), rows), cache_spec, cache_spec, state_spec],
        out_shape=[jax.ShapeDtypeStruct((n * t, _D), _F32),
                   jax.ShapeDtypeStruct((depth, n, _WIN, _KVD), _F32),
                   jax.ShapeDtypeStruct((depth, n, _WIN, _KVD), _F32),
                   jax.ShapeDtypeStruct((depth, n, 2, _CONV), _F32)],
        scratch_shapes=[pltpu.VMEM((sb, _NH * t, _LANES), _BF),
                        pltpu.VMEM((tr, _KVD), _F32),
                        pltpu.VMEM((tr, _KVD), _F32),
                        pltpu.VMEM((sb, _NH * t, nkeys), _F32),
                        pltpu.VMEM((sb, _NH * t, nkeys), _BF),
                        pltpu.VMEM((sb, _NH * t, _LANES), _F32),
                        pltpu.VMEM((sb, 2 * _SUB, _CONV), _F32)],
        input_output_aliases=aliases,
        compiler_params=pltpu.CompilerParams(dimension_semantics=("arbitrary",),
                                             vmem_limit_bytes=_VMEM_LIMIT),
        name="mixer_sample",
    )(*inputs, *alias_in)
    return out[0].reshape(n, t, _D), tuple(out[1:])


def _mlp(x, layer, gpre, gpost, wup, wdown):
    shape = x.shape
    x2 = x.reshape(-1, _D)
    nt = x2.shape[0]
    tm = _MLP_TM
    assert nt % tm == 0
    rows = lambda i: (i, 0)
    y = pl.pallas_call(
        _mlp_kernel,
        grid=(nt // tm,),
        in_specs=[pl.BlockSpec((tm, _D), rows), _layer_spec((1, _D), layer), _layer_spec((1, _D), layer),
                  _layer_spec((_D, _DFF), layer), _layer_spec((_DFF, _D), layer)],
        out_specs=pl.BlockSpec((tm, _D), rows),
        out_shape=jax.ShapeDtypeStruct((nt, _D), _F32),
        compiler_params=pltpu.CompilerParams(dimension_semantics=("parallel",),
                                             vmem_limit_bytes=_VMEM_LIMIT),
        name="mlp",
    )(x2, gpre, gpost, wup, wdown)
    return y.reshape(shape)


def kernel(x_prompt, x_sample, cache_k, cache_v, state_conv, g_mix_pre, g_mix_post, g_mlp_pre, g_mlp_post,
           w_in, attn_sinks, conv_w, w_attn_o, w_conv_o, w_out, w_up, w_down):
    depth = w_in.shape[0]
    n_p, n_dec = x_prompt.shape[0], x_sample.shape[0]
    ck = cache_k.reshape(depth, n_dec, _WIN, _KVD)
    cv = cache_v.reshape(depth, n_dec, _WIN, _KVD)
    win, wao, wco, wout, wup, wdown = (w.astype(_BF) for w in (w_in, w_attn_o, w_conv_o, w_out, w_up, w_down))
    gains = [g.reshape(depth, 1, _D) for g in (g_mix_pre, g_mix_post, g_mlp_pre, g_mlp_post)]
    mix_args = (attn_sinks, gains[0], gains[1], conv_w, win, wao, wco, wout)

    yp, ys = x_prompt, x_sample
    caches_p = caches_s = None
    for l in range(depth):
        yp, caches_p = _mixer_prompt(yp, l, caches_p, *mix_args)
        yp = _mlp(yp, l, gains[2], gains[3], wup, wdown)
        ys, caches_s = _mixer_sample(ys, l, caches_s, ck, cv, state_conv, *mix_args)
        ys = _mlp(ys, l, gains[2], gains[3], wup, wdown)

    kv5 = lambda a: a.reshape(a.shape[:3] + (2, _HD))
    return (yp, ys, kv5(caches_p[0]), kv5(caches_p[1]), caches_p[2],
            kv5(caches_s[0]), kv5(caches_s[1]), caches_s[2])
```

```python
import functools

import jax
import jax.numpy as jnp
from jax import lax
from jax.experimental import pallas as pl
from jax.experimental.pallas import tpu as pltpu

_D = 1024
_HD = 64
_NH = 8
_ATT = _NH * _HD
_KVD = 128
_CONV = 512
_DFF = 4096
_WIN = 128
_EPS = 1e-6
_LANES = 128
_SUB = 8
_BF_ROWS = 16

_OFF_Q = 0
_OFF_KV = _ATT
_OFF_B = _OFF_KV + 2 * _KVD
_OFF_C = _OFF_B + _CONV
_OFF_U = _OFF_C + _CONV
_OFF_GA = _OFF_U + _CONV
_OFF_GC = _OFF_GA + _D
_IN_DIM = _OFF_GC + _D

_PROMPT_TM = 512
_SAMPLE_SB = 32
_MLP_TM = 512
_VMEM_LIMIT = 56 * 1024 * 1024

_BF = jnp.bfloat16
_F32 = jnp.float32


def _rms(x, g):
    return x * lax.rsqrt(jnp.mean(x * x, axis=-1, keepdims=True) + _EPS) * g


def _proj(h_bf, win_ref, off, width):
    return jnp.dot(h_bf, win_ref[:, off:off + width], preferred_element_type=_F32)


def _dup_halves(val, lo):
    rolled = pltpu.roll(val, _HD, axis=1)
    return jnp.where(lo, val, rolled), jnp.where(lo, rolled, val)


def _band_mask(qi, kj):
    return (kj > qi) & (kj <= qi + _WIN)


def _dot_nt(a, b):
    return lax.dot_general(a, b, (((1,), (1,)), ((), ())), preferred_element_type=_F32)


def _own_layer(ref, first):
    if not first:
        return ref
    if ref.shape[0] > 1:
        ref[1:] = jnp.zeros((ref.shape[0] - 1,) + ref.shape[1:], ref.dtype)
    return ref.at[0]


def _attend_group(q4, kd, vd, bias, sinks, nq):
    s = _dot_nt(q4, kd)
    es, rden = [], []
    for j in range(4):
        sj = s[j * nq:(j + 1) * nq] + bias
        m = jnp.maximum(jnp.max(sj, axis=-1, keepdims=True), sinks[j])
        e = jnp.exp(sj - m)
        den = jnp.sum(e, axis=-1, keepdims=True) + jnp.exp(sinks[j] - m)
        es.append(e)
        rden.append(1.0 / den)
    e_all = jnp.concatenate(es, axis=0).astype(_BF)
    o = jnp.dot(e_all, vd, preferred_element_type=_F32)
    return [o[j * nq:(j + 1) * nq] * rden[j] for j in range(4)]


def _mix_and_project(x, h, attn_bf, u_shift2, u_shift1, u, win_ref, convw_ref, wao_ref, wco_ref, wout_ref,
                     gpost_ref):
    attn_b = jnp.dot(attn_bf, wao_ref[...], preferred_element_type=_F32)
    mixed = jax.nn.sigmoid(_proj(h, win_ref, _OFF_GA, _D)) * attn_b
    cw = convw_ref[...]
    z = cw[0:1] * u_shift2
    z = z + cw[1:2] * u_shift1
    z = z + cw[2:3] * u
    bz = (_proj(h, win_ref, _OFF_B, _CONV) * z).astype(_BF)
    conv_b = jnp.dot(bz, wco_ref[...], preferred_element_type=_F32)
    mixed = mixed + jax.nn.sigmoid(_proj(h, win_ref, _OFF_GC, _D)) * conv_b
    mo = jnp.dot(mixed.astype(_BF), wout_ref[...], preferred_element_type=_F32)
    return x + _rms(mo, gpost_ref[...])


def _mixer_prompt_kernel(sink_ref, x_ref, gpre_ref, gpost_ref, convw_ref, win_ref, wao_ref, wco_ref, wout_ref,
                         *rest, tm, layer, first):
    y_ref, nk_ref, nv_ref, nc_ref, kdup, vdup, q4s, ubuf, attn_scr = rest[0 if first else 3:]
    s = pl.program_id(1)
    last = pl.num_programs(1) - 1

    @pl.when(s == 0)
    def _():
        kdup[:, 0:_WIN, :] = jnp.zeros((2, _WIN, _LANES), _BF)
        vdup[:, 0:_WIN, :] = jnp.zeros((2, _WIN, _LANES), _BF)
        ubuf[0:_SUB, :] = jnp.zeros((_SUB, _CONV), _F32)

    x = x_ref[...]
    h = _rms(x, gpre_ref[...]).astype(_BF)
    lo = lax.broadcasted_iota(jnp.int32, (tm, _LANES), 1) < _HD

    kv = _proj(h, win_ref, _OFF_KV, 2 * _KVD)
    k = kv[:, :_KVD]
    v = kv[:, _KVD:]
    for val, dst in ((k, kdup), (v, vdup)):
        d0, d1 = _dup_halves(val, lo)
        dst[0, _WIN:_WIN + tm, :] = d0.astype(_BF)
        dst[1, _WIN:_WIN + tm, :] = d1.astype(_BF)

    q = _proj(h, win_ref, _OFF_Q, _ATT) * (_HD ** -0.5)
    for c in range(4):
        qc = q[:, c * _LANES:(c + 1) * _LANES]
        q4s[c, 0] = jnp.where(lo, qc, 0.0).astype(_BF)
        q4s[c, 1] = jnp.where(lo, 0.0, qc).astype(_BF)

    qi = lax.broadcasted_iota(jnp.int32, (_WIN, 2 * _WIN), 0)
    kj = lax.broadcasted_iota(jnp.int32, (_WIN, 2 * _WIN), 1)
    band = _band_mask(qi, kj)
    neg = jnp.float32(-jnp.inf)
    bias = jnp.where(band, 0.0, neg)
    bias_first = jnp.where(band & ((kj >= _WIN) | (s > 0)), 0.0, neg)
    lo_q = lax.broadcasted_iota(jnp.int32, (_WIN, _LANES), 1) < _HD
    for qb in range(tm // _WIN):
        r0 = qb * _WIN
        b = bias_first if qb == 0 else bias
        for g in range(2):
            q4 = jnp.concatenate([q4s[2 * g, 0, r0:r0 + _WIN, :], q4s[2 * g, 1, r0:r0 + _WIN, :],
                                  q4s[2 * g + 1, 0, r0:r0 + _WIN, :], q4s[2 * g + 1, 1, r0:r0 + _WIN, :]], axis=0)
            sinks = [sink_ref[layer, 4 * g + j] for j in range(4)]
            o = _attend_group(q4, kdup[g, r0:r0 + 2 * _WIN, :], vdup[g, r0:r0 + 2 * _WIN, :], b, sinks, _WIN)
            attn_scr[r0:r0 + _WIN, (2 * g) * _LANES:(2 * g + 1) * _LANES] = jnp.where(lo_q, o[0], o[1]).astype(_BF)
            attn_scr[r0:r0 + _WIN, (2 * g + 1) * _LANES:(2 * g + 2) * _LANES] = (
                jnp.where(lo_q, o[2], o[3]).astype(_BF))

    kdup[:, 0:_WIN, :] = kdup[:, tm:tm + _WIN, :]
    vdup[:, 0:_WIN, :] = vdup[:, tm:tm + _WIN, :]

    u = _proj(h, win_ref, _OFF_C, _CONV) * _proj(h, win_ref, _OFF_U, _CONV)
    ubuf[_SUB:_SUB + tm, :] = u
    y_ref[...] = _mix_and_project(x, h, attn_scr[...], ubuf[_SUB - 2:_SUB - 2 + tm, :],
                                  ubuf[_SUB - 1:_SUB - 1 + tm, :], u,
                                  win_ref, convw_ref, wao_ref, wco_ref, wout_ref, gpost_ref)
    ubuf[0:_SUB, :] = ubuf[tm:tm + _SUB, :]

    @pl.when(s == last)
    def _():
        _own_layer(nk_ref, first)[...] = k[tm - _WIN:tm, :]
        _own_layer(nv_ref, first)[...] = v[tm - _WIN:tm, :]
        _own_layer(nc_ref, first)[...] = u[tm - 2:tm, :]


def _mixer_sample_kernel(sink_ref, x_ref, gpre_ref, gpost_ref, convw_ref, ck_ref, cv_ref, st_ref,
                         win_ref, wao_ref, wco_ref, wout_ref, *rest, sb, t, layer, first):
    y_ref, nk_ref, nv_ref, nc_ref, ubuf = rest[0 if first else 3:]
    nk_ref, nv_ref, nc_ref = (_own_layer(r, first) for r in (nk_ref, nv_ref, nc_ref))
    tr = sb * t
    nkeys = _WIN + _BF_ROWS
    rows = _NH * t
    half = rows // 2
    x = x_ref[...]
    h = _rms(x, gpre_ref[...]).astype(_BF)
    lo = lax.broadcasted_iota(jnp.int32, (tr, _LANES), 1) < _HD

    kv = _proj(h, win_ref, _OFF_KV, 2 * _KVD)
    k3 = kv[:, :_KVD].reshape(sb, t, _KVD)
    v3 = kv[:, _KVD:].reshape(sb, t, _KVD)
    q = _proj(h, win_ref, _OFF_Q, _ATT) * (_HD ** -0.5)
    pieces = []
    for hd in range(_NH):
        qc = q[:, (hd // 2) * _LANES:(hd // 2 + 1) * _LANES]
        piece = jnp.where(lo, qc, 0.0) if hd % 2 == 0 else jnp.where(lo, 0.0, qc)
        pieces.append(piece.reshape(sb, t, _LANES))
    q3 = jnp.concatenate(pieces, axis=1).astype(_BF)

    nk_ref[:, 0:_WIN - t, :] = ck_ref[:, t:_WIN, :]
    nv_ref[:, 0:_WIN - t, :] = cv_ref[:, t:_WIN, :]
    nk_ref[:, _WIN - t:_WIN, :] = k3
    nv_ref[:, _WIN - t:_WIN, :] = v3

    lo_k = lax.broadcasted_iota(jnp.int32, (sb * nkeys, _LANES), 1) < _HD
    pad = jnp.zeros((sb, nkeys - _WIN - t, _LANES), _F32)

    def dup_keys(cache_ref, new3):
        full = jnp.concatenate([cache_ref[...], new3, pad], axis=1).reshape(sb * nkeys, _LANES)
        return [d.reshape(sb, nkeys, _LANES).astype(_BF) for d in _dup_halves(full, lo_k)]

    kds = dup_keys(ck_ref, k3)
    s = jnp.concatenate([jnp.einsum('bqd,bkd->bqk', q3[:, g * half:(g + 1) * half, :], kds[g],
                                    preferred_element_type=_F32) for g in range(2)], axis=1)
    qi = lax.broadcasted_iota(jnp.int32, (rows, nkeys), 0) & (t - 1)
    kj = lax.broadcasted_iota(jnp.int32, (rows, nkeys), 1)
    bias = jnp.where(_band_mask(qi, kj), 0.0, jnp.float32(-jnp.inf))
    sink_col = jnp.concatenate([jnp.full((t, 1), sink_ref[layer, hd], _F32) for hd in range(_NH)], axis=0)
    s = s + bias[None]
    m = jnp.maximum(jnp.max(s, axis=-1, keepdims=True), sink_col[None])
    e = jnp.exp(s - m)
    rden = 1.0 / (jnp.sum(e, axis=-1, keepdims=True) + jnp.exp(sink_col[None] - m))
    e = e.astype(_BF)
    vds = dup_keys(cv_ref, v3)
    o = jnp.concatenate([jnp.einsum('bqk,bkd->bqd', e[:, g * half:(g + 1) * half, :], vds[g],
                                    preferred_element_type=_F32) for g in range(2)], axis=1) * rden
    lo3 = lax.broadcasted_iota(jnp.int32, (sb, t, _LANES), 2) < _HD
    attn = jnp.concatenate(
        [jnp.where(lo3, o[:, (2 * c) * t:(2 * c + 1) * t, :], o[:, (2 * c + 1) * t:(2 * c + 2) * t, :]).reshape(tr, _LANES)
         for c in range(_NH // 2)], axis=1)

    u = _proj(h, win_ref, _OFF_C, _CONV) * _proj(h, win_ref, _OFF_U, _CONV)
    u3 = u.reshape(sb, t, _CONV)
    ubuf[:, _SUB - 2:_SUB, :] = st_ref[...]
    ubuf[:, _SUB:_SUB + t, :] = u3
    us2 = ubuf[:, _SUB - 2:_SUB - 2 + t, :].reshape(tr, _CONV)
    us1 = ubuf[:, _SUB - 1:_SUB - 1 + t, :].reshape(tr, _CONV)
    nc_ref[...] = u3[:, t - 2:t, :]
    y_ref[...] = _mix_and_project(x, h, attn.astype(_BF), us2, us1, u,
                                  win_ref, convw_ref, wao_ref, wco_ref, wout_ref, gpost_ref)


def _mlp_kernel(x_ref, gpre_ref, gpost_ref, wup_ref, wdown_ref, y_ref):
    x = x_ref[...]
    hm = _rms(x, gpre_ref[...]).astype(_BF)
    acc = None
    for j in range(_DFF // _D):
        hj = jnp.dot(hm, wup_ref[:, j * _D:(j + 1) * _D], preferred_element_type=_F32)
        hj = jnp.square(jnp.maximum(hj, 0.0)).astype(_BF)
        part = jnp.dot(hj, wdown_ref[j * _D:(j + 1) * _D, :], preferred_element_type=_F32)
        acc = part if acc is None else acc + part
    y_ref[...] = x + _rms(acc, gpost_ref[...])


def _layer_spec(shape, layer, **kwargs):
    return pl.BlockSpec((None,) + shape, lambda *_: (layer,) + (0,) * len(shape), **kwargs)


def _weight_spec(shape, layer):
    return _layer_spec(shape, layer, pipeline_mode=pl.Buffered(1))


_SMEM_SPEC = pl.BlockSpec(memory_space=pltpu.SMEM)
_ANY_SPEC = pl.BlockSpec(memory_space=pl.ANY)


def _mixer_weight_specs(layer):
    return [_weight_spec((_D, _IN_DIM), layer), _weight_spec((_ATT, _D), layer),
            _weight_spec((_CONV, _D), layer), _weight_spec((_D, _D), layer)]


def _stacked_out_spec(depth, layer, first, block, index_map):
    lead, l0 = (depth, 0) if first else (None, layer)
    return pl.BlockSpec((lead,) + block, lambda *idx: (l0,) + index_map(*idx))


def _alias_args(prev, n_inputs):
    if prev is None:
        return [], [], {}
    return list(prev), [_ANY_SPEC] * len(prev), {n_inputs + i: 1 + i for i in range(len(prev))}


def _mixer_prompt(x, layer, prev, sinks, gpre, gpost, convw, win, wao, wco, wout):
    n, seq, _ = x.shape
    depth = win.shape[0]
    tm = _PROMPT_TM
    first = prev is None
    assert seq % tm == 0 and tm % _WIN == 0 and seq >= _WIN
    tile = lambda b, s: (b, s, 0)
    per_b = lambda b, s: (b, 0, 0)
    inputs = [sinks, x, gpre, gpost, convw, win, wao, wco, wout]
    alias_in, alias_specs, aliases = _alias_args(prev, len(inputs))
    out = pl.pallas_call(
        functools.partial(_mixer_prompt_kernel, tm=tm, layer=layer, first=first),
        grid=(n, seq // tm),
        in_specs=[_SMEM_SPEC,
                  pl.BlockSpec((None, tm, _D), tile),
                  _layer_spec((1, _D), layer), _layer_spec((1, _D), layer), _layer_spec((3, _CONV), layer)]
                 + _mixer_weight_specs(layer) + alias_specs,
        out_specs=[pl.BlockSpec((None, tm, _D), tile),
                   _stacked_out_spec(depth, layer, first, (None, _WIN, _KVD), per_b),
                   _stacked_out_spec(depth, layer, first, (None, _WIN, _KVD), per_b),
                   _stacked_out_spec(depth, layer, first, (None, 2, _CONV), per_b)],
        out_shape=[jax.ShapeDtypeStruct((n, seq, _D), _F32),
                   jax.ShapeDtypeStruct((depth, n, _WIN, _KVD), _F32),
                   jax.ShapeDtypeStruct((depth, n, _WIN, _KVD), _F32),
                   jax.ShapeDtypeStruct((depth, n, 2, _CONV), _F32)],
        scratch_shapes=[pltpu.VMEM((2, _WIN + tm, _LANES), _BF),
                        pltpu.VMEM((2, _WIN + tm, _LANES), _BF),
                        pltpu.VMEM((4, 2, tm, _LANES), _BF),
                        pltpu.VMEM((_SUB + tm, _CONV), _F32),
                        pltpu.VMEM((tm, _ATT), _BF)],
        input_output_aliases=aliases,
        compiler_params=pltpu.CompilerParams(dimension_semantics=("parallel", "arbitrary"),
                                             vmem_limit_bytes=_VMEM_LIMIT),
        name="mixer_prompt",
    )(*inputs, *alias_in)
    return out[0], tuple(out[1:])


def _mixer_sample(x, layer, prev, ck, cv, st, sinks, gpre, gpost, convw, win, wao, wco, wout):
    n, t, _ = x.shape
    depth = win.shape[0]
    sb = _SAMPLE_SB
    first = prev is None
    assert n % sb == 0 and t == _SUB and ck.shape == (depth, n, _WIN, _KVD)
    tr = sb * t
    rows = lambda i: (i, 0)
    seqs = lambda i: (i, 0, 0)
    cache_spec = pl.BlockSpec((None, sb, _WIN, _KVD), lambda i: (layer, i, 0, 0))
    state_spec = pl.BlockSpec((None, sb, 2, _CONV), lambda i: (layer, i, 0, 0))
    inputs = [sinks, x.reshape(n * t, _D), gpre, gpost, convw, ck, cv, st, win, wao, wco, wout]
    alias_in, alias_specs, aliases = _alias_args(prev, len(inputs))
    out = pl.pallas_call(
        functools.partial(_mixer_sample_kernel, sb=sb, t=t, layer=layer, first=first),
        grid=(n // sb,),
        in_specs=[_SMEM_SPEC,
                  pl.BlockSpec((tr, _D), rows),
                  _layer_spec((1, _D), layer), _layer_spec((1, _D), layer), _layer_spec((3, _CONV), layer),
                  cache_spec, cache_spec, state_spec]
                 + _mixer_weight_specs(layer) + alias_specs,
        out_specs=[pl.BlockSpec((tr, _D), rows),
                   _stacked_out_spec(depth, layer, first, (sb, _WIN, _KVD), seqs),
                   _stacked_out_spec(depth, layer, first, (sb, _WIN, _KVD), seqs),
                   _stacked_out_spec(depth, layer, first, (sb, 2, _CONV), seqs)],
        out_shape=[jax.ShapeDtypeStruct((n * t, _D), _F32),
                   jax.ShapeDtypeStruct((depth, n, _WIN, _KVD), _F32),
                   jax.ShapeDtypeStruct((depth, n, _WIN, _KVD), _F32),
                   jax.ShapeDtypeStruct((depth, n, 2, _CONV), _F32)],
        scratch_shapes=[pltpu.VMEM((sb, 2 * _SUB, _CONV), _F32)],
        input_output_aliases=aliases,
        compiler_params=pltpu.CompilerParams(dimension_semantics=("arbitrary",),
                                             vmem_limit_bytes=_VMEM_LIMIT),
        name="mixer_sample",
    )(*inputs, *alias_in)
    return out[0].reshape(n, t, _D), tuple(out[1:])


def _mlp(x, layer, gpre, gpost, wup, wdown):
    shape = x.shape
    x2 = x.reshape(-1, _D)
    nt = x2.shape[0]
    tm = _MLP_TM
    assert nt % tm == 0
    rows = lambda i: (i, 0)
    y = pl.pallas_call(
        _mlp_kernel,
        grid=(nt // tm,),
        in_specs=[pl.BlockSpec((tm, _D), rows), _layer_spec((1, _D), layer), _layer_spec((1, _D), layer),
                  _weight_spec((_D, _DFF), layer), _weight_spec((_DFF, _D), layer)],
        out_specs=pl.BlockSpec((tm, _D), rows),
        out_shape=jax.ShapeDtypeStruct((nt, _D), _F32),
        compiler_params=pltpu.CompilerParams(dimension_semantics=("parallel",),
                                             vmem_limit_bytes=_VMEM_LIMIT),
        name="mlp",
    )(x2, gpre, gpost, wup, wdown)
    return y.reshape(shape)


def kernel(x_prompt, x_sample, cache_k, cache_v, state_conv, g_mix_pre, g_mix_post, g_mlp_pre, g_mlp_post,
           w_in, attn_sinks, conv_w, w_attn_o, w_conv_o, w_out, w_up, w_down):
    depth = w_in.shape[0]
    n_p, n_dec = x_prompt.shape[0], x_sample.shape[0]
    ck = cache_k.reshape(depth, n_dec, _WIN, _KVD)
    cv = cache_v.reshape(depth, n_dec, _WIN, _KVD)
    win, wao, wco, wout, wup, wdown = (w.astype(_BF) for w in (w_in, w_attn_o, w_conv_o, w_out, w_up, w_down))
    gains = [g.reshape(depth, 1, _D) for g in (g_mix_pre, g_mix_post, g_mlp_pre, g_mlp_post)]
    mix_args = (attn_sinks, gains[0], gains[1], conv_w, win, wao, wco, wout)

    yp, ys = x_prompt, x_sample
    caches_p = caches_s = None
    for l in range(depth):
        yp, caches_p = _mixer_prompt(yp, l, caches_p, *mix_args)
        yp = _mlp(yp, l, gains[2], gains[3], wup, wdown)
        ys, caches_s = _mixer_sample(ys, l, caches_s, ck, cv, state_conv, *mix_args)
        ys = _mlp(ys, l, gains[2], gains[3], wup, wdown)

    kv5 = lambda a: a.reshape(a.shape[:3] + (2, _HD))
    return (yp, ys, kv5(caches_p[0]), kv5(caches_p[1]), caches_p[2],
            kv5(caches_s[0]), kv5(caches_s[1]), caches_s[2])
```

```python
import functools

import jax
import jax.numpy as jnp
from jax import lax
from jax.experimental import pallas as pl
from jax.experimental.pallas import tpu as pltpu

_D = 1024
_HD = 64
_NH = 8
_ATT = _NH * _HD
_KVD = 128
_CONV = 512
_DFF = 4096
_WIN = 128
_EPS = 1e-6
_LANES = 128
_SUB = 8
_BF_ROWS = 16

_OFF_Q = 0
_OFF_KV = _ATT
_OFF_B = _OFF_KV + 2 * _KVD
_OFF_C = _OFF_B + _CONV
_OFF_U = _OFF_C + _CONV
_OFF_GA = _OFF_U + _CONV
_OFF_GC = _OFF_GA + _D
_IN_DIM = _OFF_GC + _D

_PROMPT_TM = 512
_SAMPLE_SB = 32
_MLP_TM = 512
_VMEM_LIMIT = 60 * 1024 * 1024

_BF = jnp.bfloat16
_F32 = jnp.float32


def _rms(x, g):
    return x * lax.rsqrt(jnp.mean(x * x, axis=-1, keepdims=True) + _EPS) * g


def _proj(h_bf, win_ref, off, width):
    return jnp.dot(h_bf, win_ref[:, off:off + width], preferred_element_type=_F32)


def _dup_halves(val, lo):
    rolled = pltpu.roll(val, _HD, axis=1)
    return jnp.where(lo, val, rolled), jnp.where(lo, rolled, val)


def _band_mask(qi, kj):
    return (kj > qi) & (kj <= qi + _WIN)


def _dot_nt(a, b):
    return lax.dot_general(a, b, (((1,), (1,)), ((), ())), preferred_element_type=_F32)


def _own_layer(ref, first):
    if not first:
        return ref
    if ref.shape[0] > 1:
        ref[1:] = jnp.zeros((ref.shape[0] - 1,) + ref.shape[1:], ref.dtype)
    return ref.at[0]


def _softmax_pv(s, vd, bias, sinks, nq):
    es, rden = [], []
    for j in range(4):
        sj = s[j * nq:(j + 1) * nq] + bias
        m = jnp.maximum(jnp.max(sj, axis=-1, keepdims=True), sinks[j])
        e = jnp.exp(sj - m)
        den = jnp.sum(e, axis=-1, keepdims=True) + jnp.exp(sinks[j] - m)
        es.append(e)
        rden.append(1.0 / den)
    e_all = jnp.concatenate(es, axis=0).astype(_BF)
    o = jnp.dot(e_all, vd, preferred_element_type=_F32)
    return [o[j * nq:(j + 1) * nq] * rden[j] for j in range(4)]


def _mix_and_project(x, h, attn_bf, u_shift2, u_shift1, u, win_ref, convw_ref, wao_ref, wco_ref, wout_ref,
                     gpost_ref):
    mo = _mixed_out(h, attn_bf, u_shift2, u_shift1, u, win_ref, convw_ref, wao_ref, wco_ref, wout_ref)
    return x + _rms(mo, gpost_ref[...])


def _mixed_out(h, attn_bf, u_shift2, u_shift1, u, win_ref, convw_ref, wao_ref, wco_ref, wout_ref):
    attn_b = jnp.dot(attn_bf, wao_ref[...], preferred_element_type=_F32)
    mixed = jax.nn.sigmoid(_proj(h, win_ref, _OFF_GA, _D)) * attn_b
    cw = convw_ref[...]
    z = cw[0:1] * u_shift2
    z = z + cw[1:2] * u_shift1
    z = z + cw[2:3] * u
    bz = (_proj(h, win_ref, _OFF_B, _CONV) * z).astype(_BF)
    conv_b = jnp.dot(bz, wco_ref[...], preferred_element_type=_F32)
    mixed = mixed + jax.nn.sigmoid(_proj(h, win_ref, _OFF_GC, _D)) * conv_b
    return jnp.dot(mixed.astype(_BF), wout_ref[...], preferred_element_type=_F32)


def _run_interleaved(*stages):
    stages = list(stages)
    while stages:
        for g in list(stages):
            try:
                next(g)
            except StopIteration:
                stages.remove(g)


def _mlp_stages(x_ref, y_ref, gpre_ref, gpost_ref, wup_ref, wdown_ref):
    x = x_ref[...]
    hm = _rms(x, gpre_ref[...]).astype(_BF)
    yield
    acc = None
    for j in range(_DFF // _D):
        hj = jnp.dot(hm, wup_ref[:, j * _D:(j + 1) * _D], preferred_element_type=_F32)
        yield
        hj = jnp.square(jnp.maximum(hj, 0.0)).astype(_BF)
        part = jnp.dot(hj, wdown_ref[j * _D:(j + 1) * _D, :], preferred_element_type=_F32)
        acc = part if acc is None else acc + part
        yield
    y_ref[...] = x + _rms(acc, gpost_ref[...])


def _prompt_layer_kernel(sink_ref, x_ref, gpre_ref, gpost_ref, gmpre_ref, gmpost_ref, convw_ref,
                         win_ref, wao_ref, wco_ref, wout_ref, wup_ref, wdown_ref,
                         *rest, tm, tps, layer, first):
    (y_ref, nk_ref, nv_ref, nc_ref,
     kdup, vdup, q4s, ubuf, attn_scr, x1_scr, x1_prev, klast, vlast, ulast) = rest[0 if first else 3:]
    j = pl.program_id(0)
    n_tiles = pl.num_programs(0) - 1
    s = j % tps

    def start_sequence():
        @pl.when(s == 0)
        def _():
            kdup[:, 0:_WIN, :] = jnp.zeros((2, _WIN, _LANES), _BF)
            vdup[:, 0:_WIN, :] = jnp.zeros((2, _WIN, _LANES), _BF)
            ubuf[0:_SUB, :] = jnp.zeros((_SUB, _CONV), _F32)

    def end_sequence():
        @pl.when(s == tps - 1)
        def _():
            _own_layer(nk_ref, first)[...] = klast[...]
            _own_layer(nv_ref, first)[...] = vlast[...]
            _own_layer(nc_ref, first)[...] = ulast[_SUB - 2:_SUB, :]

    def mlp(src):
        return _mlp_stages(src, y_ref, gmpre_ref, gmpost_ref, wup_ref, wdown_ref)

    def mixer():
        return _mixer_prompt_stages(sink_ref, x_ref, gpre_ref, gpost_ref, convw_ref, win_ref, wao_ref, wco_ref,
                                    wout_ref, kdup, vdup, q4s, ubuf, attn_scr, x1_scr, klast, vlast, ulast, s,
                                    tm=tm, layer=layer)

    @pl.when(j == 0)
    def _():
        start_sequence()
        _run_interleaved(mixer())
        end_sequence()

    @pl.when((j > 0) & (j < n_tiles))
    def _():
        start_sequence()
        x1_prev[...] = x1_scr[...]
        mlp_stages, mixer_stages = mlp(x1_prev), mixer()
        for _ in range(2 * (_DFF // _D) - 1):
            next(mlp_stages)
        next(mixer_stages)
        next(mlp_stages)
        next(mixer_stages)
        _run_interleaved(mlp_stages, mixer_stages)
        end_sequence()

    @pl.when(j == n_tiles)
    def _():
        _run_interleaved(mlp(x1_scr))


def _mixer_prompt_stages(sink_ref, x_ref, gpre_ref, gpost_ref, convw_ref, win_ref, wao_ref, wco_ref, wout_ref,
                         kdup, vdup, q4s, ubuf, attn_scr, x1_scr, klast, vlast, ulast, s, *, tm, layer):
    x = x_ref[...]
    h = _rms(x, gpre_ref[...]).astype(_BF)
    lo = lax.broadcasted_iota(jnp.int32, (tm, _LANES), 1) < _HD

    hr = tm // 2
    kv = jnp.concatenate([_proj(h[:hr], win_ref, _OFF_KV, 2 * _KVD), _proj(h[hr:], win_ref, _OFF_KV, 2 * _KVD)],
                         axis=0)
    k = kv[:, :_KVD]
    v = kv[:, _KVD:]
    for val, dst in ((k, kdup), (v, vdup)):
        d0, d1 = _dup_halves(val, lo)
        dst[0, _WIN:_WIN + tm, :] = d0.astype(_BF)
        dst[1, _WIN:_WIN + tm, :] = d1.astype(_BF)

    q = _proj(h, win_ref, _OFF_Q, _ATT) * (_HD ** -0.5)
    for c in range(4):
        qc = q[:, c * _LANES:(c + 1) * _LANES]
        q4s[c, 0] = jnp.where(lo, qc, 0.0).astype(_BF)
        q4s[c, 1] = jnp.where(lo, 0.0, qc).astype(_BF)

    qi = lax.broadcasted_iota(jnp.int32, (_WIN, 2 * _WIN), 0)
    kj = lax.broadcasted_iota(jnp.int32, (_WIN, 2 * _WIN), 1)
    band = _band_mask(qi, kj)
    neg = jnp.float32(-jnp.inf)
    bias = jnp.where(band, 0.0, neg)
    bias_first = jnp.where(band & ((kj >= _WIN) | (s > 0)), 0.0, neg)
    lo_q = lax.broadcasted_iota(jnp.int32, (_WIN, _LANES), 1) < _HD

    def scores_of(qb):
        r0 = qb * _WIN
        out = []
        for g in range(2):
            q4 = jnp.concatenate([q4s[2 * g, 0, r0:r0 + _WIN, :], q4s[2 * g, 1, r0:r0 + _WIN, :],
                                  q4s[2 * g + 1, 0, r0:r0 + _WIN, :], q4s[2 * g + 1, 1, r0:r0 + _WIN, :]], axis=0)
            out.append(_dot_nt(q4, kdup[g, r0:r0 + 2 * _WIN, :]))
        return out

    def finish_block(qb, scores):
        r0 = qb * _WIN
        b = bias_first if qb == 0 else bias
        for g in range(2):
            sinks = [sink_ref[layer, 4 * g + j] for j in range(4)]
            o = _softmax_pv(scores[g], vdup[g, r0:r0 + 2 * _WIN, :], b, sinks, _WIN)
            attn_scr[r0:r0 + _WIN, (2 * g) * _LANES:(2 * g + 1) * _LANES] = jnp.where(lo_q, o[0], o[1]).astype(_BF)
            attn_scr[r0:r0 + _WIN, (2 * g + 1) * _LANES:(2 * g + 2) * _LANES] = (
                jnp.where(lo_q, o[2], o[3]).astype(_BF))

    def conv_input():
        u = _proj(h, win_ref, _OFF_C, _CONV) * _proj(h, win_ref, _OFF_U, _CONV)
        ubuf[_SUB:_SUB + tm, :] = u
        return u

    fillers = [conv_input,
               lambda: jax.nn.sigmoid(_proj(h, win_ref, _OFF_GA, _D)),
               lambda: jax.nn.sigmoid(_proj(h, win_ref, _OFF_GC, _D)),
               lambda: _proj(h, win_ref, _OFF_B, _CONV)]
    filled = []
    n_qb = tm // _WIN
    scores = scores_of(0)
    for qb in range(n_qb):
        if qb < len(fillers):
            filled.append(fillers[qb]())
        finish_block(qb, scores)
        if qb + 1 < n_qb:
            scores = scores_of(qb + 1)
    filled += [f() for f in fillers[n_qb:]]
    u, gate_attn, gate_conv, b_gate = filled

    kdup[:, 0:_WIN, :] = kdup[:, tm:tm + _WIN, :]
    vdup[:, 0:_WIN, :] = vdup[:, tm:tm + _WIN, :]

    mixed = gate_attn * jnp.dot(attn_scr[...], wao_ref[...], preferred_element_type=_F32)
    cw = convw_ref[...]
    z = cw[0:1] * ubuf[_SUB - 2:_SUB - 2 + tm, :]
    z = z + cw[1:2] * ubuf[_SUB - 1:_SUB - 1 + tm, :]
    z = z + cw[2:3] * u
    conv_b = jnp.dot((b_gate * z).astype(_BF), wco_ref[...], preferred_element_type=_F32)
    mixed = (mixed + gate_conv * conv_b).astype(_BF)
    yield
    mo = jnp.dot(mixed, wout_ref[...], preferred_element_type=_F32)
    yield
    x1_scr[...] = x + _rms(mo, gpost_ref[...])
    ubuf[0:_SUB, :] = ubuf[tm:tm + _SUB, :]
    klast[...] = k[tm - _WIN:tm, :]
    vlast[...] = v[tm - _WIN:tm, :]
    ulast[...] = u[tm - _SUB:tm, :]


def _mixer_sample_kernel(sink_ref, x_ref, gpre_ref, gpost_ref, convw_ref, ck_ref, cv_ref, st_ref,
                         win_ref, wao_ref, wco_ref, wout_ref, *rest, sb, t, layer, first):
    y_ref, nk_ref, nv_ref, nc_ref, ubuf = rest[0 if first else 3:]
    nk_ref, nv_ref, nc_ref = (_own_layer(r, first) for r in (nk_ref, nv_ref, nc_ref))
    tr = sb * t
    nkeys = _WIN + _BF_ROWS
    rows = _NH * t
    half = rows // 2
    x = x_ref[...]
    h = _rms(x, gpre_ref[...]).astype(_BF)
    lo = lax.broadcasted_iota(jnp.int32, (tr, _LANES), 1) < _HD

    kv = _proj(h, win_ref, _OFF_KV, 2 * _KVD)
    k3 = kv[:, :_KVD].reshape(sb, t, _KVD)
    v3 = kv[:, _KVD:].reshape(sb, t, _KVD)
    q = _proj(h, win_ref, _OFF_Q, _ATT) * (_HD ** -0.5)
    pieces = []
    for hd in range(_NH):
        qc = q[:, (hd // 2) * _LANES:(hd // 2 + 1) * _LANES]
        piece = jnp.where(lo, qc, 0.0) if hd % 2 == 0 else jnp.where(lo, 0.0, qc)
        pieces.append(piece.reshape(sb, t, _LANES))
    q3 = jnp.concatenate(pieces, axis=1).astype(_BF)

    nk_ref[:, 0:_WIN - t, :] = ck_ref[:, t:_WIN, :]
    nv_ref[:, 0:_WIN - t, :] = cv_ref[:, t:_WIN, :]
    nk_ref[:, _WIN - t:_WIN, :] = k3
    nv_ref[:, _WIN - t:_WIN, :] = v3

    lo_k = lax.broadcasted_iota(jnp.int32, (sb * nkeys, _LANES), 1) < _HD
    pad = jnp.zeros((sb, nkeys - _WIN - t, _LANES), _F32)

    def dup_keys(cache_ref, new3):
        full = jnp.concatenate([cache_ref[...], new3, pad], axis=1).reshape(sb * nkeys, _LANES)
        return [d.reshape(sb, nkeys, _LANES).astype(_BF) for d in _dup_halves(full, lo_k)]

    kds = dup_keys(ck_ref, k3)
    s = jnp.concatenate([jnp.einsum('bqd,bkd->bqk', q3[:, g * half:(g + 1) * half, :], kds[g],
                                    preferred_element_type=_F32) for g in range(2)], axis=1)
    qi = lax.broadcasted_iota(jnp.int32, (rows, nkeys), 0) & (t - 1)
    kj = lax.broadcasted_iota(jnp.int32, (rows, nkeys), 1)
    bias = jnp.where(_band_mask(qi, kj), 0.0, jnp.float32(-jnp.inf))
    sink_col = jnp.concatenate([jnp.full((t, 1), sink_ref[layer, hd], _F32) for hd in range(_NH)], axis=0)
    s = s + bias[None]
    m = jnp.maximum(jnp.max(s, axis=-1, keepdims=True), sink_col[None])
    e = jnp.exp(s - m)
    rden = 1.0 / (jnp.sum(e, axis=-1, keepdims=True) + jnp.exp(sink_col[None] - m))
    e = e.astype(_BF)
    vds = dup_keys(cv_ref, v3)
    o = jnp.concatenate([jnp.einsum('bqk,bkd->bqd', e[:, g * half:(g + 1) * half, :], vds[g],
                                    preferred_element_type=_F32) for g in range(2)], axis=1) * rden
    lo3 = lax.broadcasted_iota(jnp.int32, (sb, t, _LANES), 2) < _HD
    attn = jnp.concatenate(
        [jnp.where(lo3, o[:, (2 * c) * t:(2 * c + 1) * t, :], o[:, (2 * c + 1) * t:(2 * c + 2) * t, :]).reshape(tr, _LANES)
         for c in range(_NH // 2)], axis=1)

    u = _proj(h, win_ref, _OFF_C, _CONV) * _proj(h, win_ref, _OFF_U, _CONV)
    u3 = u.reshape(sb, t, _CONV)
    ubuf[:, _SUB - 2:_SUB, :] = st_ref[...]
    ubuf[:, _SUB:_SUB + t, :] = u3
    us2 = ubuf[:, _SUB - 2:_SUB - 2 + t, :].reshape(tr, _CONV)
    us1 = ubuf[:, _SUB - 1:_SUB - 1 + t, :].reshape(tr, _CONV)
    nc_ref[...] = u3[:, t - 2:t, :]
    y_ref[...] = _mix_and_project(x, h, attn.astype(_BF), us2, us1, u,
                                  win_ref, convw_ref, wao_ref, wco_ref, wout_ref, gpost_ref)


def _mlp_kernel(x_ref, gpre_ref, gpost_ref, wup_ref, wdown_ref, y_ref):
    _run_interleaved(_mlp_stages(x_ref, y_ref, gpre_ref, gpost_ref, wup_ref, wdown_ref))


def _layer_spec(shape, layer, **kwargs):
    return pl.BlockSpec((None,) + shape, lambda *_: (layer,) + (0,) * len(shape), **kwargs)


def _weight_spec(shape, layer):
    return _layer_spec(shape, layer, pipeline_mode=pl.Buffered(1))


_SMEM_SPEC = pl.BlockSpec(memory_space=pltpu.SMEM)
_ANY_SPEC = pl.BlockSpec(memory_space=pl.ANY)


def _mixer_weight_specs(layer):
    return [_weight_spec((_D, _IN_DIM), layer), _weight_spec((_ATT, _D), layer),
            _weight_spec((_CONV, _D), layer), _weight_spec((_D, _D), layer)]


def _stacked_out_spec(depth, layer, first, block, index_map):
    lead, l0 = (depth, 0) if first else (None, layer)
    return pl.BlockSpec((lead,) + block, lambda *idx: (l0,) + index_map(*idx))


def _alias_args(prev, n_inputs):
    if prev is None:
        return [], [], {}
    return list(prev), [_ANY_SPEC] * len(prev), {n_inputs + i: 1 + i for i in range(len(prev))}


def _prompt_layer(x, layer, prev, sinks, gpre, gpost, gmpre, gmpost, convw, win, wao, wco, wout, wup, wdown):
    n, seq, _ = x.shape
    depth = win.shape[0]
    tm = _PROMPT_TM
    first = prev is None
    assert seq % tm == 0 and tm % _WIN == 0 and seq >= _WIN
    tps = seq // tm
    n_tiles = n * tps

    def tile_of(t):
        return (t // tps, t % tps, 0)

    mixer_tile = lambda j: tile_of(jnp.minimum(j, n_tiles - 1))
    mlp_tile = lambda j: tile_of(jnp.maximum(j - 1, 0))
    per_b = lambda j: (jnp.minimum(j, n_tiles - 1) // tps, 0, 0)
    inputs = [sinks, x, gpre, gpost, gmpre, gmpost, convw, win, wao, wco, wout, wup, wdown]
    alias_in, alias_specs, aliases = _alias_args(prev, len(inputs))
    out = pl.pallas_call(
        functools.partial(_prompt_layer_kernel, tm=tm, tps=tps, layer=layer, first=first),
        grid=(n_tiles + 1,),
        in_specs=[_SMEM_SPEC,
                  pl.BlockSpec((None, tm, _D), mixer_tile)]
                 + [_layer_spec((1, _D), layer)] * 4 + [_layer_spec((3, _CONV), layer)]
                 + _mixer_weight_specs(layer)
                 + [_weight_spec((_D, _DFF), layer), _weight_spec((_DFF, _D), layer)] + alias_specs,
        out_specs=[pl.BlockSpec((None, tm, _D), mlp_tile),
                   _stacked_out_spec(depth, layer, first, (None, _WIN, _KVD), per_b),
                   _stacked_out_spec(depth, layer, first, (None, _WIN, _KVD), per_b),
                   _stacked_out_spec(depth, layer, first, (None, 2, _CONV), per_b)],
        out_shape=[jax.ShapeDtypeStruct((n, seq, _D), _F32),
                   jax.ShapeDtypeStruct((depth, n, _WIN, _KVD), _F32),
                   jax.ShapeDtypeStruct((depth, n, _WIN, _KVD), _F32),
                   jax.ShapeDtypeStruct((depth, n, 2, _CONV), _F32)],
        scratch_shapes=[pltpu.VMEM((2, _WIN + tm, _LANES), _BF),
                        pltpu.VMEM((2, _WIN + tm, _LANES), _BF),
                        pltpu.VMEM((4, 2, tm, _LANES), _BF),
                        pltpu.VMEM((_SUB + tm, _CONV), _F32),
                        pltpu.VMEM((tm, _ATT), _BF),
                        pltpu.VMEM((tm, _D), _F32),
                        pltpu.VMEM((tm, _D), _F32),
                        pltpu.VMEM((_WIN, _KVD), _F32),
                        pltpu.VMEM((_WIN, _KVD), _F32),
                        pltpu.VMEM((_SUB, _CONV), _F32)],
        input_output_aliases=aliases,
        compiler_params=pltpu.CompilerParams(dimension_semantics=("arbitrary",),
                                             vmem_limit_bytes=_VMEM_LIMIT),
        name="prompt_layer",
    )(*inputs, *alias_in)
    return out[0], tuple(out[1:])


def _mixer_sample(x, layer, prev, ck, cv, st, sinks, gpre, gpost, convw, win, wao, wco, wout):
    n, t, _ = x.shape
    depth = win.shape[0]
    sb = _SAMPLE_SB
    first = prev is None
    assert n % sb == 0 and t == _SUB and ck.shape == (depth, n, _WIN, _KVD)
    tr = sb * t
    rows = lambda i: (i, 0)
    seqs = lambda i: (i, 0, 0)
    cache_spec = pl.BlockSpec((None, sb, _WIN, _KVD), lambda i: (layer, i, 0, 0))
    state_spec = pl.BlockSpec((None, sb, 2, _CONV), lambda i: (layer, i, 0, 0))
    inputs = [sinks, x.reshape(n * t, _D), gpre, gpost, convw, ck, cv, st, win, wao, wco, wout]
    alias_in, alias_specs, aliases = _alias_args(prev, len(inputs))
    out = pl.pallas_call(
        functools.partial(_mixer_sample_kernel, sb=sb, t=t, layer=layer, first=first),
        grid=(n // sb,),
        in_specs=[_SMEM_SPEC,
                  pl.BlockSpec((tr, _D), rows),
                  _layer_spec((1, _D), layer), _layer_spec((1, _D), layer), _layer_spec((3, _CONV), layer),
                  cache_spec, cache_spec, state_spec]
                 + _mixer_weight_specs(layer) + alias_specs,
        out_specs=[pl.BlockSpec((tr, _D), rows),
                   _stacked_out_spec(depth, layer, first, (sb, _WIN, _KVD), seqs),
                   _stacked_out_spec(depth, layer, first, (sb, _WIN, _KVD), seqs),
                   _stacked_out_spec(depth, layer, first, (sb, 2, _CONV), seqs)],
        out_shape=[jax.ShapeDtypeStruct((n * t, _D), _F32),
                   jax.ShapeDtypeStruct((depth, n, _WIN, _KVD), _F32),
                   jax.ShapeDtypeStruct((depth, n, _WIN, _KVD), _F32),
                   jax.ShapeDtypeStruct((depth, n, 2, _CONV), _F32)],
        scratch_shapes=[pltpu.VMEM((sb, 2 * _SUB, _CONV), _F32)],
        input_output_aliases=aliases,
        compiler_params=pltpu.CompilerParams(dimension_semantics=("arbitrary",),
                                             vmem_limit_bytes=_VMEM_LIMIT),
        name="mixer_sample",
    )(*inputs, *alias_in)
    return out[0].reshape(n, t, _D), tuple(out[1:])


def _mlp(x, layer, gpre, gpost, wup, wdown):
    shape = x.shape
    x2 = x.reshape(-1, _D)
    nt = x2.shape[0]
    tm = _MLP_TM
    assert nt % tm == 0
    rows = lambda i: (i, 0)
    y = pl.pallas_call(
        _mlp_kernel,
        grid=(nt // tm,),
        in_specs=[pl.BlockSpec((tm, _D), rows), _layer_spec((1, _D), layer), _layer_spec((1, _D), layer),
                  _weight_spec((_D, _DFF), layer), _weight_spec((_DFF, _D), layer)],
        out_specs=pl.BlockSpec((tm, _D), rows),
        out_shape=jax.ShapeDtypeStruct((nt, _D), _F32),
        compiler_params=pltpu.CompilerParams(dimension_semantics=("parallel",),
                                             vmem_limit_bytes=_VMEM_LIMIT),
        name="mlp",
    )(x2, gpre, gpost, wup, wdown)
    return y.reshape(shape)


def kernel(x_prompt, x_sample, cache_k, cache_v, state_conv, g_mix_pre, g_mix_post, g_mlp_pre, g_mlp_post,
           w_in, attn_sinks, conv_w, w_attn_o, w_conv_o, w_out, w_up, w_down):
    depth = w_in.shape[0]
    n_p, n_dec = x_prompt.shape[0], x_sample.shape[0]
    ck = cache_k.reshape(depth, n_dec, _WIN, _KVD)
    cv = cache_v.reshape(depth, n_dec, _WIN, _KVD)
    win, wao, wco, wout, wup, wdown = (w.astype(_BF) for w in (w_in, w_attn_o, w_conv_o, w_out, w_up, w_down))
    gains = [g.reshape(depth, 1, _D) for g in (g_mix_pre, g_mix_post, g_mlp_pre, g_mlp_post)]
    mix_args = (attn_sinks, gains[0], gains[1], conv_w, win, wao, wco, wout)

    yp, ys = x_prompt, x_sample
    caches_p = caches_s = None
    for l in range(depth):
        yp, caches_p = _prompt_layer(yp, l, caches_p, attn_sinks, *gains, conv_w, win, wao, wco, wout, wup, wdown)
        ys, caches_s = _mixer_sample(ys, l, caches_s, ck, cv, state_conv, *mix_args)
        ys = _mlp(ys, l, gains[2], gains[3], wup, wdown)

    kv5 = lambda a: a.reshape(a.shape[:3] + (2, _HD))
    return (yp, ys, kv5(caches_p[0]), kv5(caches_p[1]), caches_p[2],
            kv5(caches_s[0]), kv5(caches_s[1]), caches_s[2])
```

```python
import functools

import jax
import jax.numpy as jnp
from jax import lax
from jax.experimental import pallas as pl
from jax.experimental.pallas import tpu as pltpu

_D = 1024
_HD = 64
_NH = 8
_ATT = _NH * _HD
_KVD = 128
_CONV = 512
_DFF = 4096
_WIN = 128
_EPS = 1e-6
_LANES = 128
_SUB = 8
_BF_ROWS = 16

_OFF_Q = 0
_OFF_KV = _ATT
_OFF_B = _OFF_KV + 2 * _KVD
_OFF_C = _OFF_B + _CONV
_OFF_U = _OFF_C + _CONV
_OFF_GA = _OFF_U + _CONV
_OFF_GC = _OFF_GA + _D
_IN_DIM = _OFF_GC + _D

_PROMPT_TM = 512
_SAMPLE_SB = 32
_MLP_TM = 512
_VMEM_LIMIT = 60 * 1024 * 1024

_BF = jnp.bfloat16
_F32 = jnp.float32


def _rms(x, g):
    return x * lax.rsqrt(jnp.mean(x * x, axis=-1, keepdims=True) + _EPS) * g


def _proj(h_bf, win_ref, off, width):
    return jnp.dot(h_bf, win_ref[:, off:off + width], preferred_element_type=_F32)


def _dup_halves(val, lo):
    rolled = pltpu.roll(val, _HD, axis=1)
    return jnp.where(lo, val, rolled), jnp.where(lo, rolled, val)


def _band_mask(qi, kj):
    return (kj > qi) & (kj <= qi + _WIN)


def _dot_nt(a, b):
    return lax.dot_general(a, b, (((1,), (1,)), ((), ())), preferred_element_type=_F32)


def _own_layer(ref, first):
    if not first:
        return ref
    if ref.shape[0] > 1:
        ref[1:] = jnp.zeros((ref.shape[0] - 1,) + ref.shape[1:], ref.dtype)
    return ref.at[0]


def _softmax_pv(s, vd, bias, sinks, nq):
    es, rden = [], []
    for j in range(4):
        sj = s[j * nq:(j + 1) * nq] + bias
        m = jnp.maximum(jnp.max(sj, axis=-1, keepdims=True), sinks[j])
        e = jnp.exp(sj - m)
        den = jnp.sum(e, axis=-1, keepdims=True) + jnp.exp(sinks[j] - m)
        es.append(e)
        rden.append(1.0 / den)
    e_all = jnp.concatenate(es, axis=0).astype(_BF)
    o = jnp.dot(e_all, vd, preferred_element_type=_F32)
    return [o[j * nq:(j + 1) * nq] * rden[j] for j in range(4)]


def _mix_and_project(x, h, attn_bf, u_shift2, u_shift1, u, win_ref, convw_ref, wao_ref, wco_ref, wout_ref,
                     gpost_ref):
    mo = _mixed_out(h, attn_bf, u_shift2, u_shift1, u, win_ref, convw_ref, wao_ref, wco_ref, wout_ref)
    return x + _rms(mo, gpost_ref[...])


def _mixed_out(h, attn_bf, u_shift2, u_shift1, u, win_ref, convw_ref, wao_ref, wco_ref, wout_ref):
    attn_b = jnp.dot(attn_bf, wao_ref[...], preferred_element_type=_F32)
    mixed = jax.nn.sigmoid(_proj(h, win_ref, _OFF_GA, _D)) * attn_b
    cw = convw_ref[...]
    z = cw[0:1] * u_shift2
    z = z + cw[1:2] * u_shift1
    z = z + cw[2:3] * u
    bz = (_proj(h, win_ref, _OFF_B, _CONV) * z).astype(_BF)
    conv_b = jnp.dot(bz, wco_ref[...], preferred_element_type=_F32)
    mixed = mixed + jax.nn.sigmoid(_proj(h, win_ref, _OFF_GC, _D)) * conv_b
    return jnp.dot(mixed.astype(_BF), wout_ref[...], preferred_element_type=_F32)


def _run_interleaved(*stages):
    stages = list(stages)
    while stages:
        for g in list(stages):
            try:
                next(g)
            except StopIteration:
                stages.remove(g)


def _mlp_stages(x_ref, y_ref, gpre_ref, gpost_ref, wup_ref, wdown_ref):
    x = x_ref[...]
    hm = _rms(x, gpre_ref[...]).astype(_BF)
    yield
    acc = None
    for j in range(_DFF // _D):
        hj = jnp.dot(hm, wup_ref[:, j * _D:(j + 1) * _D], preferred_element_type=_F32)
        yield
        hj = jnp.square(jnp.maximum(hj, 0.0)).astype(_BF)
        part = jnp.dot(hj, wdown_ref[j * _D:(j + 1) * _D, :], preferred_element_type=_F32)
        acc = part if acc is None else acc + part
        yield
    y_ref[...] = x + _rms(acc, gpost_ref[...])


def _prompt_layer_kernel(sink_ref, x_ref, gpre_ref, gpost_ref, gmpre_ref, gmpost_ref, convw_ref,
                         win_ref, wao_ref, wco_ref, wout_ref, wup_ref, wdown_ref,
                         *rest, tm, tps, layer, first):
    (y_ref, nk_ref, nv_ref, nc_ref,
     kdup, vdup, q4s, ubuf, attn_scr, x1_scr, x1_prev, klast, vlast, ulast) = rest[0 if first else 3:]
    j = pl.program_id(0)
    n_tiles = pl.num_programs(0) - 1
    s = j % tps

    def start_sequence():
        @pl.when(s == 0)
        def _():
            kdup[:, 0:_WIN, :] = jnp.zeros((2, _WIN, _LANES), _BF)
            vdup[:, 0:_WIN, :] = jnp.zeros((2, _WIN, _LANES), _BF)
            ubuf[0:_SUB, :] = jnp.zeros((_SUB, _CONV), _F32)

    def end_sequence():
        @pl.when(s == tps - 1)
        def _():
            _own_layer(nk_ref, first)[...] = klast[...]
            _own_layer(nv_ref, first)[...] = vlast[...]
            _own_layer(nc_ref, first)[...] = ulast[_SUB - 2:_SUB, :]

    def mlp(src):
        return _mlp_stages(src, y_ref, gmpre_ref, gmpost_ref, wup_ref, wdown_ref)

    def mixer():
        return _mixer_prompt_stages(sink_ref, x_ref, gpre_ref, gpost_ref, convw_ref, win_ref, wao_ref, wco_ref,
                                    wout_ref, kdup, vdup, q4s, ubuf, attn_scr, x1_scr, klast, vlast, ulast, s,
                                    tm=tm, layer=layer)

    @pl.when(j == 0)
    def _():
        start_sequence()
        _run_interleaved(mixer())
        end_sequence()

    @pl.when((j > 0) & (j < n_tiles))
    def _():
        start_sequence()
        x1_prev[...] = x1_scr[...]
        mlp_stages, mixer_stages = mlp(x1_prev), mixer()
        for _ in range(2 * (_DFF // _D) - 1):
            next(mlp_stages)
        next(mixer_stages)
        next(mlp_stages)
        next(mixer_stages)
        _run_interleaved(mlp_stages, mixer_stages)
        end_sequence()

    @pl.when(j == n_tiles)
    def _():
        _run_interleaved(mlp(x1_scr))


def _mixer_prompt_stages(sink_ref, x_ref, gpre_ref, gpost_ref, convw_ref, win_ref, wao_ref, wco_ref, wout_ref,
                         kdup, vdup, q4s, ubuf, attn_scr, x1_scr, klast, vlast, ulast, s, *, tm, layer):
    x = x_ref[...]
    h = _rms(x, gpre_ref[...]).astype(_BF)
    lo = lax.broadcasted_iota(jnp.int32, (tm, _LANES), 1) < _HD

    hr = tm // 2
    kv = jnp.concatenate([_proj(h[:hr], win_ref, _OFF_KV, 2 * _KVD), _proj(h[hr:], win_ref, _OFF_KV, 2 * _KVD)],
                         axis=0)
    k = kv[:, :_KVD]
    v = kv[:, _KVD:]
    for val, dst in ((k, kdup), (v, vdup)):
        d0, d1 = _dup_halves(val, lo)
        dst[0, _WIN:_WIN + tm, :] = d0.astype(_BF)
        dst[1, _WIN:_WIN + tm, :] = d1.astype(_BF)

    q = _proj(h, win_ref, _OFF_Q, _ATT) * (_HD ** -0.5)
    for c in range(4):
        qc = q[:, c * _LANES:(c + 1) * _LANES]
        q4s[c, 0] = jnp.where(lo, qc, 0.0).astype(_BF)
        q4s[c, 1] = jnp.where(lo, 0.0, qc).astype(_BF)

    qi = lax.broadcasted_iota(jnp.int32, (_WIN, 2 * _WIN), 0)
    kj = lax.broadcasted_iota(jnp.int32, (_WIN, 2 * _WIN), 1)
    band = _band_mask(qi, kj)
    neg = jnp.float32(-jnp.inf)
    bias = jnp.where(band, 0.0, neg)
    bias_first = jnp.where(band & ((kj >= _WIN) | (s > 0)), 0.0, neg)
    lo_q = lax.broadcasted_iota(jnp.int32, (_WIN, _LANES), 1) < _HD

    def scores_of(qb):
        r0 = qb * _WIN
        out = []
        for g in range(2):
            q4 = jnp.concatenate([q4s[2 * g, 0, r0:r0 + _WIN, :], q4s[2 * g, 1, r0:r0 + _WIN, :],
                                  q4s[2 * g + 1, 0, r0:r0 + _WIN, :], q4s[2 * g + 1, 1, r0:r0 + _WIN, :]], axis=0)
            out.append(_dot_nt(q4, kdup[g, r0:r0 + 2 * _WIN, :]))
        return out

    def finish_block(qb, scores):
        r0 = qb * _WIN
        b = bias_first if qb == 0 else bias
        for g in range(2):
            sinks = [sink_ref[layer, 4 * g + j] for j in range(4)]
            o = _softmax_pv(scores[g], vdup[g, r0:r0 + 2 * _WIN, :], b, sinks, _WIN)
            attn_scr[r0:r0 + _WIN, (2 * g) * _LANES:(2 * g + 1) * _LANES] = jnp.where(lo_q, o[0], o[1]).astype(_BF)
            attn_scr[r0:r0 + _WIN, (2 * g + 1) * _LANES:(2 * g + 2) * _LANES] = (
                jnp.where(lo_q, o[2], o[3]).astype(_BF))

    def conv_input():
        u = _proj(h, win_ref, _OFF_C, _CONV) * _proj(h, win_ref, _OFF_U, _CONV)
        ubuf[_SUB:_SUB + tm, :] = u
        return u

    fillers = [conv_input,
               lambda: jax.nn.sigmoid(_proj(h, win_ref, _OFF_GA, _D)),
               lambda: jax.nn.sigmoid(_proj(h, win_ref, _OFF_GC, _D)),
               lambda: _proj(h, win_ref, _OFF_B, _CONV)]
    filled = []
    n_qb = tm // _WIN
    scores = scores_of(0)
    for qb in range(n_qb):
        if qb < len(fillers):
            filled.append(fillers[qb]())
        finish_block(qb, scores)
        if qb + 1 < n_qb:
            scores = scores_of(qb + 1)
    filled += [f() for f in fillers[n_qb:]]
    u, gate_attn, gate_conv, b_gate = filled

    kdup[:, 0:_WIN, :] = kdup[:, tm:tm + _WIN, :]
    vdup[:, 0:_WIN, :] = vdup[:, tm:tm + _WIN, :]

    mixed = gate_attn * jnp.dot(attn_scr[...], wao_ref[...], preferred_element_type=_F32)
    cw = convw_ref[...]
    z = cw[0:1] * ubuf[_SUB - 2:_SUB - 2 + tm, :]
    z = z + cw[1:2] * ubuf[_SUB - 1:_SUB - 1 + tm, :]
    z = z + cw[2:3] * u
    conv_b = jnp.dot((b_gate * z).astype(_BF), wco_ref[...], preferred_element_type=_F32)
    mixed = (mixed + gate_conv * conv_b).astype(_BF)
    yield
    mo = jnp.dot(mixed, wout_ref[...], preferred_element_type=_F32)
    yield
    x1_scr[...] = x + _rms(mo, gpost_ref[...])
    ubuf[0:_SUB, :] = ubuf[tm:tm + _SUB, :]
    klast[...] = k[tm - _WIN:tm, :]
    vlast[...] = v[tm - _WIN:tm, :]
    ulast[...] = u[tm - _SUB:tm, :]


def _mixer_sample_kernel(sink_ref, x_ref, gpre_ref, gpost_ref, convw_ref, ck_ref, cv_ref, st_ref,
                         win_ref, wao_ref, wco_ref, wout_ref, *rest, sb, t, layer, first):
    y_ref, nk_ref, nv_ref, nc_ref, ubuf = rest[0 if first else 3:]
    nk_ref, nv_ref, nc_ref = (_own_layer(r, first) for r in (nk_ref, nv_ref, nc_ref))
    tr = sb * t
    rows = _NH * t
    x = x_ref[...]
    h = _rms(x, gpre_ref[...]).astype(_BF)
    lo = lax.broadcasted_iota(jnp.int32, (tr, _LANES), 1) < _HD

    kv = _proj(h, win_ref, _OFF_KV, 2 * _KVD)
    k = kv[:, :_KVD]
    v = kv[:, _KVD:]
    q = _proj(h, win_ref, _OFF_Q, _ATT) * (_HD ** -0.5)
    pieces = []
    for hd in range(_NH):
        qc = q[:, (hd // 2) * _LANES:(hd // 2 + 1) * _LANES]
        if hd % 2 != hd // (_NH // 2):
            qc = pltpu.roll(qc, _HD, axis=1)
        piece = jnp.where(lo, qc, 0.0) if hd < _NH // 2 else jnp.where(lo, 0.0, qc)
        pieces.append(piece.reshape(sb, t, _LANES))
    q3 = jnp.concatenate(pieces, axis=1).astype(_BF)

    pad = jnp.zeros((sb, _BF_ROWS - t, _LANES), _F32)
    k_new = jnp.concatenate([k.reshape(sb, t, _KVD), pad], axis=1).astype(_BF)
    v_new = jnp.concatenate([v.reshape(sb, t, _KVD), pad], axis=1).astype(_BF)

    s_old = jnp.einsum('bqf,bfp->bqp', q3, ck_ref[...].astype(_BF), preferred_element_type=_F32)
    s_new = jnp.einsum('bqf,btf->bqt', q3, k_new, preferred_element_type=_F32)
    neg = jnp.float32(-jnp.inf)
    qi_old = lax.broadcasted_iota(jnp.int32, (rows, _WIN), 0) & (t - 1)
    kj_old = lax.broadcasted_iota(jnp.int32, (rows, _WIN), 1)
    qi_new = lax.broadcasted_iota(jnp.int32, (rows, _BF_ROWS), 0) & (t - 1)
    kj_new = lax.broadcasted_iota(jnp.int32, (rows, _BF_ROWS), 1) + _WIN
    s_old = s_old + jnp.where(_band_mask(qi_old, kj_old), 0.0, neg)[None]
    s_new = s_new + jnp.where(_band_mask(qi_new, kj_new), 0.0, neg)[None]
    sink_col = jnp.concatenate([jnp.full((t, 1), sink_ref[layer, hd], _F32) for hd in range(_NH)], axis=0)[None]
    m = jnp.maximum(jnp.maximum(jnp.max(s_old, axis=-1, keepdims=True), jnp.max(s_new, axis=-1, keepdims=True)),
                    sink_col)
    e_old = jnp.exp(s_old - m)
    e_new = jnp.exp(s_new - m)
    rden = 1.0 / (jnp.sum(e_old, axis=-1, keepdims=True) + jnp.sum(e_new, axis=-1, keepdims=True)
                  + jnp.exp(sink_col - m))
    o = (jnp.einsum('bqp,bfp->bqf', e_old.astype(_BF), cv_ref[...].astype(_BF), preferred_element_type=_F32)
         + jnp.einsum('bqt,btf->bqf', e_new.astype(_BF), v_new, preferred_element_type=_F32)) * rden
    chunks = []
    for c in range(_NH // 2):
        halves = []
        for hd in (2 * c, 2 * c + 1):
            oh = o[:, hd * t:(hd + 1) * t, :].reshape(tr, _LANES)
            if hd % 2 != hd // (_NH // 2):
                oh = pltpu.roll(oh, _HD, axis=1)
            halves.append(oh)
        chunks.append(jnp.where(lo, halves[0], halves[1]))
    attn = jnp.concatenate(chunks, axis=1)

    k_t = k.T
    v_t = v.T
    keep = lax.broadcasted_iota(jnp.int32, (_KVD, _WIN), 1) < _WIN - t
    per_chunk = _LANES // t
    for b in range(sb):
        c0 = (b // per_chunk) * _LANES
        shift_new = (_WIN - t - (b % per_chunk) * t) % _LANES
        for src_ref, new_t, dst_ref in ((ck_ref, k_t, nk_ref), (cv_ref, v_t, nv_ref)):
            old = pltpu.roll(src_ref[b], _WIN - t, axis=1)
            new = new_t[:, c0:c0 + _LANES]
            if shift_new:
                new = pltpu.roll(new, shift_new, axis=1)
            dst_ref[b] = jnp.where(keep, old, new)

    u = _proj(h, win_ref, _OFF_C, _CONV) * _proj(h, win_ref, _OFF_U, _CONV)
    u3 = u.reshape(sb, t, _CONV)
    ubuf[:, _SUB - 2:_SUB, :] = st_ref[...]
    ubuf[:, _SUB:_SUB + t, :] = u3
    us2 = ubuf[:, _SUB - 2:_SUB - 2 + t, :].reshape(tr, _CONV)
    us1 = ubuf[:, _SUB - 1:_SUB - 1 + t, :].reshape(tr, _CONV)
    nc_ref[...] = u3[:, t - 2:t, :]
    y_ref[...] = _mix_and_project(x, h, attn.astype(_BF), us2, us1, u,
                                  win_ref, convw_ref, wao_ref, wco_ref, wout_ref, gpost_ref)


def _mlp_kernel(x_ref, gpre_ref, gpost_ref, wup_ref, wdown_ref, y_ref):
    _run_interleaved(_mlp_stages(x_ref, y_ref, gpre_ref, gpost_ref, wup_ref, wdown_ref))


def _layer_spec(shape, layer, **kwargs):
    return pl.BlockSpec((None,) + shape, lambda *_: (layer,) + (0,) * len(shape), **kwargs)


def _weight_spec(shape, layer):
    return _layer_spec(shape, layer, pipeline_mode=pl.Buffered(1))


_SMEM_SPEC = pl.BlockSpec(memory_space=pltpu.SMEM)
_ANY_SPEC = pl.BlockSpec(memory_space=pl.ANY)


def _mixer_weight_specs(layer):
    return [_weight_spec((_D, _IN_DIM), layer), _weight_spec((_ATT, _D), layer),
            _weight_spec((_CONV, _D), layer), _weight_spec((_D, _D), layer)]


def _stacked_out_spec(depth, layer, first, block, index_map):
    lead, l0 = (depth, 0) if first else (None, layer)
    return pl.BlockSpec((lead,) + block, lambda *idx: (l0,) + index_map(*idx))


def _alias_args(prev, n_inputs):
    if prev is None:
        return [], [], {}
    return list(prev), [_ANY_SPEC] * len(prev), {n_inputs + i: 1 + i for i in range(len(prev))}


def _prompt_layer(x, layer, prev, sinks, gpre, gpost, gmpre, gmpost, convw, win, wao, wco, wout, wup, wdown):
    n, seq, _ = x.shape
    depth = win.shape[0]
    tm = _PROMPT_TM
    first = prev is None
    assert seq % tm == 0 and tm % _WIN == 0 and seq >= _WIN
    tps = seq // tm
    n_tiles = n * tps

    def tile_of(t):
        return (t // tps, t % tps, 0)

    mixer_tile = lambda j: tile_of(jnp.minimum(j, n_tiles - 1))
    mlp_tile = lambda j: tile_of(jnp.maximum(j - 1, 0))
    per_b = lambda j: (jnp.minimum(j, n_tiles - 1) // tps, 0, 0)
    inputs = [sinks, x, gpre, gpost, gmpre, gmpost, convw, win, wao, wco, wout, wup, wdown]
    alias_in, alias_specs, aliases = _alias_args(prev, len(inputs))
    out = pl.pallas_call(
        functools.partial(_prompt_layer_kernel, tm=tm, tps=tps, layer=layer, first=first),
        grid=(n_tiles + 1,),
        in_specs=[_SMEM_SPEC,
                  pl.BlockSpec((None, tm, _D), mixer_tile)]
                 + [_layer_spec((1, _D), layer)] * 4 + [_layer_spec((3, _CONV), layer)]
                 + _mixer_weight_specs(layer)
                 + [_weight_spec((_D, _DFF), layer), _weight_spec((_DFF, _D), layer)] + alias_specs,
        out_specs=[pl.BlockSpec((None, tm, _D), mlp_tile),
                   _stacked_out_spec(depth, layer, first, (None, _WIN, _KVD), per_b),
                   _stacked_out_spec(depth, layer, first, (None, _WIN, _KVD), per_b),
                   _stacked_out_spec(depth, layer, first, (None, 2, _CONV), per_b)],
        out_shape=[jax.ShapeDtypeStruct((n, seq, _D), _F32),
                   jax.ShapeDtypeStruct((depth, n, _WIN, _KVD), _F32),
                   jax.ShapeDtypeStruct((depth, n, _WIN, _KVD), _F32),
                   jax.ShapeDtypeStruct((depth, n, 2, _CONV), _F32)],
        scratch_shapes=[pltpu.VMEM((2, _WIN + tm, _LANES), _BF),
                        pltpu.VMEM((2, _WIN + tm, _LANES), _BF),
                        pltpu.VMEM((4, 2, tm, _LANES), _BF),
                        pltpu.VMEM((_SUB + tm, _CONV), _F32),
                        pltpu.VMEM((tm, _ATT), _BF),
                        pltpu.VMEM((tm, _D), _F32),
                        pltpu.VMEM((tm, _D), _F32),
                        pltpu.VMEM((_WIN, _KVD), _F32),
                        pltpu.VMEM((_WIN, _KVD), _F32),
                        pltpu.VMEM((_SUB, _CONV), _F32)],
        input_output_aliases=aliases,
        compiler_params=pltpu.CompilerParams(dimension_semantics=("arbitrary",),
                                             vmem_limit_bytes=_VMEM_LIMIT),
        name="prompt_layer",
    )(*inputs, *alias_in)
    return out[0], tuple(out[1:])


def _mixer_sample(x, layer, prev, ck, cv, st, sinks, gpre, gpost, convw, win, wao, wco, wout):
    n, t, _ = x.shape
    depth = win.shape[0]
    sb = _SAMPLE_SB
    first = prev is None
    assert n % sb == 0 and t == _SUB and ck.shape == (depth, n, _WIN, _KVD)
    tr = sb * t
    rows = lambda i: (i, 0)
    seqs = lambda i: (i, 0, 0)
    cache_spec = pl.BlockSpec((None, sb, _WIN, _KVD), lambda i: (layer, i, 0, 0))
    state_spec = pl.BlockSpec((None, sb, 2, _CONV), lambda i: (layer, i, 0, 0))
    inputs = [sinks, x.reshape(n * t, _D), gpre, gpost, convw, ck, cv, st, win, wao, wco, wout]
    alias_in, alias_specs, aliases = _alias_args(prev, len(inputs))
    out = pl.pallas_call(
        functools.partial(_mixer_sample_kernel, sb=sb, t=t, layer=layer, first=first),
        grid=(n // sb,),
        in_specs=[_SMEM_SPEC,
                  pl.BlockSpec((tr, _D), rows),
                  _layer_spec((1, _D), layer), _layer_spec((1, _D), layer), _layer_spec((3, _CONV), layer),
                  cache_spec, cache_spec, state_spec]
                 + _mixer_weight_specs(layer) + alias_specs,
        out_specs=[pl.BlockSpec((tr, _D), rows),
                   _stacked_out_spec(depth, layer, first, (sb, _WIN, _KVD), seqs),
                   _stacked_out_spec(depth, layer, first, (sb, _WIN, _KVD), seqs),
                   _stacked_out_spec(depth, layer, first, (sb, 2, _CONV), seqs)],
        out_shape=[jax.ShapeDtypeStruct((n * t, _D), _F32),
                   jax.ShapeDtypeStruct((depth, n, _WIN, _KVD), _F32),
                   jax.ShapeDtypeStruct((depth, n, _WIN, _KVD), _F32),
                   jax.ShapeDtypeStruct((depth, n, 2, _CONV), _F32)],
        scratch_shapes=[pltpu.VMEM((sb, 2 * _SUB, _CONV), _F32)],
        input_output_aliases=aliases,
        compiler_params=pltpu.CompilerParams(dimension_semantics=("arbitrary",),
                                             vmem_limit_bytes=_VMEM_LIMIT),
        name="mixer_sample",
    )(*inputs, *alias_in)
    return out[0].reshape(n, t, _D), tuple(out[1:])


def _mlp(x, layer, gpre, gpost, wup, wdown):
    shape = x.shape
    x2 = x.reshape(-1, _D)
    nt = x2.shape[0]
    tm = _MLP_TM
    assert nt % tm == 0
    rows = lambda i: (i, 0)
    y = pl.pallas_call(
        _mlp_kernel,
        grid=(nt // tm,),
        in_specs=[pl.BlockSpec((tm, _D), rows), _layer_spec((1, _D), layer), _layer_spec((1, _D), layer),
                  _weight_spec((_D, _DFF), layer), _weight_spec((_DFF, _D), layer)],
        out_specs=pl.BlockSpec((tm, _D), rows),
        out_shape=jax.ShapeDtypeStruct((nt, _D), _F32),
        compiler_params=pltpu.CompilerParams(dimension_semantics=("parallel",),
                                             vmem_limit_bytes=_VMEM_LIMIT),
        name="mlp",
    )(x2, gpre, gpost, wup, wdown)
    return y.reshape(shape)


def kernel(x_prompt, x_sample, cache_k, cache_v, state_conv, g_mix_pre, g_mix_post, g_mlp_pre, g_mlp_post,
           w_in, attn_sinks, conv_w, w_attn_o, w_conv_o, w_out, w_up, w_down):
    depth = w_in.shape[0]
    n_p, n_dec = x_prompt.shape[0], x_sample.shape[0]
    feature_major = lambda c: jnp.transpose(c, (0, 1, 3, 4, 2)).reshape(depth, n_dec, _KVD, _WIN)
    ck, cv = feature_major(cache_k), feature_major(cache_v)
    win, wao, wco, wout, wup, wdown = (w.astype(_BF) for w in (w_in, w_attn_o, w_conv_o, w_out, w_up, w_down))
    gains = [g.reshape(depth, 1, _D) for g in (g_mix_pre, g_mix_post, g_mlp_pre, g_mlp_post)]
    mix_args = (attn_sinks, gains[0], gains[1], conv_w, win, wao, wco, wout)

    yp, ys = x_prompt, x_sample
    caches_p = caches_s = None
    for l in range(depth):
        yp, caches_p = _prompt_layer(yp, l, caches_p, attn_sinks, *gains, conv_w, win, wao, wco, wout, wup, wdown)
        ys, caches_s = _mixer_sample(ys, l, caches_s, ck, cv, state_conv, *mix_args)
        ys = _mlp(ys, l, gains[2], gains[3], wup, wdown)

    kv5 = lambda a: a.reshape(a.shape[:3] + (2, _HD))
    position_major = lambda a: jnp.transpose(a.reshape(depth, n_dec, 2, _HD, _WIN), (0, 1, 4, 2, 3))
    return (yp, ys, kv5(caches_p[0]), kv5(caches_p[1]), caches_p[2],
            position_major(caches_s[0]), position_major(caches_s[1]), caches_s[2])
```

```python
import functools

import jax
import jax.numpy as jnp
from jax import lax
from jax.experimental import pallas as pl
from jax.experimental.pallas import tpu as pltpu

_D = 1024
_HD = 64
_NH = 8
_ATT = _NH * _HD
_KVD = 128
_CONV = 512
_DFF = 4096
_WIN = 128
_EPS = 1e-6
_LANES = 128
_SUB = 8
_BF_ROWS = 16

_OFF_Q = 0
_OFF_KV = _ATT
_OFF_B = _OFF_KV + 2 * _KVD
_OFF_C = _OFF_B + _CONV
_OFF_U = _OFF_C + _CONV
_OFF_GA = _OFF_U + _CONV
_OFF_GC = _OFF_GA + _D
_IN_DIM = _OFF_GC + _D

_PROMPT_TM = 512
_SAMPLE_SB = 32
_MLP_TM = 512
_VMEM_LIMIT = 62 * 1024 * 1024

_BF = jnp.bfloat16
_F32 = jnp.float32


def _rms(x, g):
    return x * lax.rsqrt(jnp.mean(x * x, axis=-1, keepdims=True) + _EPS) * g


def _proj(h_bf, win_ref, off, width):
    return jnp.dot(h_bf, win_ref[:, off:off + width], preferred_element_type=_F32)


def _dup_halves(val, lo):
    rolled = pltpu.roll(val, _HD, axis=1)
    return jnp.where(lo, val, rolled), jnp.where(lo, rolled, val)


def _band_mask(qi, kj):
    return (kj > qi) & (kj <= qi + _WIN)


def _dot_nt(a, b):
    return lax.dot_general(a, b, (((1,), (1,)), ((), ())), preferred_element_type=_F32)


def _own_layer(ref, first):
    if not first:
        return ref
    if ref.shape[0] > 1:
        ref[1:] = jnp.zeros((ref.shape[0] - 1,) + ref.shape[1:], ref.dtype)
    return ref.at[0]


def _softmax_pv(s, vd, bias, sinks, nq):
    es, rden = [], []
    for j in range(4):
        sj = s[j * nq:(j + 1) * nq] + bias
        m = jnp.maximum(jnp.max(sj, axis=-1, keepdims=True), sinks[j])
        e = jnp.exp(sj - m)
        den = jnp.sum(e, axis=-1, keepdims=True) + jnp.exp(sinks[j] - m)
        es.append(e)
        rden.append(1.0 / den)
    e_all = jnp.concatenate(es, axis=0).astype(_BF)
    o = jnp.dot(e_all, vd, preferred_element_type=_F32)
    return [o[j * nq:(j + 1) * nq] * rden[j] for j in range(4)]


def _mix_and_project(x, h, attn_bf, u_shift2, u_shift1, u, win_ref, convw_ref, wao_ref, wco_ref, wout_ref,
                     gpost_ref):
    mo = _mixed_out(h, attn_bf, u_shift2, u_shift1, u, win_ref, convw_ref, wao_ref, wco_ref, wout_ref)
    return x + _rms(mo, gpost_ref[...])


def _mixed_out(h, attn_bf, u_shift2, u_shift1, u, win_ref, convw_ref, wao_ref, wco_ref, wout_ref):
    attn_b = jnp.dot(attn_bf, wao_ref[...], preferred_element_type=_F32)
    mixed = jax.nn.sigmoid(_proj(h, win_ref, _OFF_GA, _D)) * attn_b
    cw = convw_ref[...]
    z = cw[0:1] * u_shift2
    z = z + cw[1:2] * u_shift1
    z = z + cw[2:3] * u
    bz = (_proj(h, win_ref, _OFF_B, _CONV) * z).astype(_BF)
    conv_b = jnp.dot(bz, wco_ref[...], preferred_element_type=_F32)
    mixed = mixed + jax.nn.sigmoid(_proj(h, win_ref, _OFF_GC, _D)) * conv_b
    return jnp.dot(mixed.astype(_BF), wout_ref[...], preferred_element_type=_F32)


def _run_interleaved(*stages):
    stages = list(stages)
    while stages:
        for g in list(stages):
            try:
                next(g)
            except StopIteration:
                stages.remove(g)


def _mlp_stages(x_ref, y_ref, gpre_ref, gpost_ref, wup_ref, wdown_ref):
    x = x_ref[...]
    y_ref[...] = x
    hm = _rms(x, gpre_ref[...]).astype(_BF)
    yield
    acc = None
    for j in range(_DFF // _D):
        hj = jnp.dot(hm, wup_ref[:, j * _D:(j + 1) * _D], preferred_element_type=_F32)
        yield
        hj = jnp.square(jnp.maximum(hj, 0.0)).astype(_BF)
        part = jnp.dot(hj, wdown_ref[j * _D:(j + 1) * _D, :], preferred_element_type=_F32)
        acc = part if acc is None else acc + part
        yield
    y_ref[...] = y_ref[...] + _rms(acc, gpost_ref[...])


def _prompt_layer_kernel(sink_ref, x_ref, gpre_ref, gpost_ref, gmpre_ref, gmpost_ref, convw_ref,
                         win_ref, wao_ref, wco_ref, wout_ref, wup_ref, wdown_ref,
                         *rest, tm, tps, layer, first, n_cast):
    cast_src, rest = rest[:n_cast], rest[n_cast + (0 if first else 3):]
    y_ref, nk_ref, nv_ref, nc_ref = rest[:4]
    cast_dst, rest = rest[4:4 + n_cast], rest[4 + n_cast:]
    kdup, vdup, q4s, ubuf, attn_scr, x1_scr, klast, vlast, ulast = rest
    j = pl.program_id(0)

    def cast_next_layer():
        for src, dst in zip(cast_src, cast_dst):
            dst[...] = src[...].astype(_BF)

    n_tiles = pl.num_programs(0) - 1
    s = j % tps

    def start_sequence():
        @pl.when(s == 0)
        def _():
            kdup[:, 0:_WIN, :] = jnp.zeros((2, _WIN, _LANES), _BF)
            vdup[:, 0:_WIN, :] = jnp.zeros((2, _WIN, _LANES), _BF)
            ubuf[0:_SUB, :] = jnp.zeros((_SUB, _CONV), _F32)

    def end_sequence():
        @pl.when(s == tps - 1)
        def _():
            _own_layer(nk_ref, first)[...] = klast[...]
            _own_layer(nv_ref, first)[...] = vlast[...]
            _own_layer(nc_ref, first)[...] = ulast[_SUB - 2:_SUB, :]

    def mlp(src):
        return _mlp_stages(src, y_ref, gmpre_ref, gmpost_ref, wup_ref, wdown_ref)

    def mixer():
        return _mixer_prompt_stages(sink_ref, x_ref, gpre_ref, gpost_ref, convw_ref, win_ref, wao_ref, wco_ref,
                                    wout_ref, kdup, vdup, q4s, ubuf, attn_scr, x1_scr, klast, vlast, ulast, s,
                                    tm=tm, layer=layer)

    @pl.when(j == 0)
    def _():
        start_sequence()
        cast_next_layer()
        _run_interleaved(mixer())
        end_sequence()

    @pl.when((j > 0) & (j < n_tiles))
    def _():
        start_sequence()
        mlp_stages, mixer_stages = mlp(x1_scr), mixer()
        for _ in range(2 * (_DFF // _D) - 1):
            next(mlp_stages)
        cast_next_layer()
        next(mixer_stages)
        next(mlp_stages)
        next(mixer_stages)
        _run_interleaved(mlp_stages, mixer_stages)
        end_sequence()

    @pl.when(j == n_tiles)
    def _():
        cast_next_layer()
        _run_interleaved(mlp(x1_scr))


def _mixer_prompt_stages(sink_ref, x_ref, gpre_ref, gpost_ref, convw_ref, win_ref, wao_ref, wco_ref, wout_ref,
                         kdup, vdup, q4s, ubuf, attn_scr, x1_scr, klast, vlast, ulast, s, *, tm, layer):
    x = x_ref[...]
    h = _rms(x, gpre_ref[...]).astype(_BF)
    lo = lax.broadcasted_iota(jnp.int32, (tm, _LANES), 1) < _HD

    hr = tm // 2
    kv = jnp.concatenate([_proj(h[:hr], win_ref, _OFF_KV, 2 * _KVD), _proj(h[hr:], win_ref, _OFF_KV, 2 * _KVD)],
                         axis=0)
    k = kv[:, :_KVD]
    v = kv[:, _KVD:]
    for val, dst in ((k, kdup), (v, vdup)):
        d0, d1 = _dup_halves(val, lo)
        dst[0, _WIN:_WIN + tm, :] = d0.astype(_BF)
        dst[1, _WIN:_WIN + tm, :] = d1.astype(_BF)

    q = _proj(h, win_ref, _OFF_Q, _ATT) * (_HD ** -0.5)
    for c in range(4):
        qc = q[:, c * _LANES:(c + 1) * _LANES]
        q4s[c, 0] = jnp.where(lo, qc, 0.0).astype(_BF)
        q4s[c, 1] = jnp.where(lo, 0.0, qc).astype(_BF)

    qi = lax.broadcasted_iota(jnp.int32, (_WIN, 2 * _WIN), 0)
    kj = lax.broadcasted_iota(jnp.int32, (_WIN, 2 * _WIN), 1)
    band = _band_mask(qi, kj)
    neg = jnp.float32(-jnp.inf)
    bias = jnp.where(band, 0.0, neg)
    bias_first = jnp.where(band & ((kj >= _WIN) | (s > 0)), 0.0, neg)
    lo_q = lax.broadcasted_iota(jnp.int32, (_WIN, _LANES), 1) < _HD

    def scores_of(qb):
        r0 = qb * _WIN
        out = []
        for g in range(2):
            q4 = jnp.concatenate([q4s[2 * g, 0, r0:r0 + _WIN, :], q4s[2 * g, 1, r0:r0 + _WIN, :],
                                  q4s[2 * g + 1, 0, r0:r0 + _WIN, :], q4s[2 * g + 1, 1, r0:r0 + _WIN, :]], axis=0)
            out.append(_dot_nt(q4, kdup[g, r0:r0 + 2 * _WIN, :]))
        return out

    def finish_block(qb, scores):
        r0 = qb * _WIN
        b = bias_first if qb == 0 else bias
        for g in range(2):
            sinks = [sink_ref[layer, 4 * g + j] for j in range(4)]
            o = _softmax_pv(scores[g], vdup[g, r0:r0 + 2 * _WIN, :], b, sinks, _WIN)
            attn_scr[r0:r0 + _WIN, (2 * g) * _LANES:(2 * g + 1) * _LANES] = jnp.where(lo_q, o[0], o[1]).astype(_BF)
            attn_scr[r0:r0 + _WIN, (2 * g + 1) * _LANES:(2 * g + 2) * _LANES] = (
                jnp.where(lo_q, o[2], o[3]).astype(_BF))

    def conv_input():
        u = _proj(h, win_ref, _OFF_C, _CONV) * _proj(h, win_ref, _OFF_U, _CONV)
        ubuf[_SUB:_SUB + tm, :] = u
        return u

    fillers = [conv_input,
               lambda: jax.nn.sigmoid(_proj(h, win_ref, _OFF_GA, _D)),
               lambda: jax.nn.sigmoid(_proj(h, win_ref, _OFF_GC, _D)),
               lambda: _proj(h, win_ref, _OFF_B, _CONV)]
    filled = []
    n_qb = tm // _WIN
    scores = scores_of(0)
    for qb in range(n_qb):
        if qb < len(fillers):
            filled.append(fillers[qb]())
        finish_block(qb, scores)
        if qb + 1 < n_qb:
            scores = scores_of(qb + 1)
    filled += [f() for f in fillers[n_qb:]]
    u, gate_attn, gate_conv, b_gate = filled

    kdup[:, 0:_WIN, :] = kdup[:, tm:tm + _WIN, :]
    vdup[:, 0:_WIN, :] = vdup[:, tm:tm + _WIN, :]

    mixed = gate_attn * jnp.dot(attn_scr[...], wao_ref[...], preferred_element_type=_F32)
    cw = convw_ref[...]
    z = cw[0:1] * ubuf[_SUB - 2:_SUB - 2 + tm, :]
    z = z + cw[1:2] * ubuf[_SUB - 1:_SUB - 1 + tm, :]
    z = z + cw[2:3] * u
    conv_b = jnp.dot((b_gate * z).astype(_BF), wco_ref[...], preferred_element_type=_F32)
    mixed = (mixed + gate_conv * conv_b).astype(_BF)
    yield
    mo = jnp.dot(mixed, wout_ref[...], preferred_element_type=_F32)
    yield
    x1_scr[...] = x + _rms(mo, gpost_ref[...])
    ubuf[0:_SUB, :] = ubuf[tm:tm + _SUB, :]
    klast[...] = k[tm - _WIN:tm, :]
    vlast[...] = v[tm - _WIN:tm, :]
    ulast[...] = u[tm - _SUB:tm, :]


def _mixer_sample_kernel(sink_ref, x_ref, gpre_ref, gpost_ref, convw_ref, ck_ref, cv_ref, st_ref,
                         win_ref, wao_ref, wco_ref, wout_ref, *rest, sb, t, layer, first):
    y_ref, nk_ref, nv_ref, nc_ref, ubuf = rest[0 if first else 3:]
    nk_ref, nv_ref, nc_ref = (_own_layer(r, first) for r in (nk_ref, nv_ref, nc_ref))
    tr = sb * t
    rows = _NH * t
    x = x_ref[...]
    h = _rms(x, gpre_ref[...]).astype(_BF)
    lo = lax.broadcasted_iota(jnp.int32, (tr, _LANES), 1) < _HD

    kv = _proj(h, win_ref, _OFF_KV, 2 * _KVD)
    k = kv[:, :_KVD]
    v = kv[:, _KVD:]
    q = _proj(h, win_ref, _OFF_Q, _ATT) * (_HD ** -0.5)
    pieces = []
    for hd in range(_NH):
        qc = q[:, (hd // 2) * _LANES:(hd // 2 + 1) * _LANES]
        if hd % 2 != hd // (_NH // 2):
            qc = pltpu.roll(qc, _HD, axis=1)
        piece = jnp.where(lo, qc, 0.0) if hd < _NH // 2 else jnp.where(lo, 0.0, qc)
        pieces.append(piece.reshape(sb, t, _LANES))
    q3 = jnp.concatenate(pieces, axis=1).astype(_BF)

    pad = jnp.zeros((sb, _BF_ROWS - t, _LANES), _F32)
    k_new = jnp.concatenate([k.reshape(sb, t, _KVD), pad], axis=1).astype(_BF)
    v_new = jnp.concatenate([v.reshape(sb, t, _KVD), pad], axis=1).astype(_BF)

    s_old = jnp.einsum('bqf,bfp->bqp', q3, ck_ref[...].astype(_BF), preferred_element_type=_F32)
    s_new = jnp.einsum('bqf,btf->bqt', q3, k_new, preferred_element_type=_F32)
    neg = jnp.float32(-jnp.inf)
    qi_old = lax.broadcasted_iota(jnp.int32, (rows, _WIN), 0) & (t - 1)
    kj_old = lax.broadcasted_iota(jnp.int32, (rows, _WIN), 1)
    qi_new = lax.broadcasted_iota(jnp.int32, (rows, _BF_ROWS), 0) & (t - 1)
    kj_new = lax.broadcasted_iota(jnp.int32, (rows, _BF_ROWS), 1) + _WIN
    s_old = s_old + jnp.where(_band_mask(qi_old, kj_old), 0.0, neg)[None]
    s_new = s_new + jnp.where(_band_mask(qi_new, kj_new), 0.0, neg)[None]
    sink_col = jnp.concatenate([jnp.full((t, 1), sink_ref[layer, hd], _F32) for hd in range(_NH)], axis=0)[None]
    m = jnp.maximum(jnp.maximum(jnp.max(s_old, axis=-1, keepdims=True), jnp.max(s_new, axis=-1, keepdims=True)),
                    sink_col)
    e_old = jnp.exp(s_old - m)
    e_new = jnp.exp(s_new - m)
    rden = 1.0 / (jnp.sum(e_old, axis=-1, keepdims=True) + jnp.sum(e_new, axis=-1, keepdims=True)
                  + jnp.exp(sink_col - m))
    o = (jnp.einsum('bqp,bfp->bqf', e_old.astype(_BF), cv_ref[...].astype(_BF), preferred_element_type=_F32)
         + jnp.einsum('bqt,btf->bqf', e_new.astype(_BF), v_new, preferred_element_type=_F32)) * rden
    chunks = []
    for c in range(_NH // 2):
        halves = []
        for hd in (2 * c, 2 * c + 1):
            oh = o[:, hd * t:(hd + 1) * t, :].reshape(tr, _LANES)
            if hd % 2 != hd // (_NH // 2):
                oh = pltpu.roll(oh, _HD, axis=1)
            halves.append(oh)
        chunks.append(jnp.where(lo, halves[0], halves[1]))
    attn = jnp.concatenate(chunks, axis=1)

    k_t = k.T
    v_t = v.T
    keep = lax.broadcasted_iota(jnp.int32, (_KVD, _WIN), 1) < _WIN - t
    per_chunk = _LANES // t
    for b in range(sb):
        c0 = (b // per_chunk) * _LANES
        shift_new = (_WIN - t - (b % per_chunk) * t) % _LANES
        for src_ref, new_t, dst_ref in ((ck_ref, k_t, nk_ref), (cv_ref, v_t, nv_ref)):
            old = pltpu.roll(src_ref[b], _WIN - t, axis=1)
            new = new_t[:, c0:c0 + _LANES]
            if shift_new:
                new = pltpu.roll(new, shift_new, axis=1)
            dst_ref[b] = jnp.where(keep, old, new)

    u = _proj(h, win_ref, _OFF_C, _CONV) * _proj(h, win_ref, _OFF_U, _CONV)
    u3 = u.reshape(sb, t, _CONV)
    ubuf[:, _SUB - 2:_SUB, :] = st_ref[...]
    ubuf[:, _SUB:_SUB + t, :] = u3
    us2 = ubuf[:, _SUB - 2:_SUB - 2 + t, :].reshape(tr, _CONV)
    us1 = ubuf[:, _SUB - 1:_SUB - 1 + t, :].reshape(tr, _CONV)
    nc_ref[...] = u3[:, t - 2:t, :]
    y_ref[...] = _mix_and_project(x, h, attn.astype(_BF), us2, us1, u,
                                  win_ref, convw_ref, wao_ref, wco_ref, wout_ref, gpost_ref)


def _mlp_kernel(x_ref, gpre_ref, gpost_ref, wup_ref, wdown_ref, y_ref):
    _run_interleaved(_mlp_stages(x_ref, y_ref, gpre_ref, gpost_ref, wup_ref, wdown_ref))


def _layer_spec(shape, layer, **kwargs):
    return pl.BlockSpec((None,) + shape, lambda *_: (layer,) + (0,) * len(shape), **kwargs)


def _weight_spec(w):
    return pl.BlockSpec(w.shape, lambda *_: (0, 0), pipeline_mode=pl.Buffered(1))


_SMEM_SPEC = pl.BlockSpec(memory_space=pltpu.SMEM)
_ANY_SPEC = pl.BlockSpec(memory_space=pl.ANY)


def _stacked_out_spec(depth, layer, first, block, index_map):
    lead, l0 = (depth, 0) if first else (None, layer)
    return pl.BlockSpec((lead,) + block, lambda *idx: (l0,) + index_map(*idx))


def _alias_args(prev, n_inputs):
    if prev is None:
        return [], [], {}
    return list(prev), [_ANY_SPEC] * len(prev), {n_inputs + i: 1 + i for i in range(len(prev))}


def _prompt_layer(x, layer, prev, sinks, gpre, gpost, gmpre, gmpost, convw, weights, next_f32):
    n, seq, _ = x.shape
    depth = sinks.shape[0]
    tm = _PROMPT_TM
    first = prev is None
    assert seq % tm == 0 and tm % _WIN == 0 and seq >= _WIN
    tps = seq // tm
    n_tiles = n * tps

    def tile_of(t):
        return (t // tps, t % tps, 0)

    clamp = lambda j: jnp.minimum(j, n_tiles - 1)
    mixer_tile = lambda j: tile_of(clamp(j))
    mlp_tile = lambda j: tile_of(jnp.maximum(j - 1, 0))
    per_b = lambda j: (clamp(j) // tps, 0, 0)

    cast_in, cast_in_specs, cast_out_specs, cast_out_shapes = [], [], [], []
    for w in (next_f32 or ()):
        rows, cols = w.shape[1:]
        assert rows % (n_tiles * _BF_ROWS) == 0
        chunk = rows // n_tiles
        cast_in.append(w)
        cast_in_specs.append(pl.BlockSpec((None, chunk, cols), lambda j: (layer + 1, clamp(j), 0)))
        cast_out_specs.append(pl.BlockSpec((chunk, cols), lambda j: (clamp(j), 0)))
        cast_out_shapes.append(jax.ShapeDtypeStruct((rows, cols), _BF))

    inputs = [sinks, x, gpre, gpost, gmpre, gmpost, convw, *weights, *cast_in]
    alias_in, alias_specs, aliases = _alias_args(prev, len(inputs))
    out = pl.pallas_call(
        functools.partial(_prompt_layer_kernel, tm=tm, tps=tps, layer=layer, first=first, n_cast=len(cast_in)),
        grid=(n_tiles + 1,),
        in_specs=[_SMEM_SPEC,
                  pl.BlockSpec((None, tm, _D), mixer_tile)]
                 + [_layer_spec((1, _D), layer)] * 4 + [_layer_spec((3, _CONV), layer)]
                 + [_weight_spec(w) for w in weights] + cast_in_specs + alias_specs,
        out_specs=[pl.BlockSpec((None, tm, _D), mlp_tile),
                   _stacked_out_spec(depth, layer, first, (None, _WIN, _KVD), per_b),
                   _stacked_out_spec(depth, layer, first, (None, _WIN, _KVD), per_b),
                   _stacked_out_spec(depth, layer, first, (None, 2, _CONV), per_b)] + cast_out_specs,
        out_shape=[jax.ShapeDtypeStruct((n, seq, _D), _F32),
                   jax.ShapeDtypeStruct((depth, n, _WIN, _KVD), _F32),
                   jax.ShapeDtypeStruct((depth, n, _WIN, _KVD), _F32),
                   jax.ShapeDtypeStruct((depth, n, 2, _CONV), _F32)] + cast_out_shapes,
        scratch_shapes=[pltpu.VMEM((2, _WIN + tm, _LANES), _BF),
                        pltpu.VMEM((2, _WIN + tm, _LANES), _BF),
                        pltpu.VMEM((4, 2, tm, _LANES), _BF),
                        pltpu.VMEM((_SUB + tm, _CONV), _F32),
                        pltpu.VMEM((tm, _ATT), _BF),
                        pltpu.VMEM((tm, _D), _F32),
                        pltpu.VMEM((_WIN, _KVD), _F32),
                        pltpu.VMEM((_WIN, _KVD), _F32),
                        pltpu.VMEM((_SUB, _CONV), _F32)],
        input_output_aliases=aliases,
        compiler_params=pltpu.CompilerParams(dimension_semantics=("arbitrary",),
                                             vmem_limit_bytes=_VMEM_LIMIT),
        name="prompt_layer",
    )(*inputs, *alias_in)
    return out[0], tuple(out[1:4]), tuple(out[4:])


def _mixer_sample(x, layer, prev, ck, cv, st, sinks, gpre, gpost, convw, win, wao, wco, wout):
    n, t, _ = x.shape
    depth = sinks.shape[0]
    sb = _SAMPLE_SB
    first = prev is None
    assert n % sb == 0 and t == _SUB and ck.shape == (depth, n, _WIN, _KVD)
    tr = sb * t
    rows = lambda i: (i, 0)
    seqs = lambda i: (i, 0, 0)
    cache_spec = pl.BlockSpec((None, sb, _WIN, _KVD), lambda i: (layer, i, 0, 0))
    state_spec = pl.BlockSpec((None, sb, 2, _CONV), lambda i: (layer, i, 0, 0))
    inputs = [sinks, x.reshape(n * t, _D), gpre, gpost, convw, ck, cv, st, win, wao, wco, wout]
    alias_in, alias_specs, aliases = _alias_args(prev, len(inputs))
    out = pl.pallas_call(
        functools.partial(_mixer_sample_kernel, sb=sb, t=t, layer=layer, first=first),
        grid=(n // sb,),
        in_specs=[_SMEM_SPEC,
                  pl.BlockSpec((tr, _D), rows),
                  _layer_spec((1, _D), layer), _layer_spec((1, _D), layer), _layer_spec((3, _CONV), layer),
                  cache_spec, cache_spec, state_spec]
                 + [_weight_spec(w) for w in (win, wao, wco, wout)] + alias_specs,
        out_specs=[pl.BlockSpec((tr, _D), rows),
                   _stacked_out_spec(depth, layer, first, (sb, _WIN, _KVD), seqs),
                   _stacked_out_spec(depth, layer, first, (sb, _WIN, _KVD), seqs),
                   _stacked_out_spec(depth, layer, first, (sb, 2, _CONV), seqs)],
        out_shape=[jax.ShapeDtypeStruct((n * t, _D), _F32),
                   jax.ShapeDtypeStruct((depth, n, _WIN, _KVD), _F32),
                   jax.ShapeDtypeStruct((depth, n, _WIN, _KVD), _F32),
                   jax.ShapeDtypeStruct((depth, n, 2, _CONV), _F32)],
        scratch_shapes=[pltpu.VMEM((sb, 2 * _SUB, _CONV), _F32)],
        input_output_aliases=aliases,
        compiler_params=pltpu.CompilerParams(dimension_semantics=("arbitrary",),
                                             vmem_limit_bytes=_VMEM_LIMIT),
        name="mixer_sample",
    )(*inputs, *alias_in)
    return out[0].reshape(n, t, _D), tuple(out[1:])


def _mlp(x, layer, gpre, gpost, wup, wdown):
    shape = x.shape
    x2 = x.reshape(-1, _D)
    nt = x2.shape[0]
    tm = _MLP_TM
    assert nt % tm == 0
    rows = lambda i: (i, 0)
    y = pl.pallas_call(
        _mlp_kernel,
        grid=(nt // tm,),
        in_specs=[pl.BlockSpec((tm, _D), rows), _layer_spec((1, _D), layer), _layer_spec((1, _D), layer),
                  _weight_spec(wup), _weight_spec(wdown)],
        out_specs=pl.BlockSpec((tm, _D), rows),
        out_shape=jax.ShapeDtypeStruct((nt, _D), _F32),
        compiler_params=pltpu.CompilerParams(dimension_semantics=("parallel",),
                                             vmem_limit_bytes=_VMEM_LIMIT),
        name="mlp",
    )(x2, gpre, gpost, wup, wdown)
    return y.reshape(shape)


def kernel(x_prompt, x_sample, cache_k, cache_v, state_conv, g_mix_pre, g_mix_post, g_mlp_pre, g_mlp_post,
           w_in, attn_sinks, conv_w, w_attn_o, w_conv_o, w_out, w_up, w_down):
    depth = w_in.shape[0]
    n_p, n_dec = x_prompt.shape[0], x_sample.shape[0]
    feature_major = lambda c: jnp.transpose(c, (0, 1, 3, 4, 2)).reshape(depth, n_dec, _KVD, _WIN)
    ck, cv = feature_major(cache_k), feature_major(cache_v)
    weights_f32 = (w_in, w_attn_o, w_conv_o, w_out, w_up, w_down)
    gains = [g.reshape(depth, 1, _D) for g in (g_mix_pre, g_mix_post, g_mlp_pre, g_mlp_post)]

    weights = tuple(w[0].astype(_BF) for w in weights_f32)
    yp, ys = x_prompt, x_sample
    caches_p = caches_s = None
    for l in range(depth):
        yp, caches_p, next_weights = _prompt_layer(yp, l, caches_p, attn_sinks, *gains, conv_w, weights,
                                                   weights_f32 if l + 1 < depth else None)
        ys, caches_s = _mixer_sample(ys, l, caches_s, ck, cv, state_conv, attn_sinks, gains[0], gains[1], conv_w,
                                     *weights[:4])
        ys = _mlp(ys, l, gains[2], gains[3], *weights[4:])
        weights = next_weights

    kv5 = lambda a: a.reshape(a.shape[:3] + (2, _HD))
    position_major = lambda a: jnp.transpose(a.reshape(depth, n_dec, 2, _HD, _WIN), (0, 1, 4, 2, 3))
    return (yp, ys, kv5(caches_p[0]), kv5(caches_p[1]), caches_p[2],
            position_major(caches_s[0]), position_major(caches_s[1]), caches_s[2])
```

```python
import functools

import jax
import jax.numpy as jnp
from jax import lax
from jax.experimental import pallas as pl
from jax.experimental.pallas import tpu as pltpu

_D = 1024
_HD = 64
_NH = 8
_ATT = _NH * _HD
_KVD = 128
_CONV = 512
_DFF = 4096
_WIN = 128
_EPS = 1e-6
_LANES = 128
_SUB = 8
_BF_ROWS = 16

_OFF_Q = 0
_OFF_KV = _ATT
_OFF_B = _OFF_KV + 2 * _KVD
_OFF_C = _OFF_B + _CONV
_OFF_U = _OFF_C + _CONV
_OFF_GA = _OFF_U + _CONV
_OFF_GC = _OFF_GA + _D
_IN_DIM = _OFF_GC + _D

_PROMPT_TM = 512
_SAMPLE_SB = 32
_MLP_TM = 512
_CAST_STEPS = 32
_VMEM_LIMIT = 62 * 1024 * 1024

_BF = jnp.bfloat16
_F32 = jnp.float32


def _rms(x, g):
    return x * lax.rsqrt(jnp.mean(x * x, axis=-1, keepdims=True) + _EPS) * g


def _proj(h_bf, win_ref, off, width):
    return jnp.dot(h_bf, win_ref[:, off:off + width], preferred_element_type=_F32)


def _dup_halves(val, lo):
    rolled = pltpu.roll(val, _HD, axis=1)
    return jnp.where(lo, val, rolled), jnp.where(lo, rolled, val)


def _band_mask(qi, kj):
    return (kj > qi) & (kj <= qi + _WIN)


def _dot_nt(a, b):
    return lax.dot_general(a, b, (((1,), (1,)), ((), ())), preferred_element_type=_F32)


def _own_layer(ref, first):
    if not first:
        return ref
    if ref.shape[0] > 1:
        ref[1:] = jnp.zeros((ref.shape[0] - 1,) + ref.shape[1:], ref.dtype)
    return ref.at[0]


def _softmax_pv(s, vd, bias, sinks, nq):
    es, rden = [], []
    for j in range(4):
        sj = s[j * nq:(j + 1) * nq] + bias
        m = jnp.maximum(jnp.max(sj, axis=-1, keepdims=True), sinks[j])
        e = jnp.exp(sj - m)
        den = jnp.sum(e, axis=-1, keepdims=True) + jnp.exp(sinks[j] - m)
        es.append(e)
        rden.append(1.0 / den)
    e_all = jnp.concatenate(es, axis=0).astype(_BF)
    o = jnp.dot(e_all, vd, preferred_element_type=_F32)
    return [o[j * nq:(j + 1) * nq] * rden[j] for j in range(4)]


def _mix_and_project(x, h, attn_bf, u_shift2, u_shift1, u, win_ref, convw_ref, wao_ref, wco_ref, wout_ref,
                     gpost_ref):
    mo = _mixed_out(h, attn_bf, u_shift2, u_shift1, u, win_ref, convw_ref, wao_ref, wco_ref, wout_ref)
    return x + _rms(mo, gpost_ref[...])


def _mixed_out(h, attn_bf, u_shift2, u_shift1, u, win_ref, convw_ref, wao_ref, wco_ref, wout_ref):
    attn_b = jnp.dot(attn_bf, wao_ref[...], preferred_element_type=_F32)
    mixed = jax.nn.sigmoid(_proj(h, win_ref, _OFF_GA, _D)) * attn_b
    cw = convw_ref[...]
    z = cw[0:1] * u_shift2
    z = z + cw[1:2] * u_shift1
    z = z + cw[2:3] * u
    bz = (_proj(h, win_ref, _OFF_B, _CONV) * z).astype(_BF)
    conv_b = jnp.dot(bz, wco_ref[...], preferred_element_type=_F32)
    mixed = mixed + jax.nn.sigmoid(_proj(h, win_ref, _OFF_GC, _D)) * conv_b
    return jnp.dot(mixed.astype(_BF), wout_ref[...], preferred_element_type=_F32)


def _run_interleaved(*stages):
    stages = list(stages)
    while stages:
        for g in list(stages):
            try:
                next(g)
            except StopIteration:
                stages.remove(g)


def _mlp_stages(x_ref, y_ref, gpre_ref, gpost_ref, wup_ref, wdown_ref):
    x = x_ref[...]
    y_ref[...] = x
    hm = _rms(x, gpre_ref[...]).astype(_BF)
    yield
    acc = None
    for j in range(_DFF // _D):
        hj = jnp.dot(hm, wup_ref[:, j * _D:(j + 1) * _D], preferred_element_type=_F32)
        yield
        hj = jnp.square(jnp.maximum(hj, 0.0)).astype(_BF)
        part = jnp.dot(hj, wdown_ref[j * _D:(j + 1) * _D, :], preferred_element_type=_F32)
        acc = part if acc is None else acc + part
        yield
    y_ref[...] = y_ref[...] + _rms(acc, gpost_ref[...])


_N_WEIGHTS = 6


def _prompt_layer_kernel(sink_ref, x_ref, gpre_ref, gpost_ref, gmpre_ref, gmpost_ref, convw_ref,
                         *rest, tm, tps, n_pro, layer, first):
    w_f32, rest = rest[:_N_WEIGHTS], rest[_N_WEIGHTS + (0 if first else 3):]
    y_ref, nk_ref, nv_ref, nc_ref = rest[:4]
    w_out_bf, rest = rest[4:4 + _N_WEIGHTS], rest[4 + _N_WEIGHTS:]
    w_vmem, rest = rest[:_N_WEIGHTS], rest[_N_WEIGHTS:]
    win_ref, wao_ref, wco_ref, wout_ref, wup_ref, wdown_ref = w_vmem
    kdup, vdup, q4s, ubuf, attn_scr, x1_scr, klast, vlast, ulast = rest
    step = pl.program_id(0)

    @pl.when(step < n_pro)
    def _():
        for src, dst_hbm, dst_vmem in zip(w_f32, w_out_bf, w_vmem):
            chunk = src.shape[0]
            w = src[...].astype(_BF)
            dst_hbm[...] = w
            dst_vmem[pl.ds(pl.multiple_of(step * chunk, chunk), chunk), :] = w

    j = step - n_pro
    n_tiles = pl.num_programs(0) - n_pro - 1
    s = jnp.maximum(j, 0) % tps

    def start_sequence():
        @pl.when(s == 0)
        def _():
            kdup[:, 0:_WIN, :] = jnp.zeros((2, _WIN, _LANES), _BF)
            vdup[:, 0:_WIN, :] = jnp.zeros((2, _WIN, _LANES), _BF)
            ubuf[0:_SUB, :] = jnp.zeros((_SUB, _CONV), _F32)

    def end_sequence():
        @pl.when(s == tps - 1)
        def _():
            _own_layer(nk_ref, first)[...] = klast[...]
            _own_layer(nv_ref, first)[...] = vlast[...]
            _own_layer(nc_ref, first)[...] = ulast[_SUB - 2:_SUB, :]

    def mlp(src):
        return _mlp_stages(src, y_ref, gmpre_ref, gmpost_ref, wup_ref, wdown_ref)

    def mixer():
        return _mixer_prompt_stages(sink_ref, x_ref, gpre_ref, gpost_ref, convw_ref, win_ref, wao_ref, wco_ref,
                                    wout_ref, kdup, vdup, q4s, ubuf, attn_scr, x1_scr, klast, vlast, ulast, s,
                                    tm=tm, layer=layer)

    @pl.when(j == 0)
    def _():
        start_sequence()
        _run_interleaved(mixer())
        end_sequence()

    @pl.when((j > 0) & (j < n_tiles))
    def _():
        start_sequence()
        mlp_stages, mixer_stages = mlp(x1_scr), mixer()
        for _ in range(2 * (_DFF // _D) - 1):
            next(mlp_stages)
        next(mixer_stages)
        next(mlp_stages)
        next(mixer_stages)
        _run_interleaved(mlp_stages, mixer_stages)
        end_sequence()

    @pl.when(j == n_tiles)
    def _():
        _run_interleaved(mlp(x1_scr))


def _mixer_prompt_stages(sink_ref, x_ref, gpre_ref, gpost_ref, convw_ref, win_ref, wao_ref, wco_ref, wout_ref,
                         kdup, vdup, q4s, ubuf, attn_scr, x1_scr, klast, vlast, ulast, s, *, tm, layer):
    x = x_ref[...]
    h = _rms(x, gpre_ref[...]).astype(_BF)
    lo = lax.broadcasted_iota(jnp.int32, (tm, _LANES), 1) < _HD

    hr = tm // 2
    kv = jnp.concatenate([_proj(h[:hr], win_ref, _OFF_KV, 2 * _KVD), _proj(h[hr:], win_ref, _OFF_KV, 2 * _KVD)],
                         axis=0)
    k = kv[:, :_KVD]
    v = kv[:, _KVD:]
    for val, dst in ((k, kdup), (v, vdup)):
        d0, d1 = _dup_halves(val, lo)
        dst[0, _WIN:_WIN + tm, :] = d0.astype(_BF)
        dst[1, _WIN:_WIN + tm, :] = d1.astype(_BF)

    q = _proj(h, win_ref, _OFF_Q, _ATT) * (_HD ** -0.5)
    for c in range(4):
        qc = q[:, c * _LANES:(c + 1) * _LANES]
        q4s[c, 0] = jnp.where(lo, qc, 0.0).astype(_BF)
        q4s[c, 1] = jnp.where(lo, 0.0, qc).astype(_BF)

    qi = lax.broadcasted_iota(jnp.int32, (_WIN, 2 * _WIN), 0)
    kj = lax.broadcasted_iota(jnp.int32, (_WIN, 2 * _WIN), 1)
    band = _band_mask(qi, kj)
    neg = jnp.float32(-jnp.inf)
    bias = jnp.where(band, 0.0, neg)
    bias_first = jnp.where(band & ((kj >= _WIN) | (s > 0)), 0.0, neg)
    lo_q = lax.broadcasted_iota(jnp.int32, (_WIN, _LANES), 1) < _HD

    def scores_of(qb):
        r0 = qb * _WIN
        out = []
        for g in range(2):
            q4 = jnp.concatenate([q4s[2 * g, 0, r0:r0 + _WIN, :], q4s[2 * g, 1, r0:r0 + _WIN, :],
                                  q4s[2 * g + 1, 0, r0:r0 + _WIN, :], q4s[2 * g + 1, 1, r0:r0 + _WIN, :]], axis=0)
            out.append(_dot_nt(q4, kdup[g, r0:r0 + 2 * _WIN, :]))
        return out

    def finish_block(qb, scores):
        r0 = qb * _WIN
        b = bias_first if qb == 0 else bias
        for g in range(2):
            sinks = [sink_ref[layer, 4 * g + j] for j in range(4)]
            o = _softmax_pv(scores[g], vdup[g, r0:r0 + 2 * _WIN, :], b, sinks, _WIN)
            attn_scr[r0:r0 + _WIN, (2 * g) * _LANES:(2 * g + 1) * _LANES] = jnp.where(lo_q, o[0], o[1]).astype(_BF)
            attn_scr[r0:r0 + _WIN, (2 * g + 1) * _LANES:(2 * g + 2) * _LANES] = (
                jnp.where(lo_q, o[2], o[3]).astype(_BF))

    def conv_input():
        u = _proj(h, win_ref, _OFF_C, _CONV) * _proj(h, win_ref, _OFF_U, _CONV)
        ubuf[_SUB:_SUB + tm, :] = u
        return u

    fillers = [conv_input,
               lambda: jax.nn.sigmoid(_proj(h, win_ref, _OFF_GA, _D)),
               lambda: jax.nn.sigmoid(_proj(h, win_ref, _OFF_GC, _D)),
               lambda: _proj(h, win_ref, _OFF_B, _CONV)]
    filled = []
    n_qb = tm // _WIN
    scores = scores_of(0)
    for qb in range(n_qb):
        if qb < len(fillers):
            filled.append(fillers[qb]())
        finish_block(qb, scores)
        if qb + 1 < n_qb:
            scores = scores_of(qb + 1)
    filled += [f() for f in fillers[n_qb:]]
    u, gate_attn, gate_conv, b_gate = filled

    kdup[:, 0:_WIN, :] = kdup[:, tm:tm + _WIN, :]
    vdup[:, 0:_WIN, :] = vdup[:, tm:tm + _WIN, :]

    mixed = gate_attn * jnp.dot(attn_scr[...], wao_ref[...], preferred_element_type=_F32)
    cw = convw_ref[...]
    z = cw[0:1] * ubuf[_SUB - 2:_SUB - 2 + tm, :]
    z = z + cw[1:2] * ubuf[_SUB - 1:_SUB - 1 + tm, :]
    z = z + cw[2:3] * u
    conv_b = jnp.dot((b_gate * z).astype(_BF), wco_ref[...], preferred_element_type=_F32)
    mixed = (mixed + gate_conv * conv_b).astype(_BF)
    yield
    mo = jnp.dot(mixed, wout_ref[...], preferred_element_type=_F32)
    yield
    x1_scr[...] = x + _rms(mo, gpost_ref[...])
    ubuf[0:_SUB, :] = ubuf[tm:tm + _SUB, :]
    klast[...] = k[tm - _WIN:tm, :]
    vlast[...] = v[tm - _WIN:tm, :]
    ulast[...] = u[tm - _SUB:tm, :]


def _mixer_sample_kernel(sink_ref, x_ref, gpre_ref, gpost_ref, convw_ref, ck_ref, cv_ref, st_ref,
                         win_ref, wao_ref, wco_ref, wout_ref, *rest, sb, t, layer, first):
    y_ref, nk_ref, nv_ref, nc_ref, ubuf = rest[0 if first else 3:]
    nk_ref, nv_ref, nc_ref = (_own_layer(r, first) for r in (nk_ref, nv_ref, nc_ref))
    tr = sb * t
    rows = _NH * t
    x = x_ref[...]
    h = _rms(x, gpre_ref[...]).astype(_BF)
    lo = lax.broadcasted_iota(jnp.int32, (tr, _LANES), 1) < _HD

    kv = _proj(h, win_ref, _OFF_KV, 2 * _KVD)
    k = kv[:, :_KVD]
    v = kv[:, _KVD:]
    q = _proj(h, win_ref, _OFF_Q, _ATT) * (_HD ** -0.5)
    pieces = []
    for hd in range(_NH):
        qc = q[:, (hd // 2) * _LANES:(hd // 2 + 1) * _LANES]
        if hd % 2 != hd // (_NH // 2):
            qc = pltpu.roll(qc, _HD, axis=1)
        piece = jnp.where(lo, qc, 0.0) if hd < _NH // 2 else jnp.where(lo, 0.0, qc)
        pieces.append(piece.reshape(sb, t, _LANES))
    q3 = jnp.concatenate(pieces, axis=1).astype(_BF)

    pad = jnp.zeros((sb, _BF_ROWS - t, _LANES), _F32)
    k_new = jnp.concatenate([k.reshape(sb, t, _KVD), pad], axis=1).astype(_BF)
    v_new = jnp.concatenate([v.reshape(sb, t, _KVD), pad], axis=1).astype(_BF)

    s_old = jnp.einsum('bqf,bfp->bqp', q3, ck_ref[...].astype(_BF), preferred_element_type=_F32)
    s_new = jnp.einsum('bqf,btf->bqt', q3, k_new, preferred_element_type=_F32)
    neg = jnp.float32(-jnp.inf)
    qi_old = lax.broadcasted_iota(jnp.int32, (rows, _WIN), 0) & (t - 1)
    kj_old = lax.broadcasted_iota(jnp.int32, (rows, _WIN), 1)
    qi_new = lax.broadcasted_iota(jnp.int32, (rows, _BF_ROWS), 0) & (t - 1)
    kj_new = lax.broadcasted_iota(jnp.int32, (rows, _BF_ROWS), 1) + _WIN
    s_old = s_old + jnp.where(_band_mask(qi_old, kj_old), 0.0, neg)[None]
    s_new = s_new + jnp.where(_band_mask(qi_new, kj_new), 0.0, neg)[None]
    sink_col = jnp.concatenate([jnp.full((t, 1), sink_ref[layer, hd], _F32) for hd in range(_NH)], axis=0)[None]
    m = jnp.maximum(jnp.maximum(jnp.max(s_old, axis=-1, keepdims=True), jnp.max(s_new, axis=-1, keepdims=True)),
                    sink_col)
    e_old = jnp.exp(s_old - m)
    e_new = jnp.exp(s_new - m)
    rden = 1.0 / (jnp.sum(e_old, axis=-1, keepdims=True) + jnp.sum(e_new, axis=-1, keepdims=True)
                  + jnp.exp(sink_col - m))
    o = (jnp.einsum('bqp,bfp->bqf', e_old.astype(_BF), cv_ref[...].astype(_BF), preferred_element_type=_F32)
         + jnp.einsum('bqt,btf->bqf', e_new.astype(_BF), v_new, preferred_element_type=_F32)) * rden
    chunks = []
    for c in range(_NH // 2):
        halves = []
        for hd in (2 * c, 2 * c + 1):
            oh = o[:, hd * t:(hd + 1) * t, :].reshape(tr, _LANES)
            if hd % 2 != hd // (_NH // 2):
                oh = pltpu.roll(oh, _HD, axis=1)
            halves.append(oh)
        chunks.append(jnp.where(lo, halves[0], halves[1]))
    attn = jnp.concatenate(chunks, axis=1)

    k_t = k.T
    v_t = v.T
    keep = lax.broadcasted_iota(jnp.int32, (_KVD, _WIN), 1) < _WIN - t
    per_chunk = _LANES // t
    for b in range(sb):
        c0 = (b // per_chunk) * _LANES
        shift_new = (_WIN - t - (b % per_chunk) * t) % _LANES
        for src_ref, new_t, dst_ref in ((ck_ref, k_t, nk_ref), (cv_ref, v_t, nv_ref)):
            old = pltpu.roll(src_ref[b], _WIN - t, axis=1)
            new = new_t[:, c0:c0 + _LANES]
            if shift_new:
                new = pltpu.roll(new, shift_new, axis=1)
            dst_ref[b] = jnp.where(keep, old, new)

    u = _proj(h, win_ref, _OFF_C, _CONV) * _proj(h, win_ref, _OFF_U, _CONV)
    u3 = u.reshape(sb, t, _CONV)
    ubuf[:, _SUB - 2:_SUB, :] = st_ref[...]
    ubuf[:, _SUB:_SUB + t, :] = u3
    us2 = ubuf[:, _SUB - 2:_SUB - 2 + t, :].reshape(tr, _CONV)
    us1 = ubuf[:, _SUB - 1:_SUB - 1 + t, :].reshape(tr, _CONV)
    nc_ref[...] = u3[:, t - 2:t, :]
    y_ref[...] = _mix_and_project(x, h, attn.astype(_BF), us2, us1, u,
                                  win_ref, convw_ref, wao_ref, wco_ref, wout_ref, gpost_ref)


def _mlp_kernel(x_ref, gpre_ref, gpost_ref, wup_ref, wdown_ref, y_ref):
    _run_interleaved(_mlp_stages(x_ref, y_ref, gpre_ref, gpost_ref, wup_ref, wdown_ref))


def _layer_spec(shape, layer, **kwargs):
    return pl.BlockSpec((None,) + shape, lambda *_: (layer,) + (0,) * len(shape), **kwargs)


def _weight_spec(w):
    return pl.BlockSpec(w.shape, lambda *_: (0, 0), pipeline_mode=pl.Buffered(1))


_SMEM_SPEC = pl.BlockSpec(memory_space=pltpu.SMEM)
_ANY_SPEC = pl.BlockSpec(memory_space=pl.ANY)


def _stacked_out_spec(depth, layer, first, block, index_map):
    lead, l0 = (depth, 0) if first else (None, layer)
    return pl.BlockSpec((lead,) + block, lambda *idx: (l0,) + index_map(*idx))


def _alias_args(prev, n_inputs):
    if prev is None:
        return [], [], {}
    return list(prev), [_ANY_SPEC] * len(prev), {n_inputs + i: 1 + i for i in range(len(prev))}


def _prompt_layer(x, layer, prev, sinks, gpre, gpost, gmpre, gmpost, convw, weights_f32):
    n, seq, _ = x.shape
    depth = sinks.shape[0]
    tm = _PROMPT_TM
    first = prev is None
    assert seq % tm == 0 and tm % _WIN == 0 and seq >= _WIN and len(weights_f32) == _N_WEIGHTS
    tps = seq // tm
    n_tiles = n * tps
    n_pro = _CAST_STEPS

    def tile_of(t):
        return (t // tps, t % tps, 0)

    mixer_j = lambda i: jnp.clip(i - n_pro, 0, n_tiles - 1)
    mixer_tile = lambda i: tile_of(mixer_j(i))
    mlp_tile = lambda i: tile_of(jnp.maximum(i - n_pro - 1, 0))
    per_b = lambda i: (mixer_j(i) // tps, 0, 0)

    chunk_of = lambda i: jnp.minimum(i, n_pro - 1)
    cast_in_specs, cast_out_specs, cast_out_shapes, weight_scratch = [], [], [], []
    for w in weights_f32:
        rows, cols = w.shape[1:]
        assert rows % (n_pro * _BF_ROWS) == 0
        chunk = rows // n_pro
        cast_in_specs.append(pl.BlockSpec((None, chunk, cols), lambda i: (layer, chunk_of(i), 0)))
        cast_out_specs.append(pl.BlockSpec((chunk, cols), lambda i: (chunk_of(i), 0)))
        cast_out_shapes.append(jax.ShapeDtypeStruct((rows, cols), _BF))
        weight_scratch.append(pltpu.VMEM((rows, cols), _BF))

    inputs = [sinks, x, gpre, gpost, gmpre, gmpost, convw, *weights_f32]
    alias_in, alias_specs, aliases = _alias_args(prev, len(inputs))
    out = pl.pallas_call(
        functools.partial(_prompt_layer_kernel, tm=tm, tps=tps, n_pro=n_pro, layer=layer, first=first),
        grid=(n_pro + n_tiles + 1,),
        in_specs=[_SMEM_SPEC,
                  pl.BlockSpec((None, tm, _D), mixer_tile)]
                 + [_layer_spec((1, _D), layer)] * 4 + [_layer_spec((3, _CONV), layer)]
                 + cast_in_specs + alias_specs,
        out_specs=[pl.BlockSpec((None, tm, _D), mlp_tile),
                   _stacked_out_spec(depth, layer, first, (None, _WIN, _KVD), per_b),
                   _stacked_out_spec(depth, layer, first, (None, _WIN, _KVD), per_b),
                   _stacked_out_spec(depth, layer, first, (None, 2, _CONV), per_b)] + cast_out_specs,
        out_shape=[jax.ShapeDtypeStruct((n, seq, _D), _F32),
                   jax.ShapeDtypeStruct((depth, n, _WIN, _KVD), _F32),
                   jax.ShapeDtypeStruct((depth, n, _WIN, _KVD), _F32),
                   jax.ShapeDtypeStruct((depth, n, 2, _CONV), _F32)] + cast_out_shapes,
        scratch_shapes=weight_scratch + [
                        pltpu.VMEM((2, _WIN + tm, _LANES), _BF),
                        pltpu.VMEM((2, _WIN + tm, _LANES), _BF),
                        pltpu.VMEM((4, 2, tm, _LANES), _BF),
                        pltpu.VMEM((_SUB + tm, _CONV), _F32),
                        pltpu.VMEM((tm, _ATT), _BF),
                        pltpu.VMEM((tm, _D), _F32),
                        pltpu.VMEM((_WIN, _KVD), _F32),
                        pltpu.VMEM((_WIN, _KVD), _F32),
                        pltpu.VMEM((_SUB, _CONV), _F32)],
        input_output_aliases=aliases,
        compiler_params=pltpu.CompilerParams(dimension_semantics=("arbitrary",),
                                             vmem_limit_bytes=_VMEM_LIMIT),
        name="prompt_layer",
    )(*inputs, *alias_in)
    return out[0], tuple(out[1:4]), tuple(out[4:])


def _mixer_sample(x, layer, prev, ck, cv, st, sinks, gpre, gpost, convw, win, wao, wco, wout):
    n, t, _ = x.shape
    depth = sinks.shape[0]
    sb = _SAMPLE_SB
    first = prev is None
    assert n % sb == 0 and t == _SUB and ck.shape == (depth, n, _WIN, _KVD)
    tr = sb * t
    rows = lambda i: (i, 0)
    seqs = lambda i: (i, 0, 0)
    cache_spec = pl.BlockSpec((None, sb, _WIN, _KVD), lambda i: (layer, i, 0, 0))
    state_spec = pl.BlockSpec((None, sb, 2, _CONV), lambda i: (layer, i, 0, 0))
    inputs = [sinks, x.reshape(n * t, _D), gpre, gpost, convw, ck, cv, st, win, wao, wco, wout]
    alias_in, alias_specs, aliases = _alias_args(prev, len(inputs))
    out = pl.pallas_call(
        functools.partial(_mixer_sample_kernel, sb=sb, t=t, layer=layer, first=first),
        grid=(n // sb,),
        in_specs=[_SMEM_SPEC,
                  pl.BlockSpec((tr, _D), rows),
                  _layer_spec((1, _D), layer), _layer_spec((1, _D), layer), _layer_spec((3, _CONV), layer),
                  cache_spec, cache_spec, state_spec]
                 + [_weight_spec(w) for w in (win, wao, wco, wout)] + alias_specs,
        out_specs=[pl.BlockSpec((tr, _D), rows),
                   _stacked_out_spec(depth, layer, first, (sb, _WIN, _KVD), seqs),
                   _stacked_out_spec(depth, layer, first, (sb, _WIN, _KVD), seqs),
                   _stacked_out_spec(depth, layer, first, (sb, 2, _CONV), seqs)],
        out_shape=[jax.ShapeDtypeStruct((n * t, _D), _F32),
                   jax.ShapeDtypeStruct((depth, n, _WIN, _KVD), _F32),
                   jax.ShapeDtypeStruct((depth, n, _WIN, _KVD), _F32),
                   jax.ShapeDtypeStruct((depth, n, 2, _CONV), _F32)],
        scratch_shapes=[pltpu.VMEM((sb, 2 * _SUB, _CONV), _F32)],
        input_output_aliases=aliases,
        compiler_params=pltpu.CompilerParams(dimension_semantics=("arbitrary",),
                                             vmem_limit_bytes=_VMEM_LIMIT),
        name="mixer_sample",
    )(*inputs, *alias_in)
    return out[0].reshape(n, t, _D), tuple(out[1:])


def _mlp(x, layer, gpre, gpost, wup, wdown):
    shape = x.shape
    x2 = x.reshape(-1, _D)
    nt = x2.shape[0]
    tm = _MLP_TM
    assert nt % tm == 0
    rows = lambda i: (i, 0)
    y = pl.pallas_call(
        _mlp_kernel,
        grid=(nt // tm,),
        in_specs=[pl.BlockSpec((tm, _D), rows), _layer_spec((1, _D), layer), _layer_spec((1, _D), layer),
                  _weight_spec(wup), _weight_spec(wdown)],
        out_specs=pl.BlockSpec((tm, _D), rows),
        out_shape=jax.ShapeDtypeStruct((nt, _D), _F32),
        compiler_params=pltpu.CompilerParams(dimension_semantics=("parallel",),
                                             vmem_limit_bytes=_VMEM_LIMIT),
        name="mlp",
    )(x2, gpre, gpost, wup, wdown)
    return y.reshape(shape)


def kernel(x_prompt, x_sample, cache_k, cache_v, state_conv, g_mix_pre, g_mix_post, g_mlp_pre, g_mlp_post,
           w_in, attn_sinks, conv_w, w_attn_o, w_conv_o, w_out, w_up, w_down):
    depth = w_in.shape[0]
    n_p, n_dec = x_prompt.shape[0], x_sample.shape[0]
    feature_major = lambda c: jnp.transpose(c, (0, 1, 3, 4, 2)).reshape(depth, n_dec, _KVD, _WIN)
    ck, cv = feature_major(cache_k), feature_major(cache_v)
    weights_f32 = (w_in, w_attn_o, w_conv_o, w_out, w_up, w_down)
    gains = [g.reshape(depth, 1, _D) for g in (g_mix_pre, g_mix_post, g_mlp_pre, g_mlp_post)]

    yp, ys = x_prompt, x_sample
    caches_p = caches_s = None
    for l in range(depth):
        yp, caches_p, weights = _prompt_layer(yp, l, caches_p, attn_sinks, *gains, conv_w, weights_f32)
        ys, caches_s = _mixer_sample(ys, l, caches_s, ck, cv, state_conv, attn_sinks, gains[0], gains[1], conv_w,
                                     *weights[:4])
        ys = _mlp(ys, l, gains[2], gains[3], *weights[4:])

    kv5 = lambda a: a.reshape(a.shape[:3] + (2, _HD))
    position_major = lambda a: jnp.transpose(a.reshape(depth, n_dec, 2, _HD, _WIN), (0, 1, 4, 2, 3))
    return (yp, ys, kv5(caches_p[0]), kv5(caches_p[1]), caches_p[2],
            position_major(caches_s[0]), position_major(caches_s[1]), caches_s[2])
```

```python
import functools

import jax
import jax.numpy as jnp
from jax import lax
from jax.experimental import pallas as pl
from jax.experimental.pallas import tpu as pltpu

_D = 1024
_HD = 64
_NH = 8
_ATT = _NH * _HD
_KVD = 128
_CONV = 512
_DFF = 4096
_WIN = 128
_EPS = 1e-6
_LANES = 128
_SUB = 8
_BF_ROWS = 16

_OFF_Q = 0
_OFF_KV = _ATT
_OFF_B = _OFF_KV + 2 * _KVD
_OFF_C = _OFF_B + _CONV
_OFF_U = _OFF_C + _CONV
_OFF_GA = _OFF_U + _CONV
_OFF_GC = _OFF_GA + _D
_IN_DIM = _OFF_GC + _D

_PROMPT_TM = 512
_SAMPLE_SB = 32
_MLP_TM = 512
_VMEM_LIMIT = 62 * 1024 * 1024

_BF = jnp.bfloat16
_F32 = jnp.float32


def _rms(x, g):
    return x * lax.rsqrt(jnp.mean(x * x, axis=-1, keepdims=True) + _EPS) * g


def _proj(h_bf, win_ref, off, width):
    return jnp.dot(h_bf, win_ref[:, off:off + width], preferred_element_type=_F32)


def _dup_halves(val, lo):
    rolled = pltpu.roll(val, _HD, axis=1)
    return jnp.where(lo, val, rolled), jnp.where(lo, rolled, val)


def _band_mask(qi, kj):
    return (kj > qi) & (kj <= qi + _WIN)


def _dot_nt(a, b):
    return lax.dot_general(a, b, (((1,), (1,)), ((), ())), preferred_element_type=_F32)


def _own_layer(ref, first):
    if not first:
        return ref
    if ref.shape[0] > 1:
        ref[1:] = jnp.zeros((ref.shape[0] - 1,) + ref.shape[1:], ref.dtype)
    return ref.at[0]


def _softmax_pv(s, vd, bias, sinks, nq):
    es, rden = [], []
    for j in range(4):
        sj = s[j * nq:(j + 1) * nq] + bias
        m = jnp.maximum(jnp.max(sj, axis=-1, keepdims=True), sinks[j])
        e = jnp.exp(sj - m)
        den = jnp.sum(e, axis=-1, keepdims=True) + jnp.exp(sinks[j] - m)
        es.append(e)
        rden.append(1.0 / den)
    e_all = jnp.concatenate(es, axis=0).astype(_BF)
    o = jnp.dot(e_all, vd, preferred_element_type=_F32)
    return [o[j * nq:(j + 1) * nq] * rden[j] for j in range(4)]


def _mix_and_project(x, h, attn_bf, u_shift2, u_shift1, u, win_ref, convw_ref, wao_ref, wco_ref, wout_ref,
                     gpost_ref):
    attn_b = jnp.dot(attn_bf, wao_ref[...], preferred_element_type=_F32)
    mixed = jax.nn.sigmoid(_proj(h, win_ref, _OFF_GA, _D)) * attn_b
    cw = convw_ref[...]
    z = cw[0:1] * u_shift2
    z = z + cw[1:2] * u_shift1
    z = z + cw[2:3] * u
    bz = (_proj(h, win_ref, _OFF_B, _CONV) * z).astype(_BF)
    conv_b = jnp.dot(bz, wco_ref[...], preferred_element_type=_F32)
    mixed = mixed + jax.nn.sigmoid(_proj(h, win_ref, _OFF_GC, _D)) * conv_b
    mo = jnp.dot(mixed.astype(_BF), wout_ref[...], preferred_element_type=_F32)
    return x + _rms(mo, gpost_ref[...])


def _run_interleaved(*stages):
    stages = list(stages)
    while stages:
        for g in list(stages):
            try:
                next(g)
            except StopIteration:
                stages.remove(g)


def _mlp_stages(x_ref, y_ref, gpre_ref, gpost_ref, wup_ref, wdown_ref):
    x = x_ref[...]
    y_ref[...] = x
    hm = _rms(x, gpre_ref[...]).astype(_BF)
    yield
    acc = None
    for j in range(_DFF // _D):
        hj = jnp.dot(hm, wup_ref[:, j * _D:(j + 1) * _D], preferred_element_type=_F32)
        yield
        hj = jnp.square(jnp.maximum(hj, 0.0)).astype(_BF)
        part = jnp.dot(hj, wdown_ref[j * _D:(j + 1) * _D, :], preferred_element_type=_F32)
        acc = part if acc is None else acc + part
        yield
    y_ref[...] = y_ref[...] + _rms(acc, gpost_ref[...])


def _prompt_layer_kernel(sink_ref, x_ref, gpre_ref, gpost_ref, gmpre_ref, gmpost_ref, convw_ref,
                         win_ref, wao_ref, wco_ref, wout_ref, wup_ref, wdown_ref,
                         *rest, tm, tps, layer, first, n_cast):
    cast_src, rest = rest[:n_cast], rest[n_cast + (0 if first else 3):]
    y_ref, nk_ref, nv_ref, nc_ref = rest[:4]
    cast_dst, rest = rest[4:4 + n_cast], rest[4 + n_cast:]
    kdup, vdup, q4s, ubuf, attn_scr, x1_scr, klast, vlast, ulast = rest
    j = pl.program_id(0)

    def cast_next_layer():
        for src, dst in zip(cast_src, cast_dst):
            dst[...] = src[...].astype(_BF)

    n_tiles = pl.num_programs(0) - 1
    s = j % tps

    def start_sequence():
        @pl.when(s == 0)
        def _():
            kdup[:, 0:_WIN, :] = jnp.zeros((2, _WIN, _LANES), _BF)
            vdup[:, 0:_WIN, :] = jnp.zeros((2, _WIN, _LANES), _BF)
            ubuf[0:_SUB, :] = jnp.zeros((_SUB, _CONV), _F32)

    def end_sequence():
        @pl.when(s == tps - 1)
        def _():
            _own_layer(nk_ref, first)[...] = klast[...]
            _own_layer(nv_ref, first)[...] = vlast[...]
            _own_layer(nc_ref, first)[...] = ulast[_SUB - 2:_SUB, :]

    def mlp(src):
        return _mlp_stages(src, y_ref, gmpre_ref, gmpost_ref, wup_ref, wdown_ref)

    def mixer():
        return _mixer_prompt_stages(sink_ref, x_ref, gpre_ref, gpost_ref, convw_ref, win_ref, wao_ref, wco_ref,
                                    wout_ref, kdup, vdup, q4s, ubuf, attn_scr, x1_scr, klast, vlast, ulast, s,
                                    tm=tm, layer=layer)

    @pl.when(j == 0)
    def _():
        start_sequence()
        cast_next_layer()
        _run_interleaved(mixer())
        end_sequence()

    @pl.when((j > 0) & (j < n_tiles))
    def _():
        start_sequence()
        mlp_stages, mixer_stages = mlp(x1_scr), mixer()
        for _ in range(2 * (_DFF // _D) - 1):
            next(mlp_stages)
        cast_next_layer()
        next(mixer_stages)
        next(mlp_stages)
        next(mixer_stages)
        _run_interleaved(mlp_stages, mixer_stages)
        end_sequence()

    @pl.when(j == n_tiles)
    def _():
        cast_next_layer()
        _run_interleaved(mlp(x1_scr))


def _mixer_prompt_stages(sink_ref, x_ref, gpre_ref, gpost_ref, convw_ref, win_ref, wao_ref, wco_ref, wout_ref,
                         kdup, vdup, q4s, ubuf, attn_scr, x1_scr, klast, vlast, ulast, s, *, tm, layer):
    x = x_ref[...]
    h = _rms(x, gpre_ref[...]).astype(_BF)
    lo = lax.broadcasted_iota(jnp.int32, (tm, _LANES), 1) < _HD

    hr = tm // 2
    kv = jnp.concatenate([_proj(h[:hr], win_ref, _OFF_KV, 2 * _KVD), _proj(h[hr:], win_ref, _OFF_KV, 2 * _KVD)],
                         axis=0)
    k = kv[:, :_KVD]
    v = kv[:, _KVD:]
    for val, dst in ((k, kdup), (v, vdup)):
        d0, d1 = _dup_halves(val, lo)
        dst[0, _WIN:_WIN + tm, :] = d0.astype(_BF)
        dst[1, _WIN:_WIN + tm, :] = d1.astype(_BF)

    q = _proj(h, win_ref, _OFF_Q, _ATT) * (_HD ** -0.5)
    for c in range(4):
        qc = q[:, c * _LANES:(c + 1) * _LANES]
        q4s[c, 0] = jnp.where(lo, qc, 0.0).astype(_BF)
        q4s[c, 1] = jnp.where(lo, 0.0, qc).astype(_BF)

    qi = lax.broadcasted_iota(jnp.int32, (_WIN, 2 * _WIN), 0)
    kj = lax.broadcasted_iota(jnp.int32, (_WIN, 2 * _WIN), 1)
    band = _band_mask(qi, kj)
    neg = jnp.float32(-jnp.inf)
    bias = jnp.where(band, 0.0, neg)
    bias_first = jnp.where(band & ((kj >= _WIN) | (s > 0)), 0.0, neg)
    lo_q = lax.broadcasted_iota(jnp.int32, (_WIN, _LANES), 1) < _HD

    def scores_of(qb):
        r0 = qb * _WIN
        out = []
        for g in range(2):
            q4 = jnp.concatenate([q4s[2 * g, 0, r0:r0 + _WIN, :], q4s[2 * g, 1, r0:r0 + _WIN, :],
                                  q4s[2 * g + 1, 0, r0:r0 + _WIN, :], q4s[2 * g + 1, 1, r0:r0 + _WIN, :]], axis=0)
            out.append(_dot_nt(q4, kdup[g, r0:r0 + 2 * _WIN, :]))
        return out

    def finish_block(qb, scores):
        r0 = qb * _WIN
        b = bias_first if qb == 0 else bias
        for g in range(2):
            sinks = [sink_ref[layer, 4 * g + j] for j in range(4)]
            o = _softmax_pv(scores[g], vdup[g, r0:r0 + 2 * _WIN, :], b, sinks, _WIN)
            attn_scr[r0:r0 + _WIN, (2 * g) * _LANES:(2 * g + 1) * _LANES] = jnp.where(lo_q, o[0], o[1]).astype(_BF)
            attn_scr[r0:r0 + _WIN, (2 * g + 1) * _LANES:(2 * g + 2) * _LANES] = (
                jnp.where(lo_q, o[2], o[3]).astype(_BF))

    def conv_input():
        u = _proj(h, win_ref, _OFF_C, _CONV) * _proj(h, win_ref, _OFF_U, _CONV)
        ubuf[_SUB:_SUB + tm, :] = u
        return u

    fillers = [conv_input,
               lambda: jax.nn.sigmoid(_proj(h, win_ref, _OFF_GA, _D)),
               lambda: jax.nn.sigmoid(_proj(h, win_ref, _OFF_GC, _D)),
               lambda: _proj(h, win_ref, _OFF_B, _CONV)]
    filled = []
    n_qb = tm // _WIN
    scores = scores_of(0)
    for qb in range(n_qb):
        if qb < len(fillers):
            filled.append(fillers[qb]())
        finish_block(qb, scores)
        if qb + 1 < n_qb:
            scores = scores_of(qb + 1)
    filled += [f() for f in fillers[n_qb:]]
    u, gate_attn, gate_conv, b_gate = filled

    kdup[:, 0:_WIN, :] = kdup[:, tm:tm + _WIN, :]
    vdup[:, 0:_WIN, :] = vdup[:, tm:tm + _WIN, :]

    mixed = gate_attn * jnp.dot(attn_scr[...], wao_ref[...], preferred_element_type=_F32)
    cw = convw_ref[...]
    z = cw[0:1] * ubuf[_SUB - 2:_SUB - 2 + tm, :]
    z = z + cw[1:2] * ubuf[_SUB - 1:_SUB - 1 + tm, :]
    z = z + cw[2:3] * u
    conv_b = jnp.dot((b_gate * z).astype(_BF), wco_ref[...], preferred_element_type=_F32)
    mixed = (mixed + gate_conv * conv_b).astype(_BF)
    yield
    mo = jnp.dot(mixed, wout_ref[...], preferred_element_type=_F32)
    yield
    x1_scr[...] = x + _rms(mo, gpost_ref[...])
    ubuf[0:_SUB, :] = ubuf[tm:tm + _SUB, :]
    klast[...] = k[tm - _WIN:tm, :]
    vlast[...] = v[tm - _WIN:tm, :]
    ulast[...] = u[tm - _SUB:tm, :]


def _mixer_sample_kernel(sink_ref, x_ref, gpre_ref, gpost_ref, convw_ref, ck_ref, cv_ref, st_ref,
                         win_ref, wao_ref, wco_ref, wout_ref, *rest, sb, t, layer, first):
    y_ref, nk_ref, nv_ref, nc_ref, ubuf = rest[0 if first else 3:]
    nk_ref, nv_ref, nc_ref = (_own_layer(r, first) for r in (nk_ref, nv_ref, nc_ref))
    tr = sb * t
    rows = _NH * t
    x = x_ref[...]
    h = _rms(x, gpre_ref[...]).astype(_BF)
    lo = lax.broadcasted_iota(jnp.int32, (tr, _LANES), 1) < _HD

    kv = _proj(h, win_ref, _OFF_KV, 2 * _KVD)
    k = kv[:, :_KVD]
    v = kv[:, _KVD:]
    q = _proj(h, win_ref, _OFF_Q, _ATT) * (_HD ** -0.5)
    pieces = []
    for hd in range(_NH):
        qc = q[:, (hd // 2) * _LANES:(hd // 2 + 1) * _LANES]
        if hd % 2 != hd // (_NH // 2):
            qc = pltpu.roll(qc, _HD, axis=1)
        piece = jnp.where(lo, qc, 0.0) if hd < _NH // 2 else jnp.where(lo, 0.0, qc)
        pieces.append(piece.reshape(sb, t, _LANES))
    q3 = jnp.concatenate(pieces, axis=1).astype(_BF)

    pad = jnp.zeros((sb, _BF_ROWS - t, _LANES), _F32)
    k_new = jnp.concatenate([k.reshape(sb, t, _KVD), pad], axis=1).astype(_BF)
    v_new = jnp.concatenate([v.reshape(sb, t, _KVD), pad], axis=1).astype(_BF)

    s_old = jnp.einsum('bqf,bfp->bqp', q3, ck_ref[...].astype(_BF), preferred_element_type=_F32)
    s_new = jnp.einsum('bqf,btf->bqt', q3, k_new, preferred_element_type=_F32)
    neg = jnp.float32(-jnp.inf)
    qi_old = lax.broadcasted_iota(jnp.int32, (rows, _WIN), 0) & (t - 1)
    kj_old = lax.broadcasted_iota(jnp.int32, (rows, _WIN), 1)
    qi_new = lax.broadcasted_iota(jnp.int32, (rows, _BF_ROWS), 0) & (t - 1)
    kj_new = lax.broadcasted_iota(jnp.int32, (rows, _BF_ROWS), 1) + _WIN
    s_old = s_old + jnp.where(_band_mask(qi_old, kj_old), 0.0, neg)[None]
    s_new = s_new + jnp.where(_band_mask(qi_new, kj_new), 0.0, neg)[None]
    sink_col = jnp.concatenate([jnp.full((t, 1), sink_ref[layer, hd], _F32) for hd in range(_NH)], axis=0)[None]
    m = jnp.maximum(jnp.maximum(jnp.max(s_old, axis=-1, keepdims=True), jnp.max(s_new, axis=-1, keepdims=True)),
                    sink_col)
    e_old = jnp.exp(s_old - m)
    e_new = jnp.exp(s_new - m)
    rden = 1.0 / (jnp.sum(e_old, axis=-1, keepdims=True) + jnp.sum(e_new, axis=-1, keepdims=True)
                  + jnp.exp(sink_col - m))
    o = (jnp.einsum('bqp,bfp->bqf', e_old.astype(_BF), cv_ref[...].astype(_BF), preferred_element_type=_F32)
         + jnp.einsum('bqt,btf->bqf', e_new.astype(_BF), v_new, preferred_element_type=_F32)) * rden
    chunks = []
    for c in range(_NH // 2):
        halves = []
        for hd in (2 * c, 2 * c + 1):
            oh = o[:, hd * t:(hd + 1) * t, :].reshape(tr, _LANES)
            if hd % 2 != hd // (_NH // 2):
                oh = pltpu.roll(oh, _HD, axis=1)
            halves.append(oh)
        chunks.append(jnp.where(lo, halves[0], halves[1]))
    attn = jnp.concatenate(chunks, axis=1)

    k_t = k.T
    v_t = v.T
    keep = lax.broadcasted_iota(jnp.int32, (_KVD, _WIN), 1) < _WIN - t
    per_chunk = _LANES // t
    for b in range(sb):
        c0 = (b // per_chunk) * _LANES
        shift_new = (_WIN - t - (b % per_chunk) * t) % _LANES
        for src_ref, new_t, dst_ref in ((ck_ref, k_t, nk_ref), (cv_ref, v_t, nv_ref)):
            old = pltpu.roll(src_ref[b], _WIN - t, axis=1)
            new = new_t[:, c0:c0 + _LANES]
            if shift_new:
                new = pltpu.roll(new, shift_new, axis=1)
            dst_ref[b] = jnp.where(keep, old, new)

    u = _proj(h, win_ref, _OFF_C, _CONV) * _proj(h, win_ref, _OFF_U, _CONV)
    u3 = u.reshape(sb, t, _CONV)
    ubuf[:, _SUB - 2:_SUB, :] = st_ref[...]
    ubuf[:, _SUB:_SUB + t, :] = u3
    us2 = ubuf[:, _SUB - 2:_SUB - 2 + t, :].reshape(tr, _CONV)
    us1 = ubuf[:, _SUB - 1:_SUB - 1 + t, :].reshape(tr, _CONV)
    nc_ref[...] = u3[:, t - 2:t, :]
    y_ref[...] = _mix_and_project(x, h, attn.astype(_BF), us2, us1, u,
                                  win_ref, convw_ref, wao_ref, wco_ref, wout_ref, gpost_ref)


def _mlp_kernel(x_ref, gpre_ref, gpost_ref, wup_ref, wdown_ref, y_ref):
    _run_interleaved(_mlp_stages(x_ref, y_ref, gpre_ref, gpost_ref, wup_ref, wdown_ref))


def _layer_spec(shape, layer, **kwargs):
    return pl.BlockSpec((None,) + shape, lambda *_: (layer,) + (0,) * len(shape), **kwargs)


def _weight_spec(w):
    return pl.BlockSpec(w.shape, lambda *_: (0, 0), pipeline_mode=pl.Buffered(1))


_SMEM_SPEC = pl.BlockSpec(memory_space=pltpu.SMEM)
_ANY_SPEC = pl.BlockSpec(memory_space=pl.ANY)


def _stacked_out_spec(depth, layer, first, block, index_map):
    lead, l0 = (depth, 0) if first else (None, layer)
    return pl.BlockSpec((lead,) + block, lambda *idx: (l0,) + index_map(*idx))


def _alias_args(prev, n_inputs):
    if prev is None:
        return [], [], {}
    return list(prev), [_ANY_SPEC] * len(prev), {n_inputs + i: 1 + i for i in range(len(prev))}


def _prompt_layer(x, layer, prev, sinks, gpre, gpost, gmpre, gmpost, convw, weights, next_f32):
    n, seq, _ = x.shape
    depth = sinks.shape[0]
    tm = _PROMPT_TM
    first = prev is None
    assert seq % tm == 0 and tm % _WIN == 0 and seq >= _WIN
    tps = seq // tm
    n_tiles = n * tps

    def tile_of(t):
        return (t // tps, t % tps, 0)

    clamp = lambda j: jnp.minimum(j, n_tiles - 1)
    mixer_tile = lambda j: tile_of(clamp(j))
    mlp_tile = lambda j: tile_of(jnp.maximum(j - 1, 0))
    per_b = lambda j: (clamp(j) // tps, 0, 0)

    cast_in, cast_in_specs, cast_out_specs, cast_out_shapes = [], [], [], []
    for w in (next_f32 or ()):
        rows, cols = w.shape[1:]
        assert rows % (n_tiles * _BF_ROWS) == 0
        chunk = rows // n_tiles
        cast_in.append(w)
        cast_in_specs.append(pl.BlockSpec((None, chunk, cols), lambda j: (layer + 1, clamp(j), 0)))
        cast_out_specs.append(pl.BlockSpec((chunk, cols), lambda j: (clamp(j), 0)))
        cast_out_shapes.append(jax.ShapeDtypeStruct((rows, cols), _BF))

    inputs = [sinks, x, gpre, gpost, gmpre, gmpost, convw, *weights, *cast_in]
    alias_in, alias_specs, aliases = _alias_args(prev, len(inputs))
    out = pl.pallas_call(
        functools.partial(_prompt_layer_kernel, tm=tm, tps=tps, layer=layer, first=first, n_cast=len(cast_in)),
        grid=(n_tiles + 1,),
        in_specs=[_SMEM_SPEC,
                  pl.BlockSpec((None, tm, _D), mixer_tile)]
                 + [_layer_spec((1, _D), layer)] * 4 + [_layer_spec((3, _CONV), layer)]
                 + [_weight_spec(w) for w in weights] + cast_in_specs + alias_specs,
        out_specs=[pl.BlockSpec((None, tm, _D), mlp_tile),
                   _stacked_out_spec(depth, layer, first, (None, _WIN, _KVD), per_b),
                   _stacked_out_spec(depth, layer, first, (None, _WIN, _KVD), per_b),
                   _stacked_out_spec(depth, layer, first, (None, 2, _CONV), per_b)] + cast_out_specs,
        out_shape=[jax.ShapeDtypeStruct((n, seq, _D), _F32),
                   jax.ShapeDtypeStruct((depth, n, _WIN, _KVD), _F32),
                   jax.ShapeDtypeStruct((depth, n, _WIN, _KVD), _F32),
                   jax.ShapeDtypeStruct((depth, n, 2, _CONV), _F32)] + cast_out_shapes,
        scratch_shapes=[pltpu.VMEM((2, _WIN + tm, _LANES), _BF),
                        pltpu.VMEM((2, _WIN + tm, _LANES), _BF),
                        pltpu.VMEM((4, 2, tm, _LANES), _BF),
                        pltpu.VMEM((_SUB + tm, _CONV), _F32),
                        pltpu.VMEM((tm, _ATT), _BF),
                        pltpu.VMEM((tm, _D), _F32),
                        pltpu.VMEM((_WIN, _KVD), _F32),
                        pltpu.VMEM((_WIN, _KVD), _F32),
                        pltpu.VMEM((_SUB, _CONV), _F32)],
        input_output_aliases=aliases,
        compiler_params=pltpu.CompilerParams(dimension_semantics=("arbitrary",),
                                             vmem_limit_bytes=_VMEM_LIMIT),
        name="prompt_layer",
    )(*inputs, *alias_in)
    return out[0], tuple(out[1:4]), tuple(out[4:])


def _mixer_sample(x, layer, prev, ck, cv, st, sinks, gpre, gpost, convw, win, wao, wco, wout):
    n, t, _ = x.shape
    depth = sinks.shape[0]
    sb = _SAMPLE_SB
    first = prev is None
    assert n % sb == 0 and t == _SUB and ck.shape == (depth, n, _KVD, _WIN)
    tr = sb * t
    rows = lambda i: (i, 0)
    seqs = lambda i: (i, 0, 0)
    cache_spec = pl.BlockSpec((None, sb, _KVD, _WIN), lambda i: (layer, i, 0, 0))
    state_spec = pl.BlockSpec((None, sb, 2, _CONV), lambda i: (layer, i, 0, 0))
    inputs = [sinks, x.reshape(n * t, _D), gpre, gpost, convw, ck, cv, st, win, wao, wco, wout]
    alias_in, alias_specs, aliases = _alias_args(prev, len(inputs))
    out = pl.pallas_call(
        functools.partial(_mixer_sample_kernel, sb=sb, t=t, layer=layer, first=first),
        grid=(n // sb,),
        in_specs=[_SMEM_SPEC,
                  pl.BlockSpec((tr, _D), rows),
                  _layer_spec((1, _D), layer), _layer_spec((1, _D), layer), _layer_spec((3, _CONV), layer),
                  cache_spec, cache_spec, state_spec]
                 + [_weight_spec(w) for w in (win, wao, wco, wout)] + alias_specs,
        out_specs=[pl.BlockSpec((tr, _D), rows),
                   _stacked_out_spec(depth, layer, first, (sb, _KVD, _WIN), seqs),
                   _stacked_out_spec(depth, layer, first, (sb, _KVD, _WIN), seqs),
                   _stacked_out_spec(depth, layer, first, (sb, 2, _CONV), seqs)],
        out_shape=[jax.ShapeDtypeStruct((n * t, _D), _F32),
                   jax.ShapeDtypeStruct((depth, n, _KVD, _WIN), _F32),
                   jax.ShapeDtypeStruct((depth, n, _KVD, _WIN), _F32),
                   jax.ShapeDtypeStruct((depth, n, 2, _CONV), _F32)],
        scratch_shapes=[pltpu.VMEM((sb, 2 * _SUB, _CONV), _F32)],
        input_output_aliases=aliases,
        compiler_params=pltpu.CompilerParams(dimension_semantics=("arbitrary",),
                                             vmem_limit_bytes=_VMEM_LIMIT),
        name="mixer_sample",
    )(*inputs, *alias_in)
    return out[0].reshape(n, t, _D), tuple(out[1:])


def _mlp(x, layer, gpre, gpost, wup, wdown):
    shape = x.shape
    x2 = x.reshape(-1, _D)
    nt = x2.shape[0]
    tm = _MLP_TM
    assert nt % tm == 0
    rows = lambda i: (i, 0)
    y = pl.pallas_call(
        _mlp_kernel,
        grid=(nt // tm,),
        in_specs=[pl.BlockSpec((tm, _D), rows), _layer_spec((1, _D), layer), _layer_spec((1, _D), layer),
                  _weight_spec(wup), _weight_spec(wdown)],
        out_specs=pl.BlockSpec((tm, _D), rows),
        out_shape=jax.ShapeDtypeStruct((nt, _D), _F32),
        compiler_params=pltpu.CompilerParams(dimension_semantics=("parallel",),
                                             vmem_limit_bytes=_VMEM_LIMIT),
        name="mlp",
    )(x2, gpre, gpost, wup, wdown)
    return y.reshape(shape)


def kernel(x_prompt, x_sample, cache_k, cache_v, state_conv, g_mix_pre, g_mix_post, g_mlp_pre, g_mlp_post,
           w_in, attn_sinks, conv_w, w_attn_o, w_conv_o, w_out, w_up, w_down):
    depth = w_in.shape[0]
    n_dec = x_sample.shape[0]
    feature_major = lambda c: jnp.transpose(c, (0, 1, 3, 4, 2)).reshape(depth, n_dec, _KVD, _WIN)
    ck, cv = feature_major(cache_k), feature_major(cache_v)
    weights_f32 = (w_in, w_attn_o, w_conv_o, w_out, w_up, w_down)
    gains = [g.reshape(depth, 1, _D) for g in (g_mix_pre, g_mix_post, g_mlp_pre, g_mlp_post)]

    weights = tuple(w[0].astype(_BF) for w in weights_f32)
    yp, ys = x_prompt, x_sample
    caches_p = caches_s = None
    for l in range(depth):
        yp, caches_p, next_weights = _prompt_layer(yp, l, caches_p, attn_sinks, *gains, conv_w, weights,
                                                   weights_f32 if l + 1 < depth else None)
        ys, caches_s = _mixer_sample(ys, l, caches_s, ck, cv, state_conv, attn_sinks, gains[0], gains[1], conv_w,
                                     *weights[:4])
        ys = _mlp(ys, l, gains[2], gains[3], *weights[4:])
        weights = next_weights

    kv5 = lambda a: a.reshape(a.shape[:3] + (2, _HD))
    position_major = lambda a: jnp.transpose(a.reshape(depth, n_dec, 2, _HD, _WIN), (0, 1, 4, 2, 3))
    return (yp, ys, kv5(caches_p[0]), kv5(caches_p[1]), caches_p[2],
            position_major(caches_s[0]), position_major(caches_s[1]), caches_s[2])
```

```python
import functools

import jax
import jax.numpy as jnp
from jax import lax
from jax.experimental import pallas as pl
from jax.experimental.pallas import tpu as pltpu

_D = 1024
_HD = 64
_NH = 8
_ATT = _NH * _HD
_KVD = 128
_CONV = 512
_DFF = 4096
_WIN = 128
_EPS = 1e-6
_LANES = 128
_SUB = 8
_BF_ROWS = 16

_OFF_Q = 0
_OFF_KV = _ATT
_OFF_B = _OFF_KV + 2 * _KVD
_OFF_C = _OFF_B + _CONV
_OFF_U = _OFF_C + _CONV
_OFF_GA = _OFF_U + _CONV
_OFF_GC = _OFF_GA + _D
_IN_DIM = _OFF_GC + _D

_PROMPT_TM = 512
_SAMPLE_SB = 32
_MLP_TM = 512
_VMEM_LIMIT = 62 * 1024 * 1024

_BF = jnp.bfloat16
_F32 = jnp.float32


def _rms(x, g):
    return x * lax.rsqrt(jnp.mean(x * x, axis=-1, keepdims=True) + _EPS) * g


def _proj(h_bf, win_ref, off, width):
    return jnp.dot(h_bf, win_ref[:, off:off + width], preferred_element_type=_F32)


def _dup_halves(val, lo):
    rolled = pltpu.roll(val, _HD, axis=1)
    return jnp.where(lo, val, rolled), jnp.where(lo, rolled, val)


def _band_mask(qi, kj):
    return (kj > qi) & (kj <= qi + _WIN)


def _dot_nt(a, b):
    return lax.dot_general(a, b, (((1,), (1,)), ((), ())), preferred_element_type=_F32)


def _own_layer(ref, first):
    if not first:
        return ref
    if ref.shape[0] > 1:
        ref[1:] = jnp.zeros((ref.shape[0] - 1,) + ref.shape[1:], ref.dtype)
    return ref.at[0]


def _softmax_pv(s, vd, bias, sinks, nq):
    es, rden = [], []
    for j in range(4):
        sj = s[j * nq:(j + 1) * nq] + bias
        m = jnp.maximum(jnp.max(sj, axis=-1, keepdims=True), sinks[j])
        e = jnp.exp(sj - m)
        den = jnp.sum(e, axis=-1, keepdims=True) + jnp.exp(sinks[j] - m)
        es.append(e)
        rden.append(1.0 / den)
    e_all = jnp.concatenate(es, axis=0).astype(_BF)
    o = jnp.dot(e_all, vd, preferred_element_type=_F32)
    return [o[j * nq:(j + 1) * nq] * rden[j] for j in range(4)]


def _mix_and_project(x, h, attn_bf, u_shift2, u_shift1, u, win_ref, convw_ref, wao_ref, wco_ref, wout_ref,
                     gpost_ref):
    attn_b = jnp.dot(attn_bf, wao_ref[...], preferred_element_type=_F32)
    mixed = jax.nn.sigmoid(_proj(h, win_ref, _OFF_GA, _D)) * attn_b
    cw = convw_ref[...]
    z = cw[0:1] * u_shift2
    z = z + cw[1:2] * u_shift1
    z = z + cw[2:3] * u
    bz = (_proj(h, win_ref, _OFF_B, _CONV) * z).astype(_BF)
    conv_b = jnp.dot(bz, wco_ref[...], preferred_element_type=_F32)
    mixed = mixed + jax.nn.sigmoid(_proj(h, win_ref, _OFF_GC, _D)) * conv_b
    mo = jnp.dot(mixed.astype(_BF), wout_ref[...], preferred_element_type=_F32)
    return x + _rms(mo, gpost_ref[...])


def _run_interleaved(*stages):
    stages = list(stages)
    while stages:
        for g in list(stages):
            try:
                next(g)
            except StopIteration:
                stages.remove(g)


def _mlp_stages(x_ref, y_ref, gpre_ref, gpost_ref, wup_ref, wdown_ref):
    x = x_ref[...]
    y_ref[...] = x
    hm = _rms(x, gpre_ref[...]).astype(_BF)
    yield
    acc = None
    for j in range(_DFF // _D):
        hj = jnp.dot(hm, wup_ref[:, j * _D:(j + 1) * _D], preferred_element_type=_F32)
        yield
        hj = jnp.square(jnp.maximum(hj, 0.0)).astype(_BF)
        part = jnp.dot(hj, wdown_ref[j * _D:(j + 1) * _D, :], preferred_element_type=_F32)
        acc = part if acc is None else acc + part
        yield
    y_ref[...] = y_ref[...] + _rms(acc, gpost_ref[...])


def _prompt_layer_kernel(sink_ref, x_ref, gpre_ref, gpost_ref, gmpre_ref, gmpost_ref, convw_ref,
                         win_ref, wao_ref, wco_ref, wout_ref, wup_ref, wdown_ref,
                         *rest, tm, tps, layer, first, n_cast):
    cast_src, rest = rest[:n_cast], rest[n_cast + (0 if first else 3):]
    y_ref, nk_ref, nv_ref, nc_ref = rest[:4]
    cast_dst, rest = rest[4:4 + n_cast], rest[4 + n_cast:]
    kdup, vdup, q4s, ubuf, attn_scr, x1_scr, klast, vlast, ulast = rest
    j = pl.program_id(0)

    def cast_next_layer():
        for src, dst in zip(cast_src, cast_dst):
            dst[...] = src[...].astype(_BF)

    n_tiles = pl.num_programs(0) - 1
    s = j % tps

    def start_sequence():
        @pl.when(s == 0)
        def _():
            kdup[:, 0:_WIN, :] = jnp.zeros((2, _WIN, _LANES), _BF)
            vdup[:, 0:_WIN, :] = jnp.zeros((2, _WIN, _LANES), _BF)
            ubuf[0:_SUB, :] = jnp.zeros((_SUB, _CONV), _F32)

    def end_sequence():
        @pl.when(s == tps - 1)
        def _():
            _own_layer(nk_ref, first)[...] = klast[...]
            _own_layer(nv_ref, first)[...] = vlast[...]
            _own_layer(nc_ref, first)[...] = ulast[_SUB - 2:_SUB, :]

    def mlp(src):
        return _mlp_stages(src, y_ref, gmpre_ref, gmpost_ref, wup_ref, wdown_ref)

    def mixer():
        return _mixer_prompt_stages(sink_ref, x_ref, gpre_ref, gpost_ref, convw_ref, win_ref, wao_ref, wco_ref,
                                    wout_ref, kdup, vdup, q4s, ubuf, attn_scr, x1_scr, klast, vlast, ulast, s,
                                    tm=tm, layer=layer)

    @pl.when(j == 0)
    def _():
        start_sequence()
        cast_next_layer()
        _run_interleaved(mixer())
        end_sequence()

    @pl.when((j > 0) & (j < n_tiles))
    def _():
        start_sequence()
        mlp_stages, mixer_stages = mlp(x1_scr), mixer()
        for _ in range(2 * (_DFF // _D) - 1):
            next(mlp_stages)
        cast_next_layer()
        next(mixer_stages)
        next(mlp_stages)
        next(mixer_stages)
        _run_interleaved(mlp_stages, mixer_stages)
        end_sequence()

    @pl.when(j == n_tiles)
    def _():
        cast_next_layer()
        _run_interleaved(mlp(x1_scr))


def _mixer_prompt_stages(sink_ref, x_ref, gpre_ref, gpost_ref, convw_ref, win_ref, wao_ref, wco_ref, wout_ref,
                         kdup, vdup, q4s, ubuf, attn_scr, x1_scr, klast, vlast, ulast, s, *, tm, layer):
    x = x_ref[...]
    h = _rms(x, gpre_ref[...]).astype(_BF)
    lo = lax.broadcasted_iota(jnp.int32, (tm, _LANES), 1) < _HD

    hr = tm // 2
    kv = jnp.concatenate([_proj(h[:hr], win_ref, _OFF_KV, 2 * _KVD), _proj(h[hr:], win_ref, _OFF_KV, 2 * _KVD)],
                         axis=0)
    k = kv[:, :_KVD]
    v = kv[:, _KVD:]
    for val, dst in ((k, kdup), (v, vdup)):
        d0, d1 = _dup_halves(val, lo)
        dst[0, _WIN:_WIN + tm, :] = d0.astype(_BF)
        dst[1, _WIN:_WIN + tm, :] = d1.astype(_BF)

    q = _proj(h, win_ref, _OFF_Q, _ATT) * (_HD ** -0.5)
    for c in range(4):
        qc = q[:, c * _LANES:(c + 1) * _LANES]
        q4s[c, 0] = jnp.where(lo, qc, 0.0).astype(_BF)
        q4s[c, 1] = jnp.where(lo, 0.0, qc).astype(_BF)

    qi = lax.broadcasted_iota(jnp.int32, (_WIN, 2 * _WIN), 0)
    kj = lax.broadcasted_iota(jnp.int32, (_WIN, 2 * _WIN), 1)
    band = _band_mask(qi, kj)
    neg = jnp.float32(-jnp.inf)
    bias = jnp.where(band, 0.0, neg)
    bias_first = jnp.where(band & ((kj >= _WIN) | (s > 0)), 0.0, neg)
    lo_q = lax.broadcasted_iota(jnp.int32, (_WIN, _LANES), 1) < _HD

    def scores_of(qb):
        r0 = qb * _WIN
        out = []
        for g in range(2):
            q4 = jnp.concatenate([q4s[2 * g, 0, r0:r0 + _WIN, :], q4s[2 * g, 1, r0:r0 + _WIN, :],
                                  q4s[2 * g + 1, 0, r0:r0 + _WIN, :], q4s[2 * g + 1, 1, r0:r0 + _WIN, :]], axis=0)
            out.append(_dot_nt(q4, kdup[g, r0:r0 + 2 * _WIN, :]))
        return out

    def finish_block(qb, scores):
        r0 = qb * _WIN
        b = bias_first if qb == 0 else bias
        for g in range(2):
            sinks = [sink_ref[layer, 4 * g + j] for j in range(4)]
            o = _softmax_pv(scores[g], vdup[g, r0:r0 + 2 * _WIN, :], b, sinks, _WIN)
            attn_scr[r0:r0 + _WIN, (2 * g) * _LANES:(2 * g + 1) * _LANES] = jnp.where(lo_q, o[0], o[1]).astype(_BF)
            attn_scr[r0:r0 + _WIN, (2 * g + 1) * _LANES:(2 * g + 2) * _LANES] = (
                jnp.where(lo_q, o[2], o[3]).astype(_BF))

    def conv_input():
        u = _proj(h, win_ref, _OFF_C, _CONV) * _proj(h, win_ref, _OFF_U, _CONV)
        ubuf[_SUB:_SUB + tm, :] = u
        return u

    fillers = [conv_input,
               lambda: jax.nn.sigmoid(_proj(h, win_ref, _OFF_GA, _D)),
               lambda: jax.nn.sigmoid(_proj(h, win_ref, _OFF_GC, _D)),
               lambda: _proj(h, win_ref, _OFF_B, _CONV)]
    filled, pending = [], []
    for qb in range(tm // _WIN):
        pending.append((qb, scores_of(qb)))
        if qb < len(fillers):
            filled.append(fillers[qb]())
        if len(pending) > 1:
            finish_block(*pending.pop(0))
    filled += [f() for f in fillers[len(filled):]]
    u, gate_attn, gate_conv, b_gate = filled

    cw = convw_ref[...]
    z = cw[0:1] * ubuf[_SUB - 2:_SUB - 2 + tm, :]
    z = z + cw[1:2] * ubuf[_SUB - 1:_SUB - 1 + tm, :]
    z = z + cw[2:3] * u
    conv_b = jnp.dot((b_gate * z).astype(_BF), wco_ref[...], preferred_element_type=_F32)
    for block in pending:
        finish_block(*block)

    kdup[:, 0:_WIN, :] = kdup[:, tm:tm + _WIN, :]
    vdup[:, 0:_WIN, :] = vdup[:, tm:tm + _WIN, :]

    mixed = gate_attn * jnp.dot(attn_scr[...], wao_ref[...], preferred_element_type=_F32)
    mixed = (mixed + gate_conv * conv_b).astype(_BF)
    yield
    mo = jnp.dot(mixed, wout_ref[...], preferred_element_type=_F32)
    yield
    x1_scr[...] = x + _rms(mo, gpost_ref[...])
    ubuf[0:_SUB, :] = ubuf[tm:tm + _SUB, :]
    klast[...] = k[tm - _WIN:tm, :]
    vlast[...] = v[tm - _WIN:tm, :]
    ulast[...] = u[tm - _SUB:tm, :]


def _mixer_sample_kernel(sink_ref, x_ref, gpre_ref, gpost_ref, convw_ref, ck_ref, cv_ref, st_ref,
                         win_ref, wao_ref, wco_ref, wout_ref, *rest, sb, t, layer, first):
    y_ref, nk_ref, nv_ref, nc_ref, ubuf = rest[0 if first else 3:]
    nk_ref, nv_ref, nc_ref = (_own_layer(r, first) for r in (nk_ref, nv_ref, nc_ref))
    tr = sb * t
    rows = _NH * t
    x = x_ref[...]
    h = _rms(x, gpre_ref[...]).astype(_BF)
    lo = lax.broadcasted_iota(jnp.int32, (tr, _LANES), 1) < _HD

    kv = _proj(h, win_ref, _OFF_KV, 2 * _KVD)
    k = kv[:, :_KVD]
    v = kv[:, _KVD:]
    q = _proj(h, win_ref, _OFF_Q, _ATT) * (_HD ** -0.5)
    pieces = []
    for hd in range(_NH):
        qc = q[:, (hd // 2) * _LANES:(hd // 2 + 1) * _LANES]
        if hd % 2 != hd // (_NH // 2):
            qc = pltpu.roll(qc, _HD, axis=1)
        piece = jnp.where(lo, qc, 0.0) if hd < _NH // 2 else jnp.where(lo, 0.0, qc)
        pieces.append(piece.reshape(sb, t, _LANES))
    q3 = jnp.concatenate(pieces, axis=1).astype(_BF)

    pad = jnp.zeros((sb, _BF_ROWS - t, _LANES), _F32)
    k_new = jnp.concatenate([k.reshape(sb, t, _KVD), pad], axis=1).astype(_BF)
    v_new = jnp.concatenate([v.reshape(sb, t, _KVD), pad], axis=1).astype(_BF)

    s_old = jnp.einsum('bqf,bfp->bqp', q3, ck_ref[...].astype(_BF), preferred_element_type=_F32)
    s_new = jnp.einsum('bqf,btf->bqt', q3, k_new, preferred_element_type=_F32)
    neg = jnp.float32(-jnp.inf)
    qi_old = lax.broadcasted_iota(jnp.int32, (rows, _WIN), 0) & (t - 1)
    kj_old = lax.broadcasted_iota(jnp.int32, (rows, _WIN), 1)
    qi_new = lax.broadcasted_iota(jnp.int32, (rows, _BF_ROWS), 0) & (t - 1)
    kj_new = lax.broadcasted_iota(jnp.int32, (rows, _BF_ROWS), 1) + _WIN
    s_old = s_old + jnp.where(_band_mask(qi_old, kj_old), 0.0, neg)[None]
    s_new = s_new + jnp.where(_band_mask(qi_new, kj_new), 0.0, neg)[None]
    sink_col = jnp.concatenate([jnp.full((t, 1), sink_ref[layer, hd], _F32) for hd in range(_NH)], axis=0)[None]
    m = jnp.maximum(jnp.maximum(jnp.max(s_old, axis=-1, keepdims=True), jnp.max(s_new, axis=-1, keepdims=True)),
                    sink_col)
    e_old = jnp.exp(s_old - m)
    e_new = jnp.exp(s_new - m)
    rden = 1.0 / (jnp.sum(e_old, axis=-1, keepdims=True) + jnp.sum(e_new, axis=-1, keepdims=True)
                  + jnp.exp(sink_col - m))
    o = (jnp.einsum('bqp,bfp->bqf', e_old.astype(_BF), cv_ref[...].astype(_BF), preferred_element_type=_F32)
         + jnp.einsum('bqt,btf->bqf', e_new.astype(_BF), v_new, preferred_element_type=_F32)) * rden
    chunks = []
    for c in range(_NH // 2):
        halves = []
        for hd in (2 * c, 2 * c + 1):
            oh = o[:, hd * t:(hd + 1) * t, :].reshape(tr, _LANES)
            if hd % 2 != hd // (_NH // 2):
                oh = pltpu.roll(oh, _HD, axis=1)
            halves.append(oh)
        chunks.append(jnp.where(lo, halves[0], halves[1]))
    attn = jnp.concatenate(chunks, axis=1)

    k_t = k.T
    v_t = v.T
    keep = lax.broadcasted_iota(jnp.int32, (_KVD, _WIN), 1) < _WIN - t
    per_chunk = _LANES // t
    for b in range(sb):
        c0 = (b // per_chunk) * _LANES
        shift_new = (_WIN - t - (b % per_chunk) * t) % _LANES
        for src_ref, new_t, dst_ref in ((ck_ref, k_t, nk_ref), (cv_ref, v_t, nv_ref)):
            old = pltpu.roll(src_ref[b], _WIN - t, axis=1)
            new = new_t[:, c0:c0 + _LANES]
            if shift_new:
                new = pltpu.roll(new, shift_new, axis=1)
            dst_ref[b] = jnp.where(keep, old, new)

    u = _proj(h, win_ref, _OFF_C, _CONV) * _proj(h, win_ref, _OFF_U, _CONV)
    u3 = u.reshape(sb, t, _CONV)
    ubuf[:, _SUB - 2:_SUB, :] = st_ref[...]
    ubuf[:, _SUB:_SUB + t, :] = u3
    us2 = ubuf[:, _SUB - 2:_SUB - 2 + t, :].reshape(tr, _CONV)
    us1 = ubuf[:, _SUB - 1:_SUB - 1 + t, :].reshape(tr, _CONV)
    nc_ref[...] = u3[:, t - 2:t, :]
    y_ref[...] = _mix_and_project(x, h, attn.astype(_BF), us2, us1, u,
                                  win_ref, convw_ref, wao_ref, wco_ref, wout_ref, gpost_ref)


def _mlp_kernel(x_ref, gpre_ref, gpost_ref, wup_ref, wdown_ref, y_ref):
    _run_interleaved(_mlp_stages(x_ref, y_ref, gpre_ref, gpost_ref, wup_ref, wdown_ref))


def _layer_spec(shape, layer, **kwargs):
    return pl.BlockSpec((None,) + shape, lambda *_: (layer,) + (0,) * len(shape), **kwargs)


def _weight_spec(w):
    return pl.BlockSpec(w.shape, lambda *_: (0, 0), pipeline_mode=pl.Buffered(1))


_SMEM_SPEC = pl.BlockSpec(memory_space=pltpu.SMEM)
_ANY_SPEC = pl.BlockSpec(memory_space=pl.ANY)


def _stacked_out_spec(depth, layer, first, block, index_map):
    lead, l0 = (depth, 0) if first else (None, layer)
    return pl.BlockSpec((lead,) + block, lambda *idx: (l0,) + index_map(*idx))


def _alias_args(prev, n_inputs):
    if prev is None:
        return [], [], {}
    return list(prev), [_ANY_SPEC] * len(prev), {n_inputs + i: 1 + i for i in range(len(prev))}


def _prompt_layer(x, layer, prev, sinks, gpre, gpost, gmpre, gmpost, convw, weights, next_f32):
    n, seq, _ = x.shape
    depth = sinks.shape[0]
    tm = _PROMPT_TM
    first = prev is None
    assert seq % tm == 0 and tm % _WIN == 0 and seq >= _WIN
    tps = seq // tm
    n_tiles = n * tps

    def tile_of(t):
        return (t // tps, t % tps, 0)

    clamp = lambda j: jnp.minimum(j, n_tiles - 1)
    mixer_tile = lambda j: tile_of(clamp(j))
    mlp_tile = lambda j: tile_of(jnp.maximum(j - 1, 0))
    per_b = lambda j: (clamp(j) // tps, 0, 0)

    cast_in, cast_in_specs, cast_out_specs, cast_out_shapes = [], [], [], []
    for w in (next_f32 or ()):
        rows, cols = w.shape[1:]
        assert rows % (n_tiles * _BF_ROWS) == 0
        chunk = rows // n_tiles
        cast_in.append(w)
        cast_in_specs.append(pl.BlockSpec((None, chunk, cols), lambda j: (layer + 1, clamp(j), 0)))
        cast_out_specs.append(pl.BlockSpec((chunk, cols), lambda j: (clamp(j), 0)))
        cast_out_shapes.append(jax.ShapeDtypeStruct((rows, cols), _BF))

    inputs = [sinks, x, gpre, gpost, gmpre, gmpost, convw, *weights, *cast_in]
    alias_in, alias_specs, aliases = _alias_args(prev, len(inputs))
    out = pl.pallas_call(
        functools.partial(_prompt_layer_kernel, tm=tm, tps=tps, layer=layer, first=first, n_cast=len(cast_in)),
        grid=(n_tiles + 1,),
        in_specs=[_SMEM_SPEC,
                  pl.BlockSpec((None, tm, _D), mixer_tile)]
                 + [_layer_spec((1, _D), layer)] * 4 + [_layer_spec((3, _CONV), layer)]
                 + [_weight_spec(w) for w in weights] + cast_in_specs + alias_specs,
        out_specs=[pl.BlockSpec((None, tm, _D), mlp_tile),
                   _stacked_out_spec(depth, layer, first, (None, _WIN, _KVD), per_b),
                   _stacked_out_spec(depth, layer, first, (None, _WIN, _KVD), per_b),
                   _stacked_out_spec(depth, layer, first, (None, 2, _CONV), per_b)] + cast_out_specs,
        out_shape=[jax.ShapeDtypeStruct((n, seq, _D), _F32),
                   jax.ShapeDtypeStruct((depth, n, _WIN, _KVD), _F32),
                   jax.ShapeDtypeStruct((depth, n, _WIN, _KVD), _F32),
                   jax.ShapeDtypeStruct((depth, n, 2, _CONV), _F32)] + cast_out_shapes,
        scratch_shapes=[pltpu.VMEM((2, _WIN + tm, _LANES), _BF),
                        pltpu.VMEM((2, _WIN + tm, _LANES), _BF),
                        pltpu.VMEM((4, 2, tm, _LANES), _BF),
                        pltpu.VMEM((_SUB + tm, _CONV), _F32),
                        pltpu.VMEM((tm, _ATT), _BF),
                        pltpu.VMEM((tm, _D), _F32),
                        pltpu.VMEM((_WIN, _KVD), _F32),
                        pltpu.VMEM((_WIN, _KVD), _F32),
                        pltpu.VMEM((_SUB, _CONV), _F32)],
        input_output_aliases=aliases,
        compiler_params=pltpu.CompilerParams(dimension_semantics=("arbitrary",),
                                             vmem_limit_bytes=_VMEM_LIMIT),
        name="prompt_layer",
    )(*inputs, *alias_in)
    return out[0], tuple(out[1:4]), tuple(out[4:])


def _mixer_sample(x, layer, prev, ck, cv, st, sinks, gpre, gpost, convw, win, wao, wco, wout):
    n, t, _ = x.shape
    depth = sinks.shape[0]
    sb = _SAMPLE_SB
    first = prev is None
    assert n % sb == 0 and t == _SUB and ck.shape == (depth, n, _KVD, _WIN)
    tr = sb * t
    rows = lambda i: (i, 0)
    seqs = lambda i: (i, 0, 0)
    cache_spec = pl.BlockSpec((None, sb, _KVD, _WIN), lambda i: (layer, i, 0, 0))
    state_spec = pl.BlockSpec((None, sb, 2, _CONV), lambda i: (layer, i, 0, 0))
    inputs = [sinks, x.reshape(n * t, _D), gpre, gpost, convw, ck, cv, st, win, wao, wco, wout]
    alias_in, alias_specs, aliases = _alias_args(prev, len(inputs))
    out = pl.pallas_call(
        functools.partial(_mixer_sample_kernel, sb=sb, t=t, layer=layer, first=first),
        grid=(n // sb,),
        in_specs=[_SMEM_SPEC,
                  pl.BlockSpec((tr, _D), rows),
                  _layer_spec((1, _D), layer), _layer_spec((1, _D), layer), _layer_spec((3, _CONV), layer),
                  cache_spec, cache_spec, state_spec]
                 + [_weight_spec(w) for w in (win, wao, wco, wout)] + alias_specs,
        out_specs=[pl.BlockSpec((tr, _D), rows),
                   _stacked_out_spec(depth, layer, first, (sb, _KVD, _WIN), seqs),
                   _stacked_out_spec(depth, layer, first, (sb, _KVD, _WIN), seqs),
                   _stacked_out_spec(depth, layer, first, (sb, 2, _CONV), seqs)],
        out_shape=[jax.ShapeDtypeStruct((n * t, _D), _F32),
                   jax.ShapeDtypeStruct((depth, n, _KVD, _WIN), _F32),
                   jax.ShapeDtypeStruct((depth, n, _KVD, _WIN), _F32),
                   jax.ShapeDtypeStruct((depth, n, 2, _CONV), _F32)],
        scratch_shapes=[pltpu.VMEM((sb, 2 * _SUB, _CONV), _F32)],
        input_output_aliases=aliases,
        compiler_params=pltpu.CompilerParams(dimension_semantics=("arbitrary",),
                                             vmem_limit_bytes=_VMEM_LIMIT),
        name="mixer_sample",
    )(*inputs, *alias_in)
    return out[0].reshape(n, t, _D), tuple(out[1:])


def _mlp(x, layer, gpre, gpost, wup, wdown):
    shape = x.shape
    x2 = x.reshape(-1, _D)
    nt = x2.shape[0]
    tm = _MLP_TM
    assert nt % tm == 0
    rows = lambda i: (i, 0)
    y = pl.pallas_call(
        _mlp_kernel,
        grid=(nt // tm,),
        in_specs=[pl.BlockSpec((tm, _D), rows), _layer_spec((1, _D), layer), _layer_spec((1, _D), layer),
                  _weight_spec(wup), _weight_spec(wdown)],
        out_specs=pl.BlockSpec((tm, _D), rows),
        out_shape=jax.ShapeDtypeStruct((nt, _D), _F32),
        compiler_params=pltpu.CompilerParams(dimension_semantics=("parallel",),
                                             vmem_limit_bytes=_VMEM_LIMIT),
        name="mlp",
    )(x2, gpre, gpost, wup, wdown)
    return y.reshape(shape)


def kernel(x_prompt, x_sample, cache_k, cache_v, state_conv, g_mix_pre, g_mix_post, g_mlp_pre, g_mlp_post,
           w_in, attn_sinks, conv_w, w_attn_o, w_conv_o, w_out, w_up, w_down):
    depth = w_in.shape[0]
    n_dec = x_sample.shape[0]
    feature_major = lambda c: jnp.transpose(c, (0, 1, 3, 4, 2)).reshape(depth, n_dec, _KVD, _WIN)
    ck, cv = feature_major(cache_k), feature_major(cache_v)
    weights_f32 = (w_in, w_attn_o, w_conv_o, w_out, w_up, w_down)
    gains = [g.reshape(depth, 1, _D) for g in (g_mix_pre, g_mix_post, g_mlp_pre, g_mlp_post)]

    weights = tuple(w[0].astype(_BF) for w in weights_f32)
    yp, ys = x_prompt, x_sample
    caches_p = caches_s = None
    for l in range(depth):
        yp, caches_p, next_weights = _prompt_layer(yp, l, caches_p, attn_sinks, *gains, conv_w, weights,
                                                   weights_f32 if l + 1 < depth else None)
        ys, caches_s = _mixer_sample(ys, l, caches_s, ck, cv, state_conv, attn_sinks, gains[0], gains[1], conv_w,
                                     *weights[:4])
        ys = _mlp(ys, l, gains[2], gains[3], *weights[4:])
        weights = next_weights

    kv5 = lambda a: a.reshape(a.shape[:3] + (2, _HD))
    position_major = lambda a: jnp.transpose(a.reshape(depth, n_dec, 2, _HD, _WIN), (0, 1, 4, 2, 3))
    return (yp, ys, kv5(caches_p[0]), kv5(caches_p[1]), caches_p[2],
            position_major(caches_s[0]), position_major(caches_s[1]), caches_s[2])
```

```python
import functools

import jax
import jax.numpy as jnp
from jax import lax
from jax.experimental import pallas as pl
from jax.experimental.pallas import tpu as pltpu

_D = 1024
_HD = 64
_NH = 8
_ATT = _NH * _HD
_KVD = 128
_CONV = 512
_DFF = 4096
_WIN = 128
_EPS = 1e-6
_LANES = 128
_SUB = 8
_BF_ROWS = 16

_OFF_Q = 0
_OFF_KV = _ATT
_OFF_B = _OFF_KV + 2 * _KVD
_OFF_C = _OFF_B + _CONV
_OFF_U = _OFF_C + _CONV
_OFF_GA = _OFF_U + _CONV
_OFF_GC = _OFF_GA + _D
_IN_DIM = _OFF_GC + _D

_PROMPT_TM = 512
_SAMPLE_SB = 32
_MLP_TM = 512
_VMEM_LIMIT = 62 * 1024 * 1024

_BF = jnp.bfloat16
_F32 = jnp.float32


def _rms(x, g):
    return x * lax.rsqrt(jnp.mean(x * x, axis=-1, keepdims=True) + _EPS) * g


def _proj(h_bf, win_ref, off, width):
    return jnp.dot(h_bf, win_ref[:, off:off + width], preferred_element_type=_F32)


def _dup_halves(val, lo):
    rolled = pltpu.roll(val, _HD, axis=1)
    return jnp.where(lo, val, rolled), jnp.where(lo, rolled, val)


def _band_mask(qi, kj):
    return (kj > qi) & (kj <= qi + _WIN)


def _dot_nt(a, b):
    return lax.dot_general(a, b, (((1,), (1,)), ((), ())), preferred_element_type=_F32)


def _own_layer(ref, first):
    if not first:
        return ref
    if ref.shape[0] > 1:
        ref[1:] = jnp.zeros((ref.shape[0] - 1,) + ref.shape[1:], ref.dtype)
    return ref.at[0]


def _softmax_pv(s, vd, bias, sinks, nq):
    es, rden = [], []
    for j in range(4):
        sj = s[j * nq:(j + 1) * nq] + bias
        m = jnp.maximum(jnp.max(sj, axis=-1, keepdims=True), sinks[j])
        e = jnp.exp(sj - m)
        den = jnp.sum(e, axis=-1, keepdims=True) + jnp.exp(sinks[j] - m)
        es.append(e)
        rden.append(1.0 / den)
    e_all = jnp.concatenate(es, axis=0).astype(_BF)
    o = jnp.dot(e_all, vd, preferred_element_type=_F32)
    return [o[j * nq:(j + 1) * nq] * rden[j] for j in range(4)]


def _mix_and_project(x, h, attn_bf, u_shift2, u_shift1, u, win_ref, convw_ref, wao_ref, wco_ref, wout_ref,
                     gpost_ref):
    attn_b = jnp.dot(attn_bf, wao_ref[...], preferred_element_type=_F32)
    mixed = jax.nn.sigmoid(_proj(h, win_ref, _OFF_GA, _D)) * attn_b
    cw = convw_ref[...]
    z = cw[0:1] * u_shift2
    z = z + cw[1:2] * u_shift1
    z = z + cw[2:3] * u
    bz = (_proj(h, win_ref, _OFF_B, _CONV) * z).astype(_BF)
    conv_b = jnp.dot(bz, wco_ref[...], preferred_element_type=_F32)
    mixed = mixed + jax.nn.sigmoid(_proj(h, win_ref, _OFF_GC, _D)) * conv_b
    mo = jnp.dot(mixed.astype(_BF), wout_ref[...], preferred_element_type=_F32)
    return x + _rms(mo, gpost_ref[...])


def _run_interleaved(*stages):
    stages = list(stages)
    while stages:
        for g in list(stages):
            try:
                next(g)
            except StopIteration:
                stages.remove(g)


def _mlp_stages(x_ref, y_ref, gpre_ref, gpost_ref, wup_ref, wdown_ref):
    x = x_ref[...]
    y_ref[...] = x
    hm = _rms(x, gpre_ref[...]).astype(_BF)
    yield
    acc = None
    for j in range(_DFF // _D):
        hj = jnp.dot(hm, wup_ref[:, j * _D:(j + 1) * _D], preferred_element_type=_F32)
        yield
        hj = jnp.square(jnp.maximum(hj, 0.0)).astype(_BF)
        part = jnp.dot(hj, wdown_ref[j * _D:(j + 1) * _D, :], preferred_element_type=_F32)
        acc = part if acc is None else acc + part
        yield
    y_ref[...] = y_ref[...] + _rms(acc, gpost_ref[...])


_N_WEIGHTS = 6


def _prompt_layer_kernel(sink_ref, x_ref, gpre_ref, gpost_ref, gmpre_ref, gmpost_ref, convw_ref,
                         *rest, tm, tps, layer, first, n_pro, n_cast):
    own_cast = n_pro > 0
    if not own_cast:
        weights, rest = rest[:_N_WEIGHTS], rest[_N_WEIGHTS:]
    cast_src, rest = rest[:n_cast], rest[n_cast + (0 if first else 3):]
    y_ref, nk_ref, nv_ref, nc_ref = rest[:4]
    cast_dst, rest = rest[4:4 + n_cast], rest[4 + n_cast:]
    if own_cast:
        weights, rest = rest[:_N_WEIGHTS], rest[_N_WEIGHTS:]
    win_ref, wao_ref, wco_ref, wout_ref, wup_ref, wdown_ref = weights
    kdup, vdup, q4s, ubuf, attn_scr, x1_scr, klast, vlast, ulast = rest
    step = pl.program_id(0)

    if own_cast:
        @pl.when(step < n_pro)
        def _():
            for src, dst_hbm, dst_vmem in zip(cast_src, cast_dst, weights):
                chunk = src.shape[0]
                w = src[...].astype(_BF)
                dst_hbm[...] = w
                dst_vmem[pl.ds(pl.multiple_of(step * chunk, chunk), chunk), :] = w

    def cast_next_layer():
        for src, dst in zip(cast_src, cast_dst):
            dst[...] = src[...].astype(_BF)

    j = step - n_pro
    n_tiles = pl.num_programs(0) - n_pro - 1
    s = jnp.maximum(j, 0) % tps

    def start_sequence():
        @pl.when(s == 0)
        def _():
            kdup[:, 0:_WIN, :] = jnp.zeros((2, _WIN, _LANES), _BF)
            vdup[:, 0:_WIN, :] = jnp.zeros((2, _WIN, _LANES), _BF)
            ubuf[0:_SUB, :] = jnp.zeros((_SUB, _CONV), _F32)

    def end_sequence():
        @pl.when(s == tps - 1)
        def _():
            _own_layer(nk_ref, first)[...] = klast[...]
            _own_layer(nv_ref, first)[...] = vlast[...]
            _own_layer(nc_ref, first)[...] = ulast[_SUB - 2:_SUB, :]

    def mlp(src):
        return _mlp_stages(src, y_ref, gmpre_ref, gmpost_ref, wup_ref, wdown_ref)

    def mixer():
        return _mixer_prompt_stages(sink_ref, x_ref, gpre_ref, gpost_ref, convw_ref, win_ref, wao_ref, wco_ref,
                                    wout_ref, kdup, vdup, q4s, ubuf, attn_scr, x1_scr, klast, vlast, ulast, s,
                                    tm=tm, layer=layer)

    @pl.when(j == 0)
    def _():
        start_sequence()
        cast_next_layer()
        _run_interleaved(mixer())
        end_sequence()

    @pl.when((j > 0) & (j < n_tiles))
    def _():
        start_sequence()
        mlp_stages, mixer_stages = mlp(x1_scr), mixer()
        for _ in range(2 * (_DFF // _D) - 1):
            next(mlp_stages)
        cast_next_layer()
        next(mixer_stages)
        next(mlp_stages)
        next(mixer_stages)
        _run_interleaved(mlp_stages, mixer_stages)
        end_sequence()

    @pl.when(j == n_tiles)
    def _():
        cast_next_layer()
        _run_interleaved(mlp(x1_scr))


def _mixer_prompt_stages(sink_ref, x_ref, gpre_ref, gpost_ref, convw_ref, win_ref, wao_ref, wco_ref, wout_ref,
                         kdup, vdup, q4s, ubuf, attn_scr, x1_scr, klast, vlast, ulast, s, *, tm, layer):
    x = x_ref[...]
    h = _rms(x, gpre_ref[...]).astype(_BF)
    lo = lax.broadcasted_iota(jnp.int32, (tm, _LANES), 1) < _HD

    hr = tm // 2
    kv = jnp.concatenate([_proj(h[:hr], win_ref, _OFF_KV, 2 * _KVD), _proj(h[hr:], win_ref, _OFF_KV, 2 * _KVD)],
                         axis=0)
    k = kv[:, :_KVD]
    v = kv[:, _KVD:]
    for val, dst in ((k, kdup), (v, vdup)):
        d0, d1 = _dup_halves(val, lo)
        dst[0, _WIN:_WIN + tm, :] = d0.astype(_BF)
        dst[1, _WIN:_WIN + tm, :] = d1.astype(_BF)

    q = _proj(h, win_ref, _OFF_Q, _ATT) * (_HD ** -0.5)
    for c in range(4):
        qc = q[:, c * _LANES:(c + 1) * _LANES]
        q4s[c, 0] = jnp.where(lo, qc, 0.0).astype(_BF)
        q4s[c, 1] = jnp.where(lo, 0.0, qc).astype(_BF)

    qi = lax.broadcasted_iota(jnp.int32, (_WIN, 2 * _WIN), 0)
    kj = lax.broadcasted_iota(jnp.int32, (_WIN, 2 * _WIN), 1)
    band = _band_mask(qi, kj)
    neg = jnp.float32(-jnp.inf)
    bias = jnp.where(band, 0.0, neg)
    bias_first = jnp.where(band & ((kj >= _WIN) | (s > 0)), 0.0, neg)
    lo_q = lax.broadcasted_iota(jnp.int32, (_WIN, _LANES), 1) < _HD

    def scores_of(qb):
        r0 = qb * _WIN
        out = []
        for g in range(2):
            q4 = jnp.concatenate([q4s[2 * g, 0, r0:r0 + _WIN, :], q4s[2 * g, 1, r0:r0 + _WIN, :],
                                  q4s[2 * g + 1, 0, r0:r0 + _WIN, :], q4s[2 * g + 1, 1, r0:r0 + _WIN, :]], axis=0)
            out.append(_dot_nt(q4, kdup[g, r0:r0 + 2 * _WIN, :]))
        return out

    def finish_block(qb, scores):
        r0 = qb * _WIN
        b = bias_first if qb == 0 else bias
        for g in range(2):
            sinks = [sink_ref[layer, 4 * g + j] for j in range(4)]
            o = _softmax_pv(scores[g], vdup[g, r0:r0 + 2 * _WIN, :], b, sinks, _WIN)
            attn_scr[r0:r0 + _WIN, (2 * g) * _LANES:(2 * g + 1) * _LANES] = jnp.where(lo_q, o[0], o[1]).astype(_BF)
            attn_scr[r0:r0 + _WIN, (2 * g + 1) * _LANES:(2 * g + 2) * _LANES] = (
                jnp.where(lo_q, o[2], o[3]).astype(_BF))

    def conv_input():
        u = _proj(h, win_ref, _OFF_C, _CONV) * _proj(h, win_ref, _OFF_U, _CONV)
        ubuf[_SUB:_SUB + tm, :] = u
        return u

    fillers = [conv_input,
               lambda: jax.nn.sigmoid(_proj(h, win_ref, _OFF_GA, _D)),
               lambda: jax.nn.sigmoid(_proj(h, win_ref, _OFF_GC, _D)),
               lambda: _proj(h, win_ref, _OFF_B, _CONV)]
    filled, pending = [], []
    for qb in range(tm // _WIN):
        pending.append((qb, scores_of(qb)))
        if qb < len(fillers):
            filled.append(fillers[qb]())
        if len(pending) > 1:
            finish_block(*pending.pop(0))
    filled += [f() for f in fillers[len(filled):]]
    u, gate_attn, gate_conv, b_gate = filled

    cw = convw_ref[...]
    z = cw[0:1] * ubuf[_SUB - 2:_SUB - 2 + tm, :]
    z = z + cw[1:2] * ubuf[_SUB - 1:_SUB - 1 + tm, :]
    z = z + cw[2:3] * u
    conv_b = jnp.dot((b_gate * z).astype(_BF), wco_ref[...], preferred_element_type=_F32)
    for block in pending:
        finish_block(*block)

    kdup[:, 0:_WIN, :] = kdup[:, tm:tm + _WIN, :]
    vdup[:, 0:_WIN, :] = vdup[:, tm:tm + _WIN, :]

    mixed = gate_attn * jnp.dot(attn_scr[...], wao_ref[...], preferred_element_type=_F32)
    mixed = (mixed + gate_conv * conv_b).astype(_BF)
    yield
    mo = jnp.dot(mixed, wout_ref[...], preferred_element_type=_F32)
    yield
    x1_scr[...] = x + _rms(mo, gpost_ref[...])
    ubuf[0:_SUB, :] = ubuf[tm:tm + _SUB, :]
    klast[...] = k[tm - _WIN:tm, :]
    vlast[...] = v[tm - _WIN:tm, :]
    ulast[...] = u[tm - _SUB:tm, :]


def _mixer_sample_kernel(sink_ref, x_ref, gpre_ref, gpost_ref, convw_ref, ck_ref, cv_ref, st_ref,
                         win_ref, wao_ref, wco_ref, wout_ref, *rest, sb, t, layer, first):
    y_ref, nk_ref, nv_ref, nc_ref, ubuf = rest[0 if first else 3:]
    nk_ref, nv_ref, nc_ref = (_own_layer(r, first) for r in (nk_ref, nv_ref, nc_ref))
    tr = sb * t
    rows = _NH * t
    x = x_ref[...]
    h = _rms(x, gpre_ref[...]).astype(_BF)
    lo = lax.broadcasted_iota(jnp.int32, (tr, _LANES), 1) < _HD

    kv = _proj(h, win_ref, _OFF_KV, 2 * _KVD)
    k = kv[:, :_KVD]
    v = kv[:, _KVD:]
    q = _proj(h, win_ref, _OFF_Q, _ATT) * (_HD ** -0.5)
    pieces = []
    for hd in range(_NH):
        qc = q[:, (hd // 2) * _LANES:(hd // 2 + 1) * _LANES]
        if hd % 2 != hd // (_NH // 2):
            qc = pltpu.roll(qc, _HD, axis=1)
        piece = jnp.where(lo, qc, 0.0) if hd < _NH // 2 else jnp.where(lo, 0.0, qc)
        pieces.append(piece.reshape(sb, t, _LANES))
    q3 = jnp.concatenate(pieces, axis=1).astype(_BF)

    pad = jnp.zeros((sb, _BF_ROWS - t, _LANES), _F32)
    k_new = jnp.concatenate([k.reshape(sb, t, _KVD), pad], axis=1).astype(_BF)
    v_new = jnp.concatenate([v.reshape(sb, t, _KVD), pad], axis=1).astype(_BF)

    s_old = jnp.einsum('bqf,bfp->bqp', q3, ck_ref[...].astype(_BF), preferred_element_type=_F32)
    s_new = jnp.einsum('bqf,btf->bqt', q3, k_new, preferred_element_type=_F32)
    neg = jnp.float32(-jnp.inf)
    qi_old = lax.broadcasted_iota(jnp.int32, (rows, _WIN), 0) & (t - 1)
    kj_old = lax.broadcasted_iota(jnp.int32, (rows, _WIN), 1)
    qi_new = lax.broadcasted_iota(jnp.int32, (rows, _BF_ROWS), 0) & (t - 1)
    kj_new = lax.broadcasted_iota(jnp.int32, (rows, _BF_ROWS), 1) + _WIN
    s_old = s_old + jnp.where(_band_mask(qi_old, kj_old), 0.0, neg)[None]
    s_new = s_new + jnp.where(_band_mask(qi_new, kj_new), 0.0, neg)[None]
    sink_col = jnp.concatenate([jnp.full((t, 1), sink_ref[layer, hd], _F32) for hd in range(_NH)], axis=0)[None]
    m = jnp.maximum(jnp.maximum(jnp.max(s_old, axis=-1, keepdims=True), jnp.max(s_new, axis=-1, keepdims=True)),
                    sink_col)
    e_old = jnp.exp(s_old - m)
    e_new = jnp.exp(s_new - m)
    rden = 1.0 / (jnp.sum(e_old, axis=-1, keepdims=True) + jnp.sum(e_new, axis=-1, keepdims=True)
                  + jnp.exp(sink_col - m))
    o = (jnp.einsum('bqp,bfp->bqf', e_old.astype(_BF), cv_ref[...].astype(_BF), preferred_element_type=_F32)
         + jnp.einsum('bqt,btf->bqf', e_new.astype(_BF), v_new, preferred_element_type=_F32)) * rden
    chunks = []
    for c in range(_NH // 2):
        halves = []
        for hd in (2 * c, 2 * c + 1):
            oh = o[:, hd * t:(hd + 1) * t, :].reshape(tr, _LANES)
            if hd % 2 != hd // (_NH // 2):
                oh = pltpu.roll(oh, _HD, axis=1)
            halves.append(oh)
        chunks.append(jnp.where(lo, halves[0], halves[1]))
    attn = jnp.concatenate(chunks, axis=1)

    k_t = k.T
    v_t = v.T
    keep = lax.broadcasted_iota(jnp.int32, (_KVD, _WIN), 1) < _WIN - t
    per_chunk = _LANES // t
    for b in range(sb):
        c0 = (b // per_chunk) * _LANES
        shift_new = (_WIN - t - (b % per_chunk) * t) % _LANES
        for src_ref, new_t, dst_ref in ((ck_ref, k_t, nk_ref), (cv_ref, v_t, nv_ref)):
            old = pltpu.roll(src_ref[b], _WIN - t, axis=1)
            new = new_t[:, c0:c0 + _LANES]
            if shift_new:
                new = pltpu.roll(new, shift_new, axis=1)
            dst_ref[b] = jnp.where(keep, old, new)

    u = _proj(h, win_ref, _OFF_C, _CONV) * _proj(h, win_ref, _OFF_U, _CONV)
    u3 = u.reshape(sb, t, _CONV)
    ubuf[:, _SUB - 2:_SUB, :] = st_ref[...]
    ubuf[:, _SUB:_SUB + t, :] = u3
    us2 = ubuf[:, _SUB - 2:_SUB - 2 + t, :].reshape(tr, _CONV)
    us1 = ubuf[:, _SUB - 1:_SUB - 1 + t, :].reshape(tr, _CONV)
    nc_ref[...] = u3[:, t - 2:t, :]
    y_ref[...] = _mix_and_project(x, h, attn.astype(_BF), us2, us1, u,
                                  win_ref, convw_ref, wao_ref, wco_ref, wout_ref, gpost_ref)


def _mlp_kernel(x_ref, gpre_ref, gpost_ref, wup_ref, wdown_ref, y_ref):
    _run_interleaved(_mlp_stages(x_ref, y_ref, gpre_ref, gpost_ref, wup_ref, wdown_ref))


def _layer_spec(shape, layer, **kwargs):
    return pl.BlockSpec((None,) + shape, lambda *_: (layer,) + (0,) * len(shape), **kwargs)


def _weight_spec(w, slot):
    return pl.BlockSpec((None,) + w.shape[1:], lambda *_: (slot, 0, 0), pipeline_mode=pl.Buffered(1))


_SMEM_SPEC = pl.BlockSpec(memory_space=pltpu.SMEM)
_ANY_SPEC = pl.BlockSpec(memory_space=pl.ANY)


def _stacked_out_spec(depth, layer, first, block, index_map):
    lead, l0 = (depth, 0) if first else (None, layer)
    return pl.BlockSpec((lead,) + block, lambda *idx: (l0,) + index_map(*idx))


def _alias_args(prev, n_inputs):
    if prev is None:
        return [], [], {}
    return list(prev), [_ANY_SPEC] * len(prev), {n_inputs + i: 1 + i for i in range(len(prev))}


def _prompt_layer(x, layer, prev, sinks, gpre, gpost, gmpre, gmpost, convw, weights, weights_f32):
    n, seq, _ = x.shape
    depth = sinks.shape[0]
    tm = _PROMPT_TM
    first = prev is None
    own_cast = weights is None
    has_next = layer + 1 < depth
    assert seq % tm == 0 and tm % _WIN == 0 and seq >= _WIN
    tps = seq // tm
    n_tiles = n * tps
    n_pro = n_tiles if own_cast else 0
    n_slots = int(own_cast) + int(has_next)

    def tile_of(t):
        return (t // tps, t % tps, 0)

    mixer_j = lambda i: jnp.clip(i - n_pro, 0, n_tiles - 1)
    mixer_tile = lambda i: tile_of(mixer_j(i))
    mlp_tile = lambda i: tile_of(jnp.maximum(i - n_pro - 1, 0))
    per_b = lambda i: (mixer_j(i) // tps, 0, 0)

    def cast_block(i):
        in_pro = i < n_pro
        chunk = jnp.where(in_pro, i, mixer_j(i)) if has_next else jnp.minimum(i, n_pro - 1)
        src_layer = jnp.where(in_pro, layer, layer + 1) if has_next else layer
        slot = jnp.where(in_pro, 0, n_slots - 1)
        return src_layer, slot, chunk

    cast_in, cast_in_specs, cast_out_specs, cast_out_shapes, weight_scratch = [], [], [], [], []
    for w in (weights_f32 if n_slots else ()):
        rows, cols = w.shape[1:]
        assert rows % (n_tiles * _BF_ROWS) == 0
        chunk = rows // n_tiles
        cast_in.append(w)
        cast_in_specs.append(pl.BlockSpec((None, chunk, cols),
                                          lambda i: (cast_block(i)[0], cast_block(i)[2], 0)))
        cast_out_specs.append(pl.BlockSpec((None, chunk, cols),
                                           lambda i: (cast_block(i)[1], cast_block(i)[2], 0)))
        cast_out_shapes.append(jax.ShapeDtypeStruct((n_slots, rows, cols), _BF))
        if own_cast:
            weight_scratch.append(pltpu.VMEM((rows, cols), _BF))

    weight_in = [] if own_cast else list(weights[0])
    weight_specs = [] if own_cast else [_weight_spec(w, weights[1]) for w in weights[0]]
    inputs = [sinks, x, gpre, gpost, gmpre, gmpost, convw, *weight_in, *cast_in]
    alias_in, alias_specs, aliases = _alias_args(prev, len(inputs))
    out = pl.pallas_call(
        functools.partial(_prompt_layer_kernel, tm=tm, tps=tps, layer=layer, first=first, n_pro=n_pro,
                          n_cast=len(cast_in)),
        grid=(n_pro + n_tiles + 1,),
        in_specs=[_SMEM_SPEC,
                  pl.BlockSpec((None, tm, _D), mixer_tile)]
                 + [_layer_spec((1, _D), layer)] * 4 + [_layer_spec((3, _CONV), layer)]
                 + weight_specs + cast_in_specs + alias_specs,
        out_specs=[pl.BlockSpec((None, tm, _D), mlp_tile),
                   _stacked_out_spec(depth, layer, first, (None, _WIN, _KVD), per_b),
                   _stacked_out_spec(depth, layer, first, (None, _WIN, _KVD), per_b),
                   _stacked_out_spec(depth, layer, first, (None, 2, _CONV), per_b)] + cast_out_specs,
        out_shape=[jax.ShapeDtypeStruct((n, seq, _D), _F32),
                   jax.ShapeDtypeStruct((depth, n, _WIN, _KVD), _F32),
                   jax.ShapeDtypeStruct((depth, n, _WIN, _KVD), _F32),
                   jax.ShapeDtypeStruct((depth, n, 2, _CONV), _F32)] + cast_out_shapes,
        scratch_shapes=weight_scratch + [
                        pltpu.VMEM((2, _WIN + tm, _LANES), _BF),
                        pltpu.VMEM((2, _WIN + tm, _LANES), _BF),
                        pltpu.VMEM((4, 2, tm, _LANES), _BF),
                        pltpu.VMEM((_SUB + tm, _CONV), _F32),
                        pltpu.VMEM((tm, _ATT), _BF),
                        pltpu.VMEM((tm, _D), _F32),
                        pltpu.VMEM((_WIN, _KVD), _F32),
                        pltpu.VMEM((_WIN, _KVD), _F32),
                        pltpu.VMEM((_SUB, _CONV), _F32)],
        input_output_aliases=aliases,
        compiler_params=pltpu.CompilerParams(dimension_semantics=("arbitrary",),
                                             vmem_limit_bytes=_VMEM_LIMIT),
        name="prompt_layer",
    )(*inputs, *alias_in)
    return out[0], tuple(out[1:4]), tuple(out[4:])


def _mixer_sample(x, layer, prev, ck, cv, st, sinks, gpre, gpost, convw, slot, win, wao, wco, wout):
    n, t, _ = x.shape
    depth = sinks.shape[0]
    sb = _SAMPLE_SB
    first = prev is None
    assert n % sb == 0 and t == _SUB and ck.shape == (depth, n, _KVD, _WIN)
    tr = sb * t
    rows = lambda i: (i, 0)
    seqs = lambda i: (i, 0, 0)
    cache_spec = pl.BlockSpec((None, sb, _KVD, _WIN), lambda i: (layer, i, 0, 0))
    state_spec = pl.BlockSpec((None, sb, 2, _CONV), lambda i: (layer, i, 0, 0))
    inputs = [sinks, x.reshape(n * t, _D), gpre, gpost, convw, ck, cv, st, win, wao, wco, wout]
    alias_in, alias_specs, aliases = _alias_args(prev, len(inputs))
    out = pl.pallas_call(
        functools.partial(_mixer_sample_kernel, sb=sb, t=t, layer=layer, first=first),
        grid=(n // sb,),
        in_specs=[_SMEM_SPEC,
                  pl.BlockSpec((tr, _D), rows),
                  _layer_spec((1, _D), layer), _layer_spec((1, _D), layer), _layer_spec((3, _CONV), layer),
                  cache_spec, cache_spec, state_spec]
                 + [_weight_spec(w, slot) for w in (win, wao, wco, wout)] + alias_specs,
        out_specs=[pl.BlockSpec((tr, _D), rows),
                   _stacked_out_spec(depth, layer, first, (sb, _KVD, _WIN), seqs),
                   _stacked_out_spec(depth, layer, first, (sb, _KVD, _WIN), seqs),
                   _stacked_out_spec(depth, layer, first, (sb, 2, _CONV), seqs)],
        out_shape=[jax.ShapeDtypeStruct((n * t, _D), _F32),
                   jax.ShapeDtypeStruct((depth, n, _KVD, _WIN), _F32),
                   jax.ShapeDtypeStruct((depth, n, _KVD, _WIN), _F32),
                   jax.ShapeDtypeStruct((depth, n, 2, _CONV), _F32)],
        scratch_shapes=[pltpu.VMEM((sb, 2 * _SUB, _CONV), _F32)],
        input_output_aliases=aliases,
        compiler_params=pltpu.CompilerParams(dimension_semantics=("arbitrary",),
                                             vmem_limit_bytes=_VMEM_LIMIT),
        name="mixer_sample",
    )(*inputs, *alias_in)
    return out[0].reshape(n, t, _D), tuple(out[1:])


def _mlp(x, layer, gpre, gpost, slot, wup, wdown):
    shape = x.shape
    x2 = x.reshape(-1, _D)
    nt = x2.shape[0]
    tm = _MLP_TM
    assert nt % tm == 0
    rows = lambda i: (i, 0)
    y = pl.pallas_call(
        _mlp_kernel,
        grid=(nt // tm,),
        in_specs=[pl.BlockSpec((tm, _D), rows), _layer_spec((1, _D), layer), _layer_spec((1, _D), layer),
                  _weight_spec(wup, slot), _weight_spec(wdown, slot)],
        out_specs=pl.BlockSpec((tm, _D), rows),
        out_shape=jax.ShapeDtypeStruct((nt, _D), _F32),
        compiler_params=pltpu.CompilerParams(dimension_semantics=("parallel",),
                                             vmem_limit_bytes=_VMEM_LIMIT),
        name="mlp",
    )(x2, gpre, gpost, wup, wdown)
    return y.reshape(shape)


def kernel(x_prompt, x_sample, cache_k, cache_v, state_conv, g_mix_pre, g_mix_post, g_mlp_pre, g_mlp_post,
           w_in, attn_sinks, conv_w, w_attn_o, w_conv_o, w_out, w_up, w_down):
    depth = w_in.shape[0]
    n_dec = x_sample.shape[0]
    feature_major = lambda c: jnp.transpose(c, (0, 1, 3, 4, 2)).reshape(depth, n_dec, _KVD, _WIN)
    ck, cv = feature_major(cache_k), feature_major(cache_v)
    weights_f32 = (w_in, w_attn_o, w_conv_o, w_out, w_up, w_down)
    gains = [g.reshape(depth, 1, _D) for g in (g_mix_pre, g_mix_post, g_mlp_pre, g_mlp_post)]

    weights = None
    yp, ys = x_prompt, x_sample
    caches_p = caches_s = None
    for l in range(depth):
        yp, caches_p, cast = _prompt_layer(yp, l, caches_p, attn_sinks, *gains, conv_w, weights, weights_f32)
        if weights is None:
            weights = (cast, 0)
        ys, caches_s = _mixer_sample(ys, l, caches_s, ck, cv, state_conv, attn_sinks, gains[0], gains[1], conv_w,
                                     weights[1], *weights[0][:4])
        ys = _mlp(ys, l, gains[2], gains[3], weights[1], *weights[0][4:])
        if l + 1 < depth:
            weights = (cast, cast[0].shape[0] - 1)

    kv5 = lambda a: a.reshape(a.shape[:3] + (2, _HD))
    position_major = lambda a: jnp.transpose(a.reshape(depth, n_dec, 2, _HD, _WIN), (0, 1, 4, 2, 3))
    return (yp, ys, kv5(caches_p[0]), kv5(caches_p[1]), caches_p[2],
            position_major(caches_s[0]), position_major(caches_s[1]), caches_s[2])
```

```python
import functools

import jax
import jax.numpy as jnp
from jax import lax
from jax.experimental import pallas as pl
from jax.experimental.pallas import tpu as pltpu

_D = 1024
_HD = 64
_NH = 8
_ATT = _NH * _HD
_KVD = 128
_CONV = 512
_DFF = 4096
_WIN = 128
_EPS = 1e-6
_LANES = 128
_SUB = 8
_BF_ROWS = 16

_OFF_Q = 0
_OFF_KV = _ATT
_OFF_B = _OFF_KV + 2 * _KVD
_OFF_C = _OFF_B + _CONV
_OFF_U = _OFF_C + _CONV
_OFF_GA = _OFF_U + _CONV
_OFF_GC = _OFF_GA + _D
_IN_DIM = _OFF_GC + _D

_PROMPT_TM = 512
_SAMPLE_SB = 32
_MLP_TM = 512
_VMEM_LIMIT = 62 * 1024 * 1024

_BF = jnp.bfloat16
_F32 = jnp.float32


def _rms(x, g):
    return x * lax.rsqrt(jnp.mean(x * x, axis=-1, keepdims=True) + _EPS) * g


def _proj(h_bf, win_ref, off, width):
    return jnp.dot(h_bf, win_ref[:, off:off + width], preferred_element_type=_F32)


def _dup_halves(val, lo):
    rolled = pltpu.roll(val, _HD, axis=1)
    return jnp.where(lo, val, rolled), jnp.where(lo, rolled, val)


def _band_mask(qi, kj):
    return (kj > qi) & (kj <= qi + _WIN)


def _dot_nt(a, b):
    return lax.dot_general(a, b, (((1,), (1,)), ((), ())), preferred_element_type=_F32)


def _own_layer(ref, first):
    if not first:
        return ref
    if ref.shape[0] > 1:
        ref[1:] = jnp.zeros((ref.shape[0] - 1,) + ref.shape[1:], ref.dtype)
    return ref.at[0]


def _softmax_pv(s, vd, bias, sinks, nq):
    es, rden = [], []
    for j in range(4):
        sj = s[j * nq:(j + 1) * nq] + bias
        m = jnp.maximum(jnp.max(sj, axis=-1, keepdims=True), sinks[j])
        e = jnp.exp(sj - m)
        den = jnp.sum(e, axis=-1, keepdims=True) + jnp.exp(sinks[j] - m)
        es.append(e)
        rden.append(1.0 / den)
    e_all = jnp.concatenate(es, axis=0).astype(_BF)
    o = jnp.dot(e_all, vd, preferred_element_type=_F32)
    return [o[j * nq:(j + 1) * nq] * rden[j] for j in range(4)]


def _mix_and_project(x, h, attn_bf, u_shift2, u_shift1, u, win_ref, convw_ref, wao_ref, wco_ref, wout_ref,
                     gpost_ref):
    attn_b = jnp.dot(attn_bf, wao_ref[...], preferred_element_type=_F32)
    mixed = jax.nn.sigmoid(_proj(h, win_ref, _OFF_GA, _D)) * attn_b
    cw = convw_ref[...]
    z = cw[0:1] * u_shift2
    z = z + cw[1:2] * u_shift1
    z = z + cw[2:3] * u
    bz = (_proj(h, win_ref, _OFF_B, _CONV) * z).astype(_BF)
    conv_b = jnp.dot(bz, wco_ref[...], preferred_element_type=_F32)
    mixed = mixed + jax.nn.sigmoid(_proj(h, win_ref, _OFF_GC, _D)) * conv_b
    mo = jnp.dot(mixed.astype(_BF), wout_ref[...], preferred_element_type=_F32)
    return x + _rms(mo, gpost_ref[...])


def _run_interleaved(*stages):
    stages = list(stages)
    while stages:
        for g in list(stages):
            try:
                next(g)
            except StopIteration:
                stages.remove(g)


def _mlp_stages(x_ref, y_ref, gpre_ref, gpost_ref, wup_ref, wdown_ref):
    x = x_ref[...]
    y_ref[...] = x
    hm = _rms(x, gpre_ref[...]).astype(_BF)
    yield
    acc = None
    for j in range(_DFF // _D):
        hj = jnp.dot(hm, wup_ref[:, j * _D:(j + 1) * _D], preferred_element_type=_F32)
        yield
        hj = jnp.square(jnp.maximum(hj, 0.0)).astype(_BF)
        part = jnp.dot(hj, wdown_ref[j * _D:(j + 1) * _D, :], preferred_element_type=_F32)
        acc = part if acc is None else acc + part
        yield
    y_ref[...] = y_ref[...] + _rms(acc, gpost_ref[...])


_N_WEIGHTS = 6
_STAGE_ROWS = 128


def _prompt_layer_kernel(sink_ref, x_ref, gpre_ref, gpost_ref, gmpre_ref, gmpost_ref, convw_ref,
                         *rest, tm, tps, layer, first, own_cast, n_cast):
    if own_cast:
        w_hbm, rest = rest[:_N_WEIGHTS], rest[_N_WEIGHTS:]
    else:
        weights, rest = rest[:_N_WEIGHTS], rest[_N_WEIGHTS:]
    cast_src, rest = rest[:n_cast], rest[n_cast + (0 if first else 3):]
    y_blk, nk_ref, nv_ref, nc_ref = rest[:4]
    y_ref = y_blk.at[0]
    cast_dst, rest = rest[4:4 + n_cast], rest[4 + n_cast:]
    if own_cast:
        own_out, weights = rest[:_N_WEIGHTS], rest[_N_WEIGHTS:2 * _N_WEIGHTS]
        stage_sem, out_sem = rest[2 * _N_WEIGHTS:2 * _N_WEIGHTS + 2]
        rest = rest[2 * _N_WEIGHTS + 2:]
    win_ref, wao_ref, wco_ref, wout_ref, wup_ref, wdown_ref = weights
    kdup, vdup, q4s, ubuf, attn_scr, x1_scr, klast, vlast, ulast = rest
    j = pl.program_id(0)

    def weight_out_copies():
        return [pltpu.make_async_copy(src, dst.at[0], out_sem.at[k])
                for k, (src, dst) in enumerate(zip(weights, own_out))]

    def load_own_weights():
        stage_bufs = (x1_scr, y_blk)
        stage_cols = x1_scr.shape[1]
        per_buf = x1_scr.shape[0] // _STAGE_ROWS
        n_slots = per_buf * len(stage_bufs)
        blocks = [(k, r0, c0, min(stage_cols, w_hbm[k].shape[2] - c0))
                  for k in range(_N_WEIGHTS)
                  for r0 in range(0, w_hbm[k].shape[1], _STAGE_ROWS)
                  for c0 in range(0, w_hbm[k].shape[2], stage_cols)]

        def slot_index(t, width):
            slot = t % n_slots
            buf = stage_bufs[slot // per_buf]
            index = (pl.ds((slot % per_buf) * _STAGE_ROWS, _STAGE_ROWS), pl.ds(0, width))
            return buf, ((0,) + index if buf is y_blk else index)

        def copy_in(t):
            k, r0, c0, width = blocks[t]
            buf, index = slot_index(t, width)
            return pltpu.make_async_copy(w_hbm[k].at[layer, pl.ds(r0, _STAGE_ROWS), pl.ds(c0, width)],
                                         buf.at[index], stage_sem.at[t % n_slots])

        for t in range(min(n_slots, len(blocks))):
            copy_in(t).start()
        for t, (k, r0, c0, width) in enumerate(blocks):
            copy_in(t).wait()
            buf, index = slot_index(t, width)
            weights[k][r0:r0 + _STAGE_ROWS, c0:c0 + width] = buf[index].astype(_BF)
            if t + n_slots < len(blocks):
                copy_in(t + n_slots).start()
        for copy in weight_out_copies():
            copy.start()

    def cast_next_layer():
        for src, dst in zip(cast_src, cast_dst):
            dst[...] = src[...].astype(_BF)

    n_tiles = pl.num_programs(0) - 1
    s = j % tps

    def start_sequence():
        @pl.when(s == 0)
        def _():
            kdup[:, 0:_WIN, :] = jnp.zeros((2, _WIN, _LANES), _BF)
            vdup[:, 0:_WIN, :] = jnp.zeros((2, _WIN, _LANES), _BF)
            ubuf[0:_SUB, :] = jnp.zeros((_SUB, _CONV), _F32)

    def end_sequence():
        @pl.when(s == tps - 1)
        def _():
            _own_layer(nk_ref, first)[...] = klast[...]
            _own_layer(nv_ref, first)[...] = vlast[...]
            _own_layer(nc_ref, first)[...] = ulast[_SUB - 2:_SUB, :]

    def mlp(src):
        return _mlp_stages(src, y_ref, gmpre_ref, gmpost_ref, wup_ref, wdown_ref)

    def mixer():
        return _mixer_prompt_stages(sink_ref, x_ref, gpre_ref, gpost_ref, convw_ref, win_ref, wao_ref, wco_ref,
                                    wout_ref, kdup, vdup, q4s, ubuf, attn_scr, x1_scr, klast, vlast, ulast, s,
                                    tm=tm, layer=layer)

    @pl.when(j == 0)
    def _():
        if own_cast:
            load_own_weights()
        start_sequence()
        cast_next_layer()
        _run_interleaved(mixer())
        end_sequence()

    @pl.when((j > 0) & (j < n_tiles))
    def _():
        start_sequence()
        mlp_stages, mixer_stages = mlp(x1_scr), mixer()
        for _ in range(2 * (_DFF // _D) - 1):
            next(mlp_stages)
        cast_next_layer()
        next(mixer_stages)
        next(mlp_stages)
        next(mixer_stages)
        _run_interleaved(mlp_stages, mixer_stages)
        end_sequence()

    @pl.when(j == n_tiles)
    def _():
        cast_next_layer()
        _run_interleaved(mlp(x1_scr))
        if own_cast:
            for copy in weight_out_copies():
                copy.wait()


def _mixer_prompt_stages(sink_ref, x_ref, gpre_ref, gpost_ref, convw_ref, win_ref, wao_ref, wco_ref, wout_ref,
                         kdup, vdup, q4s, ubuf, attn_scr, x1_scr, klast, vlast, ulast, s, *, tm, layer):
    x = x_ref[...]
    h = _rms(x, gpre_ref[...]).astype(_BF)
    lo = lax.broadcasted_iota(jnp.int32, (tm, _LANES), 1) < _HD

    hr = tm // 2
    kv = jnp.concatenate([_proj(h[:hr], win_ref, _OFF_KV, 2 * _KVD), _proj(h[hr:], win_ref, _OFF_KV, 2 * _KVD)],
                         axis=0)
    k = kv[:, :_KVD]
    v = kv[:, _KVD:]
    for val, dst in ((k, kdup), (v, vdup)):
        d0, d1 = _dup_halves(val, lo)
        dst[0, _WIN:_WIN + tm, :] = d0.astype(_BF)
        dst[1, _WIN:_WIN + tm, :] = d1.astype(_BF)

    q = _proj(h, win_ref, _OFF_Q, _ATT) * (_HD ** -0.5)
    for c in range(4):
        qc = q[:, c * _LANES:(c + 1) * _LANES]
        q4s[c, 0] = jnp.where(lo, qc, 0.0).astype(_BF)
        q4s[c, 1] = jnp.where(lo, 0.0, qc).astype(_BF)

    qi = lax.broadcasted_iota(jnp.int32, (_WIN, 2 * _WIN), 0)
    kj = lax.broadcasted_iota(jnp.int32, (_WIN, 2 * _WIN), 1)
    band = _band_mask(qi, kj)
    neg = jnp.float32(-jnp.inf)
    bias = jnp.where(band, 0.0, neg)
    bias_first = jnp.where(band & ((kj >= _WIN) | (s > 0)), 0.0, neg)
    lo_q = lax.broadcasted_iota(jnp.int32, (_WIN, _LANES), 1) < _HD

    def scores_of(qb):
        r0 = qb * _WIN
        out = []
        for g in range(2):
            q4 = jnp.concatenate([q4s[2 * g, 0, r0:r0 + _WIN, :], q4s[2 * g, 1, r0:r0 + _WIN, :],
                                  q4s[2 * g + 1, 0, r0:r0 + _WIN, :], q4s[2 * g + 1, 1, r0:r0 + _WIN, :]], axis=0)
            out.append(_dot_nt(q4, kdup[g, r0:r0 + 2 * _WIN, :]))
        return out

    def finish_block(qb, scores):
        r0 = qb * _WIN
        b = bias_first if qb == 0 else bias
        for g in range(2):
            sinks = [sink_ref[layer, 4 * g + j] for j in range(4)]
            o = _softmax_pv(scores[g], vdup[g, r0:r0 + 2 * _WIN, :], b, sinks, _WIN)
            attn_scr[r0:r0 + _WIN, (2 * g) * _LANES:(2 * g + 1) * _LANES] = jnp.where(lo_q, o[0], o[1]).astype(_BF)
            attn_scr[r0:r0 + _WIN, (2 * g + 1) * _LANES:(2 * g + 2) * _LANES] = (
                jnp.where(lo_q, o[2], o[3]).astype(_BF))

    def conv_input():
        u = _proj(h, win_ref, _OFF_C, _CONV) * _proj(h, win_ref, _OFF_U, _CONV)
        ubuf[_SUB:_SUB + tm, :] = u
        return u

    fillers = [conv_input,
               lambda: jax.nn.sigmoid(_proj(h, win_ref, _OFF_GA, _D)),
               lambda: jax.nn.sigmoid(_proj(h, win_ref, _OFF_GC, _D)),
               lambda: _proj(h, win_ref, _OFF_B, _CONV)]
    filled, pending = [], []
    for qb in range(tm // _WIN):
        pending.append((qb, scores_of(qb)))
        if qb < len(fillers):
            filled.append(fillers[qb]())
        if len(pending) > 1:
            finish_block(*pending.pop(0))
    filled += [f() for f in fillers[len(filled):]]
    u, gate_attn, gate_conv, b_gate = filled

    cw = convw_ref[...]
    z = cw[0:1] * ubuf[_SUB - 2:_SUB - 2 + tm, :]
    z = z + cw[1:2] * ubuf[_SUB - 1:_SUB - 1 + tm, :]
    z = z + cw[2:3] * u
    conv_b = jnp.dot((b_gate * z).astype(_BF), wco_ref[...], preferred_element_type=_F32)
    for block in pending:
        finish_block(*block)

    kdup[:, 0:_WIN, :] = kdup[:, tm:tm + _WIN, :]
    vdup[:, 0:_WIN, :] = vdup[:, tm:tm + _WIN, :]

    mixed = gate_attn * jnp.dot(attn_scr[...], wao_ref[...], preferred_element_type=_F32)
    mixed = (mixed + gate_conv * conv_b).astype(_BF)
    yield
    mo = jnp.dot(mixed, wout_ref[...], preferred_element_type=_F32)
    yield
    x1_scr[...] = x + _rms(mo, gpost_ref[...])
    ubuf[0:_SUB, :] = ubuf[tm:tm + _SUB, :]
    klast[...] = k[tm - _WIN:tm, :]
    vlast[...] = v[tm - _WIN:tm, :]
    ulast[...] = u[tm - _SUB:tm, :]


def _mixer_sample_kernel(sink_ref, x_ref, gpre_ref, gpost_ref, convw_ref, ck_ref, cv_ref, st_ref,
                         win_ref, wao_ref, wco_ref, wout_ref, *rest, sb, t, layer, first):
    y_ref, nk_ref, nv_ref, nc_ref, ubuf = rest[0 if first else 3:]
    nk_ref, nv_ref, nc_ref = (_own_layer(r, first) for r in (nk_ref, nv_ref, nc_ref))
    tr = sb * t
    rows = _NH * t
    x = x_ref[...]
    h = _rms(x, gpre_ref[...]).astype(_BF)
    lo = lax.broadcasted_iota(jnp.int32, (tr, _LANES), 1) < _HD

    kv = _proj(h, win_ref, _OFF_KV, 2 * _KVD)
    k = kv[:, :_KVD]
    v = kv[:, _KVD:]
    q = _proj(h, win_ref, _OFF_Q, _ATT) * (_HD ** -0.5)
    pieces = []
    for hd in range(_NH):
        qc = q[:, (hd // 2) * _LANES:(hd // 2 + 1) * _LANES]
        if hd % 2 != hd // (_NH // 2):
            qc = pltpu.roll(qc, _HD, axis=1)
        piece = jnp.where(lo, qc, 0.0) if hd < _NH // 2 else jnp.where(lo, 0.0, qc)
        pieces.append(piece.reshape(sb, t, _LANES))
    q3 = jnp.concatenate(pieces, axis=1).astype(_BF)

    pad = jnp.zeros((sb, _BF_ROWS - t, _LANES), _F32)
    k_new = jnp.concatenate([k.reshape(sb, t, _KVD), pad], axis=1).astype(_BF)
    v_new = jnp.concatenate([v.reshape(sb, t, _KVD), pad], axis=1).astype(_BF)

    s_old = jnp.einsum('bqf,bfp->bqp', q3, ck_ref[...].astype(_BF), preferred_element_type=_F32)
    s_new = jnp.einsum('bqf,btf->bqt', q3, k_new, preferred_element_type=_F32)
    neg = jnp.float32(-jnp.inf)
    qi_old = lax.broadcasted_iota(jnp.int32, (rows, _WIN), 0) & (t - 1)
    kj_old = lax.broadcasted_iota(jnp.int32, (rows, _WIN), 1)
    qi_new = lax.broadcasted_iota(jnp.int32, (rows, _BF_ROWS), 0) & (t - 1)
    kj_new = lax.broadcasted_iota(jnp.int32, (rows, _BF_ROWS), 1) + _WIN
    s_old = s_old + jnp.where(_band_mask(qi_old, kj_old), 0.0, neg)[None]
    s_new = s_new + jnp.where(_band_mask(qi_new, kj_new), 0.0, neg)[None]
    sink_col = jnp.concatenate([jnp.full((t, 1), sink_ref[layer, hd], _F32) for hd in range(_NH)], axis=0)[None]
    m = jnp.maximum(jnp.maximum(jnp.max(s_old, axis=-1, keepdims=True), jnp.max(s_new, axis=-1, keepdims=True)),
                    sink_col)
    e_old = jnp.exp(s_old - m)
    e_new = jnp.exp(s_new - m)
    rden = 1.0 / (jnp.sum(e_old, axis=-1, keepdims=True) + jnp.sum(e_new, axis=-1, keepdims=True)
                  + jnp.exp(sink_col - m))
    o = (jnp.einsum('bqp,bfp->bqf', e_old.astype(_BF), cv_ref[...].astype(_BF), preferred_element_type=_F32)
         + jnp.einsum('bqt,btf->bqf', e_new.astype(_BF), v_new, preferred_element_type=_F32)) * rden
    chunks = []
    for c in range(_NH // 2):
        halves = []
        for hd in (2 * c, 2 * c + 1):
            oh = o[:, hd * t:(hd + 1) * t, :].reshape(tr, _LANES)
            if hd % 2 != hd // (_NH // 2):
                oh = pltpu.roll(oh, _HD, axis=1)
            halves.append(oh)
        chunks.append(jnp.where(lo, halves[0], halves[1]))
    attn = jnp.concatenate(chunks, axis=1)

    k_t = k.T
    v_t = v.T
    keep = lax.broadcasted_iota(jnp.int32, (_KVD, _WIN), 1) < _WIN - t
    per_chunk = _LANES // t
    for b in range(sb):
        c0 = (b // per_chunk) * _LANES
        shift_new = (_WIN - t - (b % per_chunk) * t) % _LANES
        for src_ref, new_t, dst_ref in ((ck_ref, k_t, nk_ref), (cv_ref, v_t, nv_ref)):
            old = pltpu.roll(src_ref[b], _WIN - t, axis=1)
            new = new_t[:, c0:c0 + _LANES]
            if shift_new:
                new = pltpu.roll(new, shift_new, axis=1)
            dst_ref[b] = jnp.where(keep, old, new)

    u = _proj(h, win_ref, _OFF_C, _CONV) * _proj(h, win_ref, _OFF_U, _CONV)
    u3 = u.reshape(sb, t, _CONV)
    ubuf[:, _SUB - 2:_SUB, :] = st_ref[...]
    ubuf[:, _SUB:_SUB + t, :] = u3
    us2 = ubuf[:, _SUB - 2:_SUB - 2 + t, :].reshape(tr, _CONV)
    us1 = ubuf[:, _SUB - 1:_SUB - 1 + t, :].reshape(tr, _CONV)
    nc_ref[...] = u3[:, t - 2:t, :]
    y_ref[...] = _mix_and_project(x, h, attn.astype(_BF), us2, us1, u,
                                  win_ref, convw_ref, wao_ref, wco_ref, wout_ref, gpost_ref)


def _mlp_kernel(x_ref, gpre_ref, gpost_ref, wup_ref, wdown_ref, y_ref):
    _run_interleaved(_mlp_stages(x_ref, y_ref, gpre_ref, gpost_ref, wup_ref, wdown_ref))


def _layer_spec(shape, layer, **kwargs):
    return pl.BlockSpec((None,) + shape, lambda *_: (layer,) + (0,) * len(shape), **kwargs)


def _weight_spec(w, slot):
    return pl.BlockSpec((None,) + w.shape[1:], lambda *_: (slot, 0, 0), pipeline_mode=pl.Buffered(1))


_SMEM_SPEC = pl.BlockSpec(memory_space=pltpu.SMEM)
_ANY_SPEC = pl.BlockSpec(memory_space=pl.ANY)


def _stacked_out_spec(depth, layer, first, block, index_map):
    lead, l0 = (depth, 0) if first else (None, layer)
    return pl.BlockSpec((lead,) + block, lambda *idx: (l0,) + index_map(*idx))


def _alias_args(prev, n_inputs):
    if prev is None:
        return [], [], {}
    return list(prev), [_ANY_SPEC] * len(prev), {n_inputs + i: 1 + i for i in range(len(prev))}


def _prompt_layer(x, layer, prev, sinks, gpre, gpost, gmpre, gmpost, convw, weights, weights_f32):
    n, seq, _ = x.shape
    depth = sinks.shape[0]
    tm = _PROMPT_TM
    first = prev is None
    own_cast = weights is None
    has_next = layer + 1 < depth
    assert seq % tm == 0 and tm % _WIN == 0 and seq >= _WIN
    tps = seq // tm
    n_tiles = n * tps

    def tile_of(t):
        return (t // tps, t % tps, 0)

    mixer_j = lambda i: jnp.minimum(i, n_tiles - 1)
    mixer_tile = lambda i: tile_of(mixer_j(i))
    mlp_tile = lambda i: tile_of(jnp.maximum(i - 1, 0))
    per_b = lambda i: (mixer_j(i) // tps, 0, 0)

    cast_in, cast_in_specs, cast_out_specs, cast_out_shapes = [], [], [], []
    for w in (weights_f32 if has_next else ()):
        rows, cols = w.shape[1:]
        assert rows % (n_tiles * _BF_ROWS) == 0
        chunk = rows // n_tiles
        cast_in.append(w)
        cast_in_specs.append(pl.BlockSpec((None, chunk, cols), lambda i: (layer + 1, mixer_j(i), 0)))
        cast_out_specs.append(pl.BlockSpec((None, chunk, cols), lambda i: (0, mixer_j(i), 0)))
        cast_out_shapes.append(jax.ShapeDtypeStruct((1, rows, cols), _BF))

    own_in, own_out_shapes, own_scratch = [], [], []
    if own_cast:
        stage_cols = _D
        for w in weights_f32:
            rows, cols = w.shape[1:]
            assert rows % _STAGE_ROWS == 0 and cols % _LANES == 0 and (cols % stage_cols) % _LANES == 0
            own_in.append(w)
            own_out_shapes.append(jax.ShapeDtypeStruct((1, rows, cols), _BF))
            own_scratch.append(pltpu.VMEM((rows, cols), _BF))
        assert tm % _STAGE_ROWS == 0
        own_scratch += [pltpu.SemaphoreType.DMA((2 * tm // _STAGE_ROWS,)), pltpu.SemaphoreType.DMA((_N_WEIGHTS,))]

    weight_in = [] if own_cast else list(weights[0])
    weight_specs = [] if own_cast else [_weight_spec(w, weights[1]) for w in weights[0]]
    inputs = [sinks, x, gpre, gpost, gmpre, gmpost, convw, *weight_in, *own_in, *cast_in]
    alias_in, alias_specs, aliases = _alias_args(prev, len(inputs))
    out = pl.pallas_call(
        functools.partial(_prompt_layer_kernel, tm=tm, tps=tps, layer=layer, first=first, own_cast=own_cast,
                          n_cast=len(cast_in)),
        grid=(n_tiles + 1,),
        in_specs=[_SMEM_SPEC,
                  pl.BlockSpec((None, tm, _D), mixer_tile)]
                 + [_layer_spec((1, _D), layer)] * 4 + [_layer_spec((3, _CONV), layer)]
                 + weight_specs + [_ANY_SPEC] * len(own_in) + cast_in_specs + alias_specs,
        out_specs=[pl.BlockSpec((1, tm, _D), mlp_tile),
                   _stacked_out_spec(depth, layer, first, (None, _WIN, _KVD), per_b),
                   _stacked_out_spec(depth, layer, first, (None, _WIN, _KVD), per_b),
                   _stacked_out_spec(depth, layer, first, (None, 2, _CONV), per_b)]
                  + cast_out_specs + [_ANY_SPEC] * len(own_out_shapes),
        out_shape=[jax.ShapeDtypeStruct((n, seq, _D), _F32),
                   jax.ShapeDtypeStruct((depth, n, _WIN, _KVD), _F32),
                   jax.ShapeDtypeStruct((depth, n, _WIN, _KVD), _F32),
                   jax.ShapeDtypeStruct((depth, n, 2, _CONV), _F32)] + cast_out_shapes + own_out_shapes,
        scratch_shapes=own_scratch + [
                        pltpu.VMEM((2, _WIN + tm, _LANES), _BF),
                        pltpu.VMEM((2, _WIN + tm, _LANES), _BF),
                        pltpu.VMEM((4, 2, tm, _LANES), _BF),
                        pltpu.VMEM((_SUB + tm, _CONV), _F32),
                        pltpu.VMEM((tm, _ATT), _BF),
                        pltpu.VMEM((tm, _D), _F32),
                        pltpu.VMEM((_WIN, _KVD), _F32),
                        pltpu.VMEM((_WIN, _KVD), _F32),
                        pltpu.VMEM((_SUB, _CONV), _F32)],
        input_output_aliases=aliases,
        compiler_params=pltpu.CompilerParams(dimension_semantics=("arbitrary",),
                                             vmem_limit_bytes=_VMEM_LIMIT),
        name="prompt_layer",
    )(*inputs, *alias_in)
    n_next = len(cast_in)
    return out[0], tuple(out[1:4]), tuple(out[4:4 + n_next]), tuple(out[4 + n_next:])


def _mixer_sample(x, layer, prev, ck, cv, st, sinks, gpre, gpost, convw, slot, win, wao, wco, wout):
    n, t, _ = x.shape
    depth = sinks.shape[0]
    sb = _SAMPLE_SB
    first = prev is None
    assert n % sb == 0 and t == _SUB and ck.shape == (depth, n, _KVD, _WIN)
    tr = sb * t
    rows = lambda i: (i, 0)
    seqs = lambda i: (i, 0, 0)
    cache_spec = pl.BlockSpec((None, sb, _KVD, _WIN), lambda i: (layer, i, 0, 0))
    state_spec = pl.BlockSpec((None, sb, 2, _CONV), lambda i: (layer, i, 0, 0))
    inputs = [sinks, x.reshape(n * t, _D), gpre, gpost, convw, ck, cv, st, win, wao, wco, wout]
    alias_in, alias_specs, aliases = _alias_args(prev, len(inputs))
    out = pl.pallas_call(
        functools.partial(_mixer_sample_kernel, sb=sb, t=t, layer=layer, first=first),
        grid=(n // sb,),
        in_specs=[_SMEM_SPEC,
                  pl.BlockSpec((tr, _D), rows),
                  _layer_spec((1, _D), layer), _layer_spec((1, _D), layer), _layer_spec((3, _CONV), layer),
                  cache_spec, cache_spec, state_spec]
                 + [_weight_spec(w, slot) for w in (win, wao, wco, wout)] + alias_specs,
        out_specs=[pl.BlockSpec((tr, _D), rows),
                   _stacked_out_spec(depth, layer, first, (sb, _KVD, _WIN), seqs),
                   _stacked_out_spec(depth, layer, first, (sb, _KVD, _WIN), seqs),
                   _stacked_out_spec(depth, layer, first, (sb, 2, _CONV), seqs)],
        out_shape=[jax.ShapeDtypeStruct((n * t, _D), _F32),
                   jax.ShapeDtypeStruct((depth, n, _KVD, _WIN), _F32),
                   jax.ShapeDtypeStruct((depth, n, _KVD, _WIN), _F32),
                   jax.ShapeDtypeStruct((depth, n, 2, _CONV), _F32)],
        scratch_shapes=[pltpu.VMEM((sb, 2 * _SUB, _CONV), _F32)],
        input_output_aliases=aliases,
        compiler_params=pltpu.CompilerParams(dimension_semantics=("arbitrary",),
                                             vmem_limit_bytes=_VMEM_LIMIT),
        name="mixer_sample",
    )(*inputs, *alias_in)
    return out[0].reshape(n, t, _D), tuple(out[1:])


def _mlp(x, layer, gpre, gpost, slot, wup, wdown):
    shape = x.shape
    x2 = x.reshape(-1, _D)
    nt = x2.shape[0]
    tm = _MLP_TM
    assert nt % tm == 0
    rows = lambda i: (i, 0)
    y = pl.pallas_call(
        _mlp_kernel,
        grid=(nt // tm,),
        in_specs=[pl.BlockSpec((tm, _D), rows), _layer_spec((1, _D), layer), _layer_spec((1, _D), layer),
                  _weight_spec(wup, slot), _weight_spec(wdown, slot)],
        out_specs=pl.BlockSpec((tm, _D), rows),
        out_shape=jax.ShapeDtypeStruct((nt, _D), _F32),
        compiler_params=pltpu.CompilerParams(dimension_semantics=("parallel",),
                                             vmem_limit_bytes=_VMEM_LIMIT),
        name="mlp",
    )(x2, gpre, gpost, wup, wdown)
    return y.reshape(shape)


def kernel(x_prompt, x_sample, cache_k, cache_v, state_conv, g_mix_pre, g_mix_post, g_mlp_pre, g_mlp_post,
           w_in, attn_sinks, conv_w, w_attn_o, w_conv_o, w_out, w_up, w_down):
    depth = w_in.shape[0]
    n_dec = x_sample.shape[0]
    feature_major = lambda c: jnp.transpose(c, (0, 1, 3, 4, 2)).reshape(depth, n_dec, _KVD, _WIN)
    ck, cv = feature_major(cache_k), feature_major(cache_v)
    weights_f32 = (w_in, w_attn_o, w_conv_o, w_out, w_up, w_down)
    gains = [g.reshape(depth, 1, _D) for g in (g_mix_pre, g_mix_post, g_mlp_pre, g_mlp_post)]

    weights = None
    yp, ys = x_prompt, x_sample
    caches_p = caches_s = None
    for l in range(depth):
        yp, caches_p, next_cast, own = _prompt_layer(yp, l, caches_p, attn_sinks, *gains, conv_w, weights,
                                                     weights_f32)
        if weights is None:
            weights = (own, 0)
        ys, caches_s = _mixer_sample(ys, l, caches_s, ck, cv, state_conv, attn_sinks, gains[0], gains[1], conv_w,
                                     weights[1], *weights[0][:4])
        ys = _mlp(ys, l, gains[2], gains[3], weights[1], *weights[0][4:])
        weights = (next_cast, 0)

    kv5 = lambda a: a.reshape(a.shape[:3] + (2, _HD))
    position_major = lambda a: jnp.transpose(a.reshape(depth, n_dec, 2, _HD, _WIN), (0, 1, 4, 2, 3))
    return (yp, ys, kv5(caches_p[0]), kv5(caches_p[1]), caches_p[2],
            position_major(caches_s[0]), position_major(caches_s[1]), caches_s[2])
```

```python
import functools

import jax
import jax.numpy as jnp
from jax import lax
from jax.experimental import pallas as pl
from jax.experimental.pallas import tpu as pltpu

_D = 1024
_HD = 64
_NH = 8
_ATT = _NH * _HD
_KVD = 128
_CONV = 512
_DFF = 4096
_WIN = 128
_EPS = 1e-6
_LANES = 128
_SUB = 8
_BF_ROWS = 16

_OFF_Q = 0
_OFF_KV = _ATT
_OFF_B = _OFF_KV + 2 * _KVD
_OFF_C = _OFF_B + _CONV
_OFF_U = _OFF_C + _CONV
_OFF_GA = _OFF_U + _CONV
_OFF_GC = _OFF_GA + _D
_IN_DIM = _OFF_GC + _D

_PROMPT_TM = 512
_SAMPLE_SB = 32
_MLP_TM = 512
_VMEM_LIMIT = 62 * 1024 * 1024

_BF = jnp.bfloat16
_F32 = jnp.float32


def _rms(x, g):
    return x * lax.rsqrt(jnp.mean(x * x, axis=-1, keepdims=True) + _EPS) * g


def _gain_row(g_ref, layer):
    return g_ref.at[pl.ds(layer, 1)]


def _conv_taps(convw_ref, layer):
    return jnp.concatenate([convw_ref[k, pl.ds(layer, 1), :] for k in range(convw_ref.shape[0])], axis=0)


def _proj(h_bf, win_ref, off, width):
    return jnp.dot(h_bf, win_ref[:, off:off + width], preferred_element_type=_F32)


def _dup_halves(val, lo):
    rolled = pltpu.roll(val, _HD, axis=1)
    return jnp.where(lo, val, rolled), jnp.where(lo, rolled, val)


def _band_mask(qi, kj):
    return (kj > qi) & (kj <= qi + _WIN)


def _dot_nt(a, b):
    return lax.dot_general(a, b, (((1,), (1,)), ((), ())), preferred_element_type=_F32)


def _own_layer(ref, first):
    if not first:
        return ref
    if ref.shape[0] > 1:
        ref[1:] = jnp.zeros((ref.shape[0] - 1,) + ref.shape[1:], ref.dtype)
    return ref.at[0]


def _softmax_pv(s, vd, bias, sinks, nq):
    es, rden = [], []
    for j in range(4):
        sj = s[j * nq:(j + 1) * nq] + bias
        m = jnp.maximum(jnp.max(sj, axis=-1, keepdims=True), sinks[j])
        e = jnp.exp(sj - m)
        den = jnp.sum(e, axis=-1, keepdims=True) + jnp.exp(sinks[j] - m)
        es.append(e)
        rden.append(1.0 / den)
    e_all = jnp.concatenate(es, axis=0).astype(_BF)
    o = jnp.dot(e_all, vd, preferred_element_type=_F32)
    return [o[j * nq:(j + 1) * nq] * rden[j] for j in range(4)]


def _mix_and_project(x, h, attn_bf, u_shift2, u_shift1, u, win_ref, cw, wao_ref, wco_ref, wout_ref,
                     gpost_ref):
    attn_b = jnp.dot(attn_bf, wao_ref[...], preferred_element_type=_F32)
    mixed = jax.nn.sigmoid(_proj(h, win_ref, _OFF_GA, _D)) * attn_b
    z = cw[0:1] * u_shift2
    z = z + cw[1:2] * u_shift1
    z = z + cw[2:3] * u
    bz = (_proj(h, win_ref, _OFF_B, _CONV) * z).astype(_BF)
    conv_b = jnp.dot(bz, wco_ref[...], preferred_element_type=_F32)
    mixed = mixed + jax.nn.sigmoid(_proj(h, win_ref, _OFF_GC, _D)) * conv_b
    mo = jnp.dot(mixed.astype(_BF), wout_ref[...], preferred_element_type=_F32)
    return x + _rms(mo, gpost_ref[...])


def _run_interleaved(*stages):
    stages = list(stages)
    while stages:
        for g in list(stages):
            try:
                next(g)
            except StopIteration:
                stages.remove(g)


def _mlp_stages(x_ref, y_ref, gpre_ref, gpost_ref, wup_ref, wdown_ref):
    x = x_ref[...]
    y_ref[...] = x
    hm = _rms(x, gpre_ref[...]).astype(_BF)
    yield
    acc = None
    for j in range(_DFF // _D):
        hj = jnp.dot(hm, wup_ref[:, j * _D:(j + 1) * _D], preferred_element_type=_F32)
        yield
        hj = jnp.square(jnp.maximum(hj, 0.0)).astype(_BF)
        part = jnp.dot(hj, wdown_ref[j * _D:(j + 1) * _D, :], preferred_element_type=_F32)
        acc = part if acc is None else acc + part
        yield
    y_ref[...] = y_ref[...] + _rms(acc, gpost_ref[...])


_N_WEIGHTS = 6
_STAGE_ROWS = 128


def _prompt_layer_kernel(sink_ref, x_ref, gpre_ref, gpost_ref, gmpre_ref, gmpost_ref, convw_ref,
                         *rest, tm, tps, layer, first, own_cast, n_cast):
    if own_cast:
        w_hbm, rest = rest[:_N_WEIGHTS], rest[_N_WEIGHTS:]
    else:
        weights, rest = rest[:_N_WEIGHTS], rest[_N_WEIGHTS:]
    cast_src, rest = rest[:n_cast], rest[n_cast + (0 if first else 3):]
    y_blk, nk_ref, nv_ref, nc_ref = rest[:4]
    y_ref = y_blk.at[0]
    cast_dst, rest = rest[4:4 + n_cast], rest[4 + n_cast:]
    if own_cast:
        own_out, weights = rest[:_N_WEIGHTS], rest[_N_WEIGHTS:2 * _N_WEIGHTS]
        stage_sem, out_sem = rest[2 * _N_WEIGHTS:2 * _N_WEIGHTS + 2]
        rest = rest[2 * _N_WEIGHTS + 2:]
    win_ref, wao_ref, wco_ref, wout_ref, wup_ref, wdown_ref = weights
    kdup, vdup, q4s, ubuf, attn_scr, x1_scr, klast, vlast, ulast = rest
    j = pl.program_id(0)
    gpre_ref, gpost_ref, gmpre_ref, gmpost_ref = (_gain_row(g, layer)
                                                  for g in (gpre_ref, gpost_ref, gmpre_ref, gmpost_ref))

    def weight_out_copies():
        return [pltpu.make_async_copy(src, dst.at[0], out_sem.at[k])
                for k, (src, dst) in enumerate(zip(weights, own_out))]

    def load_own_weights():
        stage_bufs = (x1_scr, y_blk)
        stage_cols = x1_scr.shape[1]
        per_buf = x1_scr.shape[0] // _STAGE_ROWS
        n_slots = per_buf * len(stage_bufs)
        blocks = [(k, r0, c0, min(stage_cols, w_hbm[k].shape[2] - c0))
                  for k in range(_N_WEIGHTS)
                  for r0 in range(0, w_hbm[k].shape[1], _STAGE_ROWS)
                  for c0 in range(0, w_hbm[k].shape[2], stage_cols)]

        def slot_index(t, width):
            slot = t % n_slots
            buf = stage_bufs[slot // per_buf]
            index = (pl.ds((slot % per_buf) * _STAGE_ROWS, _STAGE_ROWS), pl.ds(0, width))
            return buf, ((0,) + index if buf is y_blk else index)

        def copy_in(t):
            k, r0, c0, width = blocks[t]
            buf, index = slot_index(t, width)
            return pltpu.make_async_copy(w_hbm[k].at[layer, pl.ds(r0, _STAGE_ROWS), pl.ds(c0, width)],
                                         buf.at[index], stage_sem.at[t % n_slots])

        for t in range(min(n_slots, len(blocks))):
            copy_in(t).start()
        for t, (k, r0, c0, width) in enumerate(blocks):
            copy_in(t).wait()
            buf, index = slot_index(t, width)
            weights[k][r0:r0 + _STAGE_ROWS, c0:c0 + width] = buf[index].astype(_BF)
            if t + n_slots < len(blocks):
                copy_in(t + n_slots).start()
        for copy in weight_out_copies():
            copy.start()

    def cast_next_layer():
        for src, dst in zip(cast_src, cast_dst):
            dst[...] = src[...].astype(_BF)

    n_tiles = pl.num_programs(0) - 1
    s = j % tps

    def start_sequence():
        @pl.when(s == 0)
        def _():
            kdup[:, 0:_WIN, :] = jnp.zeros((2, _WIN, _LANES), _BF)
            vdup[:, 0:_WIN, :] = jnp.zeros((2, _WIN, _LANES), _BF)
            ubuf[0:_SUB, :] = jnp.zeros((_SUB, _CONV), _F32)

    def end_sequence():
        @pl.when(s == tps - 1)
        def _():
            _own_layer(nk_ref, first)[...] = klast[...].T
            _own_layer(nv_ref, first)[...] = vlast[...].T
            _own_layer(nc_ref, first)[...] = ulast[_SUB - 2:_SUB, :]

    def mlp(src):
        return _mlp_stages(src, y_ref, gmpre_ref, gmpost_ref, wup_ref, wdown_ref)

    def mixer():
        return _mixer_prompt_stages(sink_ref, x_ref, gpre_ref, gpost_ref, convw_ref, win_ref, wao_ref, wco_ref,
                                    wout_ref, kdup, vdup, q4s, ubuf, attn_scr, x1_scr, klast, vlast, ulast, s,
                                    tm=tm, layer=layer)

    @pl.when(j == 0)
    def _():
        if own_cast:
            load_own_weights()
        start_sequence()
        cast_next_layer()
        _run_interleaved(mixer())
        end_sequence()

    @pl.when((j > 0) & (j < n_tiles))
    def _():
        start_sequence()
        mlp_stages, mixer_stages = mlp(x1_scr), mixer()
        for _ in range(2 * (_DFF // _D) - 1):
            next(mlp_stages)
        cast_next_layer()
        next(mixer_stages)
        next(mlp_stages)
        next(mixer_stages)
        _run_interleaved(mlp_stages, mixer_stages)
        end_sequence()

    @pl.when(j == n_tiles)
    def _():
        cast_next_layer()
        _run_interleaved(mlp(x1_scr))
        if own_cast:
            for copy in weight_out_copies():
                copy.wait()


def _mixer_prompt_stages(sink_ref, x_ref, gpre_ref, gpost_ref, convw_ref, win_ref, wao_ref, wco_ref, wout_ref,
                         kdup, vdup, q4s, ubuf, attn_scr, x1_scr, klast, vlast, ulast, s, *, tm, layer):
    x = x_ref[...]
    h = _rms(x, gpre_ref[...]).astype(_BF)
    lo = lax.broadcasted_iota(jnp.int32, (tm, _LANES), 1) < _HD

    hr = tm // 2
    kv = jnp.concatenate([_proj(h[:hr], win_ref, _OFF_KV, 2 * _KVD), _proj(h[hr:], win_ref, _OFF_KV, 2 * _KVD)],
                         axis=0)
    k = kv[:, :_KVD]
    v = kv[:, _KVD:]
    for val, dst in ((k, kdup), (v, vdup)):
        d0, d1 = _dup_halves(val, lo)
        dst[0, _WIN:_WIN + tm, :] = d0.astype(_BF)
        dst[1, _WIN:_WIN + tm, :] = d1.astype(_BF)

    q = _proj(h, win_ref, _OFF_Q, _ATT) * (_HD ** -0.5)
    for c in range(4):
        qc = q[:, c * _LANES:(c + 1) * _LANES]
        q4s[c, 0] = jnp.where(lo, qc, 0.0).astype(_BF)
        q4s[c, 1] = jnp.where(lo, 0.0, qc).astype(_BF)

    qi = lax.broadcasted_iota(jnp.int32, (_WIN, 2 * _WIN), 0)
    kj = lax.broadcasted_iota(jnp.int32, (_WIN, 2 * _WIN), 1)
    band = _band_mask(qi, kj)
    neg = jnp.float32(-jnp.inf)
    bias = jnp.where(band, 0.0, neg)
    bias_first = jnp.where(band & ((kj >= _WIN) | (s > 0)), 0.0, neg)
    lo_q = lax.broadcasted_iota(jnp.int32, (_WIN, _LANES), 1) < _HD

    def scores_of(qb):
        r0 = qb * _WIN
        out = []
        for g in range(2):
            q4 = jnp.concatenate([q4s[2 * g, 0, r0:r0 + _WIN, :], q4s[2 * g, 1, r0:r0 + _WIN, :],
                                  q4s[2 * g + 1, 0, r0:r0 + _WIN, :], q4s[2 * g + 1, 1, r0:r0 + _WIN, :]], axis=0)
            out.append(_dot_nt(q4, kdup[g, r0:r0 + 2 * _WIN, :]))
        return out

    def finish_block(qb, scores):
        r0 = qb * _WIN
        b = bias_first if qb == 0 else bias
        for g in range(2):
            sinks = [sink_ref[layer, 4 * g + j] for j in range(4)]
            o = _softmax_pv(scores[g], vdup[g, r0:r0 + 2 * _WIN, :], b, sinks, _WIN)
            attn_scr[r0:r0 + _WIN, (2 * g) * _LANES:(2 * g + 1) * _LANES] = jnp.where(lo_q, o[0], o[1]).astype(_BF)
            attn_scr[r0:r0 + _WIN, (2 * g + 1) * _LANES:(2 * g + 2) * _LANES] = (
                jnp.where(lo_q, o[2], o[3]).astype(_BF))

    def conv_input():
        u = _proj(h, win_ref, _OFF_C, _CONV) * _proj(h, win_ref, _OFF_U, _CONV)
        ubuf[_SUB:_SUB + tm, :] = u
        return u

    fillers = [conv_input,
               lambda: jax.nn.sigmoid(_proj(h, win_ref, _OFF_GA, _D)),
               lambda: jax.nn.sigmoid(_proj(h, win_ref, _OFF_GC, _D)),
               lambda: _proj(h, win_ref, _OFF_B, _CONV)]
    filled, pending = [], []
    for qb in range(tm // _WIN):
        pending.append((qb, scores_of(qb)))
        if qb < len(fillers):
            filled.append(fillers[qb]())
        if len(pending) > 1:
            finish_block(*pending.pop(0))
    filled += [f() for f in fillers[len(filled):]]
    u, gate_attn, gate_conv, b_gate = filled

    cw = _conv_taps(convw_ref, layer)
    z = cw[0:1] * ubuf[_SUB - 2:_SUB - 2 + tm, :]
    z = z + cw[1:2] * ubuf[_SUB - 1:_SUB - 1 + tm, :]
    z = z + cw[2:3] * u
    conv_b = jnp.dot((b_gate * z).astype(_BF), wco_ref[...], preferred_element_type=_F32)
    for block in pending:
        finish_block(*block)

    kdup[:, 0:_WIN, :] = kdup[:, tm:tm + _WIN, :]
    vdup[:, 0:_WIN, :] = vdup[:, tm:tm + _WIN, :]

    mixed = gate_attn * jnp.dot(attn_scr[...], wao_ref[...], preferred_element_type=_F32)
    mixed = (mixed + gate_conv * conv_b).astype(_BF)
    yield
    mo = jnp.dot(mixed, wout_ref[...], preferred_element_type=_F32)
    yield
    x1_scr[...] = x + _rms(mo, gpost_ref[...])
    ubuf[0:_SUB, :] = ubuf[tm:tm + _SUB, :]
    klast[...] = k[tm - _WIN:tm, :]
    vlast[...] = v[tm - _WIN:tm, :]
    ulast[...] = u[tm - _SUB:tm, :]


def _mixer_sample_kernel(sink_ref, x_ref, gpre_ref, gpost_ref, convw_ref, ck_ref, cv_ref, st_ref,
                         win_ref, wao_ref, wco_ref, wout_ref, *rest, sb, t, layer, first):
    y_ref, nk_ref, nv_ref, nc_ref, ubuf = rest[0 if first else 3:]
    nk_ref, nv_ref, nc_ref = (_own_layer(r, first) for r in (nk_ref, nv_ref, nc_ref))
    gpre_ref, gpost_ref = _gain_row(gpre_ref, layer), _gain_row(gpost_ref, layer)
    tr = sb * t
    rows = _NH * t
    x = x_ref[...]
    h = _rms(x, gpre_ref[...]).astype(_BF)
    lo = lax.broadcasted_iota(jnp.int32, (tr, _LANES), 1) < _HD

    kv = _proj(h, win_ref, _OFF_KV, 2 * _KVD)
    k = kv[:, :_KVD]
    v = kv[:, _KVD:]
    q = _proj(h, win_ref, _OFF_Q, _ATT) * (_HD ** -0.5)
    pieces = []
    for hd in range(_NH):
        qc = q[:, (hd // 2) * _LANES:(hd // 2 + 1) * _LANES]
        if hd % 2 != hd // (_NH // 2):
            qc = pltpu.roll(qc, _HD, axis=1)
        piece = jnp.where(lo, qc, 0.0) if hd < _NH // 2 else jnp.where(lo, 0.0, qc)
        pieces.append(piece.reshape(sb, t, _LANES))
    q3 = jnp.concatenate(pieces, axis=1).astype(_BF)

    pad = jnp.zeros((sb, _BF_ROWS - t, _LANES), _F32)
    k_new = jnp.concatenate([k.reshape(sb, t, _KVD), pad], axis=1).astype(_BF)
    v_new = jnp.concatenate([v.reshape(sb, t, _KVD), pad], axis=1).astype(_BF)

    s_old = jnp.einsum('bqf,bfp->bqp', q3, ck_ref[...].astype(_BF), preferred_element_type=_F32)
    s_new = jnp.einsum('bqf,btf->bqt', q3, k_new, preferred_element_type=_F32)
    neg = jnp.float32(-jnp.inf)
    qi_old = lax.broadcasted_iota(jnp.int32, (rows, _WIN), 0) & (t - 1)
    kj_old = lax.broadcasted_iota(jnp.int32, (rows, _WIN), 1)
    qi_new = lax.broadcasted_iota(jnp.int32, (rows, _BF_ROWS), 0) & (t - 1)
    kj_new = lax.broadcasted_iota(jnp.int32, (rows, _BF_ROWS), 1) + _WIN
    s_old = s_old + jnp.where(_band_mask(qi_old, kj_old), 0.0, neg)[None]
    s_new = s_new + jnp.where(_band_mask(qi_new, kj_new), 0.0, neg)[None]
    sink_col = jnp.concatenate([jnp.full((t, 1), sink_ref[layer, hd], _F32) for hd in range(_NH)], axis=0)[None]
    m = jnp.maximum(jnp.maximum(jnp.max(s_old, axis=-1, keepdims=True), jnp.max(s_new, axis=-1, keepdims=True)),
                    sink_col)
    e_old = jnp.exp(s_old - m)
    e_new = jnp.exp(s_new - m)
    rden = 1.0 / (jnp.sum(e_old, axis=-1, keepdims=True) + jnp.sum(e_new, axis=-1, keepdims=True)
                  + jnp.exp(sink_col - m))
    o = (jnp.einsum('bqp,bfp->bqf', e_old.astype(_BF), cv_ref[...].astype(_BF), preferred_element_type=_F32)
         + jnp.einsum('bqt,btf->bqf', e_new.astype(_BF), v_new, preferred_element_type=_F32)) * rden
    chunks = []
    for c in range(_NH // 2):
        halves = []
        for hd in (2 * c, 2 * c + 1):
            oh = o[:, hd * t:(hd + 1) * t, :].reshape(tr, _LANES)
            if hd % 2 != hd // (_NH // 2):
                oh = pltpu.roll(oh, _HD, axis=1)
            halves.append(oh)
        chunks.append(jnp.where(lo, halves[0], halves[1]))
    attn = jnp.concatenate(chunks, axis=1)

    k_t = k.T
    v_t = v.T
    keep = lax.broadcasted_iota(jnp.int32, (_KVD, _WIN), 1) < _WIN - t
    per_chunk = _LANES // t
    for b in range(sb):
        c0 = (b // per_chunk) * _LANES
        shift_new = (_WIN - t - (b % per_chunk) * t) % _LANES
        for src_ref, new_t, dst_ref in ((ck_ref, k_t, nk_ref), (cv_ref, v_t, nv_ref)):
            old = pltpu.roll(src_ref[b], _WIN - t, axis=1)
            new = new_t[:, c0:c0 + _LANES]
            if shift_new:
                new = pltpu.roll(new, shift_new, axis=1)
            dst_ref[b] = jnp.where(keep, old, new)

    u = _proj(h, win_ref, _OFF_C, _CONV) * _proj(h, win_ref, _OFF_U, _CONV)
    u3 = u.reshape(sb, t, _CONV)
    ubuf[:, _SUB - 2:_SUB, :] = st_ref[...]
    ubuf[:, _SUB:_SUB + t, :] = u3
    us2 = ubuf[:, _SUB - 2:_SUB - 2 + t, :].reshape(tr, _CONV)
    us1 = ubuf[:, _SUB - 1:_SUB - 1 + t, :].reshape(tr, _CONV)
    nc_ref[...] = u3[:, t - 2:t, :]
    y_ref[...] = _mix_and_project(x, h, attn.astype(_BF), us2, us1, u,
                                  win_ref, _conv_taps(convw_ref, layer), wao_ref, wco_ref, wout_ref, gpost_ref)


def _mlp_kernel(x_ref, gpre_ref, gpost_ref, wup_ref, wdown_ref, y_ref, *, layer):
    _run_interleaved(_mlp_stages(x_ref, y_ref, _gain_row(gpre_ref, layer), _gain_row(gpost_ref, layer),
                                 wup_ref, wdown_ref))


def _whole_spec(a):
    return pl.BlockSpec(a.shape, lambda *_: (0,) * a.ndim)


def _weight_spec(w, slot):
    return pl.BlockSpec((None,) + w.shape[1:], lambda *_: (slot, 0, 0), pipeline_mode=pl.Buffered(1))


_SMEM_SPEC = pl.BlockSpec(memory_space=pltpu.SMEM)
_ANY_SPEC = pl.BlockSpec(memory_space=pl.ANY)


def _stacked_out_spec(depth, layer, first, block, index_map):
    lead, l0 = (depth, 0) if first else (None, layer)
    return pl.BlockSpec((lead,) + block, lambda *idx: (l0,) + index_map(*idx))


def _alias_args(prev, n_inputs):
    if prev is None:
        return [], [], {}
    return list(prev), [_ANY_SPEC] * len(prev), {n_inputs + i: 1 + i for i in range(len(prev))}


def _prompt_layer(x, layer, prev, sinks, gpre, gpost, gmpre, gmpost, convw, weights, weights_f32):
    n, seq, _ = x.shape
    depth = sinks.shape[0]
    tm = _PROMPT_TM
    first = prev is None
    own_cast = weights is None
    has_next = layer + 1 < depth
    assert seq % tm == 0 and tm % _WIN == 0 and seq >= _WIN
    tps = seq // tm
    n_tiles = n * tps

    def tile_of(t):
        return (t // tps, t % tps, 0)

    mixer_j = lambda i: jnp.minimum(i, n_tiles - 1)
    mixer_tile = lambda i: tile_of(mixer_j(i))
    mlp_tile = lambda i: tile_of(jnp.maximum(i - 1, 0))
    per_b = lambda i: (mixer_j(i) // tps, 0, 0)

    cast_in, cast_in_specs, cast_out_specs, cast_out_shapes = [], [], [], []
    for w in (weights_f32 if has_next else ()):
        rows, cols = w.shape[1:]
        assert rows % (n_tiles * _BF_ROWS) == 0
        chunk = rows // n_tiles
        cast_in.append(w)
        cast_in_specs.append(pl.BlockSpec((None, chunk, cols), lambda i: (layer + 1, mixer_j(i), 0)))
        cast_out_specs.append(pl.BlockSpec((None, chunk, cols), lambda i: (0, mixer_j(i), 0)))
        cast_out_shapes.append(jax.ShapeDtypeStruct((1, rows, cols), _BF))

    own_in, own_out_shapes, own_scratch = [], [], []
    if own_cast:
        stage_cols = _D
        for w in weights_f32:
            rows, cols = w.shape[1:]
            assert rows % _STAGE_ROWS == 0 and cols % _LANES == 0 and (cols % stage_cols) % _LANES == 0
            own_in.append(w)
            own_out_shapes.append(jax.ShapeDtypeStruct((1, rows, cols), _BF))
            own_scratch.append(pltpu.VMEM((rows, cols), _BF))
        assert tm % _STAGE_ROWS == 0
        own_scratch += [pltpu.SemaphoreType.DMA((2 * tm // _STAGE_ROWS,)), pltpu.SemaphoreType.DMA((_N_WEIGHTS,))]

    weight_in = [] if own_cast else list(weights[0])
    weight_specs = [] if own_cast else [_weight_spec(w, weights[1]) for w in weights[0]]
    inputs = [sinks, x, gpre, gpost, gmpre, gmpost, convw, *weight_in, *own_in, *cast_in]
    alias_in, alias_specs, aliases = _alias_args(prev, len(inputs))
    out = pl.pallas_call(
        functools.partial(_prompt_layer_kernel, tm=tm, tps=tps, layer=layer, first=first, own_cast=own_cast,
                          n_cast=len(cast_in)),
        grid=(n_tiles + 1,),
        in_specs=[_SMEM_SPEC,
                  pl.BlockSpec((None, tm, _D), mixer_tile)]
                 + [_whole_spec(a) for a in (gpre, gpost, gmpre, gmpost, convw)]
                 + weight_specs + [_ANY_SPEC] * len(own_in) + cast_in_specs + alias_specs,
        out_specs=[pl.BlockSpec((1, tm, _D), mlp_tile),
                   _stacked_out_spec(depth, layer, first, (None, _KVD, _WIN), per_b),
                   _stacked_out_spec(depth, layer, first, (None, _KVD, _WIN), per_b),
                   _stacked_out_spec(depth, layer, first, (None, 2, _CONV), per_b)]
                  + cast_out_specs + [_ANY_SPEC] * len(own_out_shapes),
        out_shape=[jax.ShapeDtypeStruct((n, seq, _D), _F32),
                   jax.ShapeDtypeStruct((depth, n, _KVD, _WIN), _F32),
                   jax.ShapeDtypeStruct((depth, n, _KVD, _WIN), _F32),
                   jax.ShapeDtypeStruct((depth, n, 2, _CONV), _F32)] + cast_out_shapes + own_out_shapes,
        scratch_shapes=own_scratch + [
                        pltpu.VMEM((2, _WIN + tm, _LANES), _BF),
                        pltpu.VMEM((2, _WIN + tm, _LANES), _BF),
                        pltpu.VMEM((4, 2, tm, _LANES), _BF),
                        pltpu.VMEM((_SUB + tm, _CONV), _F32),
                        pltpu.VMEM((tm, _ATT), _BF),
                        pltpu.VMEM((tm, _D), _F32),
                        pltpu.VMEM((_WIN, _KVD), _F32),
                        pltpu.VMEM((_WIN, _KVD), _F32),
                        pltpu.VMEM((_SUB, _CONV), _F32)],
        input_output_aliases=aliases,
        compiler_params=pltpu.CompilerParams(dimension_semantics=("arbitrary",),
                                             vmem_limit_bytes=_VMEM_LIMIT),
        name="prompt_layer",
    )(*inputs, *alias_in)
    n_next = len(cast_in)
    return out[0], tuple(out[1:4]), tuple(out[4:4 + n_next]), tuple(out[4 + n_next:])


def _mixer_sample(x, layer, prev, ck, cv, st, sinks, gpre, gpost, convw, slot, win, wao, wco, wout):
    n, t, _ = x.shape
    depth = sinks.shape[0]
    sb = _SAMPLE_SB
    first = prev is None
    assert n % sb == 0 and t == _SUB and ck.shape == (depth, n, _KVD, _WIN)
    tr = sb * t
    rows = lambda i: (i, 0)
    seqs = lambda i: (i, 0, 0)
    cache_spec = pl.BlockSpec((None, sb, _KVD, _WIN), lambda i: (layer, i, 0, 0))
    state_spec = pl.BlockSpec((None, sb, 2, _CONV), lambda i: (layer, i, 0, 0))
    inputs = [sinks, x.reshape(n * t, _D), gpre, gpost, convw, ck, cv, st, win, wao, wco, wout]
    alias_in, alias_specs, aliases = _alias_args(prev, len(inputs))
    out = pl.pallas_call(
        functools.partial(_mixer_sample_kernel, sb=sb, t=t, layer=layer, first=first),
        grid=(n // sb,),
        in_specs=[_SMEM_SPEC,
                  pl.BlockSpec((tr, _D), rows),
                  _whole_spec(gpre), _whole_spec(gpost), _whole_spec(convw),
                  cache_spec, cache_spec, state_spec]
                 + [_weight_spec(w, slot) for w in (win, wao, wco, wout)] + alias_specs,
        out_specs=[pl.BlockSpec((tr, _D), rows),
                   _stacked_out_spec(depth, layer, first, (sb, _KVD, _WIN), seqs),
                   _stacked_out_spec(depth, layer, first, (sb, _KVD, _WIN), seqs),
                   _stacked_out_spec(depth, layer, first, (sb, 2, _CONV), seqs)],
        out_shape=[jax.ShapeDtypeStruct((n * t, _D), _F32),
                   jax.ShapeDtypeStruct((depth, n, _KVD, _WIN), _F32),
                   jax.ShapeDtypeStruct((depth, n, _KVD, _WIN), _F32),
                   jax.ShapeDtypeStruct((depth, n, 2, _CONV), _F32)],
        scratch_shapes=[pltpu.VMEM((sb, 2 * _SUB, _CONV), _F32)],
        input_output_aliases=aliases,
        compiler_params=pltpu.CompilerParams(dimension_semantics=("arbitrary",),
                                             vmem_limit_bytes=_VMEM_LIMIT),
        name="mixer_sample",
    )(*inputs, *alias_in)
    return out[0].reshape(n, t, _D), tuple(out[1:])


def _mlp(x, layer, gpre, gpost, slot, wup, wdown):
    shape = x.shape
    x2 = x.reshape(-1, _D)
    nt = x2.shape[0]
    tm = _MLP_TM
    assert nt % tm == 0
    rows = lambda i: (i, 0)
    y = pl.pallas_call(
        functools.partial(_mlp_kernel, layer=layer),
        grid=(nt // tm,),
        in_specs=[pl.BlockSpec((tm, _D), rows), _whole_spec(gpre), _whole_spec(gpost),
                  _weight_spec(wup, slot), _weight_spec(wdown, slot)],
        out_specs=pl.BlockSpec((tm, _D), rows),
        out_shape=jax.ShapeDtypeStruct((nt, _D), _F32),
        compiler_params=pltpu.CompilerParams(dimension_semantics=("parallel",),
                                             vmem_limit_bytes=_VMEM_LIMIT),
        name="mlp",
    )(x2, gpre, gpost, wup, wdown)
    return y.reshape(shape)


def kernel(x_prompt, x_sample, cache_k, cache_v, state_conv, g_mix_pre, g_mix_post, g_mlp_pre, g_mlp_post,
           w_in, attn_sinks, conv_w, w_attn_o, w_conv_o, w_out, w_up, w_down):
    depth = w_in.shape[0]
    n_dec = x_sample.shape[0]
    feature_major = lambda c: jnp.transpose(c, (0, 1, 3, 4, 2)).reshape(depth, n_dec, _KVD, _WIN)
    ck, cv = feature_major(cache_k), feature_major(cache_v)
    weights_f32 = (w_in, w_attn_o, w_conv_o, w_out, w_up, w_down)
    gains = [g_mix_pre, g_mix_post, g_mlp_pre, g_mlp_post]
    conv_w = jnp.transpose(conv_w, (1, 0, 2))

    weights = None
    yp, ys = x_prompt, x_sample
    caches_p = caches_s = None
    for l in range(depth):
        yp, caches_p, next_cast, own = _prompt_layer(yp, l, caches_p, attn_sinks, *gains, conv_w, weights,
                                                     weights_f32)
        if weights is None:
            weights = (own, 0)
        ys, caches_s = _mixer_sample(ys, l, caches_s, ck, cv, state_conv, attn_sinks, gains[0], gains[1], conv_w,
                                     weights[1], *weights[0][:4])
        ys = _mlp(ys, l, gains[2], gains[3], weights[1], *weights[0][4:])
        weights = (next_cast, 0)

    position_major = lambda a: jnp.transpose(a.reshape(a.shape[:2] + (2, _HD, _WIN)), (0, 1, 4, 2, 3))
    return (yp, ys, position_major(caches_p[0]), position_major(caches_p[1]), caches_p[2],
            position_major(caches_s[0]), position_major(caches_s[1]), caches_s[2])
```

```python
import functools

import jax
import jax.numpy as jnp
from jax import lax
from jax.experimental import pallas as pl
from jax.experimental.pallas import tpu as pltpu

_D = 1024
_HD = 64
_NH = 8
_ATT = _NH * _HD
_KVD = 128
_CONV = 512
_DFF = 4096
_WIN = 128
_EPS = 1e-6
_LANES = 128
_SUB = 8
_BF_ROWS = 16

_OFF_Q = 0
_OFF_KV = _ATT
_OFF_B = _OFF_KV + 2 * _KVD
_OFF_C = _OFF_B + _CONV
_OFF_U = _OFF_C + _CONV
_OFF_GA = _OFF_U + _CONV
_OFF_GC = _OFF_GA + _D
_IN_DIM = _OFF_GC + _D

_PROMPT_TM = 512
_SAMPLE_SB = 32
_MLP_TM = 512
_VMEM_LIMIT = 62 * 1024 * 1024

_BF = jnp.bfloat16
_F32 = jnp.float32


def _rms(x, g):
    return x * lax.rsqrt(jnp.mean(x * x, axis=-1, keepdims=True) + _EPS) * g


def _gain_row(g_ref, layer):
    return g_ref.at[pl.ds(layer, 1)]


def _conv_taps(convw_ref, layer):
    return jnp.concatenate([convw_ref[k, pl.ds(layer, 1), :] for k in range(convw_ref.shape[0])], axis=0)


def _proj(h_bf, win_ref, off, width):
    return jnp.dot(h_bf, win_ref[:, off:off + width], preferred_element_type=_F32)


def _dup_halves(val, lo):
    rolled = pltpu.roll(val, _HD, axis=1)
    return jnp.where(lo, val, rolled), jnp.where(lo, rolled, val)


def _band_mask(qi, kj):
    return (kj > qi) & (kj <= qi + _WIN)


def _dot_nt(a, b):
    return lax.dot_general(a, b, (((1,), (1,)), ((), ())), preferred_element_type=_F32)


def _own_layer(ref, first):
    if not first:
        return ref
    if ref.shape[0] > 1:
        ref[1:] = jnp.zeros((ref.shape[0] - 1,) + ref.shape[1:], ref.dtype)
    return ref.at[0]


def _softmax_pv(s, vd, bias, sinks, nq):
    es, rden = [], []
    for j in range(4):
        sj = s[j * nq:(j + 1) * nq] + bias
        m = jnp.maximum(jnp.max(sj, axis=-1, keepdims=True), sinks[j])
        e = jnp.exp(sj - m)
        den = jnp.sum(e, axis=-1, keepdims=True) + jnp.exp(sinks[j] - m)
        es.append(e)
        rden.append(1.0 / den)
    e_all = jnp.concatenate(es, axis=0).astype(_BF)
    o = jnp.dot(e_all, vd, preferred_element_type=_F32)
    return [o[j * nq:(j + 1) * nq] * rden[j] for j in range(4)]


def _mix_and_project(x, h, attn_bf, u_shift2, u_shift1, u, win_ref, cw, wao_ref, wco_ref, wout_ref,
                     gpost_ref):
    attn_b = jnp.dot(attn_bf, wao_ref[...], preferred_element_type=_F32)
    mixed = jax.nn.sigmoid(_proj(h, win_ref, _OFF_GA, _D)) * attn_b
    z = cw[0:1] * u_shift2
    z = z + cw[1:2] * u_shift1
    z = z + cw[2:3] * u
    bz = (_proj(h, win_ref, _OFF_B, _CONV) * z).astype(_BF)
    conv_b = jnp.dot(bz, wco_ref[...], preferred_element_type=_F32)
    mixed = mixed + jax.nn.sigmoid(_proj(h, win_ref, _OFF_GC, _D)) * conv_b
    mo = jnp.dot(mixed.astype(_BF), wout_ref[...], preferred_element_type=_F32)
    return x + _rms(mo, gpost_ref[...])


def _run_interleaved(*stages):
    stages = list(stages)
    while stages:
        for g in list(stages):
            try:
                next(g)
            except StopIteration:
                stages.remove(g)


def _mlp_stages(x_ref, y_ref, gpre_ref, gpost_ref, wup_ref, wdown_ref):
    x = x_ref[...]
    y_ref[...] = x
    hm = _rms(x, gpre_ref[...]).astype(_BF)
    yield
    acc = None
    for j in range(_DFF // _D):
        hj = jnp.dot(hm, wup_ref[:, j * _D:(j + 1) * _D], preferred_element_type=_F32)
        yield
        hj = jnp.square(jnp.maximum(hj, 0.0)).astype(_BF)
        part = jnp.dot(hj, wdown_ref[j * _D:(j + 1) * _D, :], preferred_element_type=_F32)
        acc = part if acc is None else acc + part
        yield
    y_ref[...] = y_ref[...] + _rms(acc, gpost_ref[...])


_N_WEIGHTS = 6
_STAGE_ROWS = 128


def _prompt_layer_kernel(sink_ref, x_ref, gpre_ref, gpost_ref, gmpre_ref, gmpost_ref, convw_ref,
                         *rest, tm, tps, layer, first, own_cast, n_cast):
    w_hbm, rest = rest[:_N_WEIGHTS], rest[_N_WEIGHTS:]
    cast_src, rest = rest[:n_cast], rest[n_cast + (0 if first else 3):]
    y_blk, nk_ref, nv_ref, nc_ref = rest[:4]
    y_ref = y_blk.at[0]
    cast_dst, rest = rest[4:4 + n_cast], rest[4 + n_cast:]
    if own_cast:
        own_out, weights = rest[:_N_WEIGHTS], rest[_N_WEIGHTS:2 * _N_WEIGHTS]
        stage_sem, out_sem = rest[2 * _N_WEIGHTS:2 * _N_WEIGHTS + 2]
        rest = rest[2 * _N_WEIGHTS + 2:]
    else:
        weights, load_sem, rest = rest[:_N_WEIGHTS], rest[_N_WEIGHTS], rest[_N_WEIGHTS + 1:]
    win_ref, wao_ref, wco_ref, wout_ref, wup_ref, wdown_ref = weights
    n_mixer_weights = 4

    def weight_in_copies():
        return [pltpu.make_async_copy(src.at[0], dst, load_sem.at[k])
                for k, (src, dst) in enumerate(zip(w_hbm, weights))]
    kdup, vdup, q4s, ubuf, attn_scr, x1_scr, klast, vlast, ulast = rest
    j = pl.program_id(0)
    gpre_ref, gpost_ref, gmpre_ref, gmpost_ref = (_gain_row(g, layer)
                                                  for g in (gpre_ref, gpost_ref, gmpre_ref, gmpost_ref))

    def weight_out_copies():
        return [pltpu.make_async_copy(src, dst.at[0], out_sem.at[k])
                for k, (src, dst) in enumerate(zip(weights, own_out))]

    def load_own_weights():
        stage_bufs = (x1_scr, y_blk)
        stage_cols = x1_scr.shape[1]
        per_buf = x1_scr.shape[0] // _STAGE_ROWS
        n_slots = per_buf * len(stage_bufs)
        blocks = [(k, r0, c0, min(stage_cols, w_hbm[k].shape[2] - c0))
                  for k in range(_N_WEIGHTS)
                  for r0 in range(0, w_hbm[k].shape[1], _STAGE_ROWS)
                  for c0 in range(0, w_hbm[k].shape[2], stage_cols)]

        def slot_index(t, width):
            slot = t % n_slots
            buf = stage_bufs[slot // per_buf]
            index = (pl.ds((slot % per_buf) * _STAGE_ROWS, _STAGE_ROWS), pl.ds(0, width))
            return buf, ((0,) + index if buf is y_blk else index)

        def copy_in(t):
            k, r0, c0, width = blocks[t]
            buf, index = slot_index(t, width)
            return pltpu.make_async_copy(w_hbm[k].at[layer, pl.ds(r0, _STAGE_ROWS), pl.ds(c0, width)],
                                         buf.at[index], stage_sem.at[t % n_slots])

        for t in range(min(n_slots, len(blocks))):
            copy_in(t).start()
        for t, (k, r0, c0, width) in enumerate(blocks):
            copy_in(t).wait()
            buf, index = slot_index(t, width)
            weights[k][r0:r0 + _STAGE_ROWS, c0:c0 + width] = buf[index].astype(_BF)
            if t + n_slots < len(blocks):
                copy_in(t + n_slots).start()
        for copy in weight_out_copies():
            copy.start()

    def cast_next_layer():
        for src, dst in zip(cast_src, cast_dst):
            dst[...] = src[...].astype(_BF)

    n_tiles = pl.num_programs(0) - 1
    s = j % tps

    def start_sequence():
        @pl.when(s == 0)
        def _():
            kdup[:, 0:_WIN, :] = jnp.zeros((2, _WIN, _LANES), _BF)
            vdup[:, 0:_WIN, :] = jnp.zeros((2, _WIN, _LANES), _BF)
            ubuf[0:_SUB, :] = jnp.zeros((_SUB, _CONV), _F32)

    def end_sequence():
        @pl.when(s == tps - 1)
        def _():
            _own_layer(nk_ref, first)[...] = klast[...].T
            _own_layer(nv_ref, first)[...] = vlast[...].T
            _own_layer(nc_ref, first)[...] = ulast[_SUB - 2:_SUB, :]

    def mlp(src):
        return _mlp_stages(src, y_ref, gmpre_ref, gmpost_ref, wup_ref, wdown_ref)

    def mixer():
        return _mixer_prompt_stages(sink_ref, x_ref, gpre_ref, gpost_ref, convw_ref, win_ref, wao_ref, wco_ref,
                                    wout_ref, kdup, vdup, q4s, ubuf, attn_scr, x1_scr, klast, vlast, ulast, s,
                                    tm=tm, layer=layer)

    if not own_cast:
        @pl.when(j == 1)
        def _():
            for copy in weight_in_copies()[n_mixer_weights:]:
                copy.wait()

    @pl.when(j == 0)
    def _():
        if own_cast:
            load_own_weights()
        else:
            copies = weight_in_copies()
            for copy in copies:
                copy.start()
            for copy in copies[:n_mixer_weights]:
                copy.wait()
        start_sequence()
        cast_next_layer()
        _run_interleaved(mixer())
        end_sequence()

    @pl.when((j > 0) & (j < n_tiles))
    def _():
        start_sequence()
        mlp_stages, mixer_stages = mlp(x1_scr), mixer()
        for _ in range(2 * (_DFF // _D) - 1):
            next(mlp_stages)
        cast_next_layer()
        next(mixer_stages)
        next(mlp_stages)
        next(mixer_stages)
        _run_interleaved(mlp_stages, mixer_stages)
        end_sequence()

    @pl.when(j == n_tiles)
    def _():
        cast_next_layer()
        _run_interleaved(mlp(x1_scr))
        if own_cast:
            for copy in weight_out_copies():
                copy.wait()


def _mixer_prompt_stages(sink_ref, x_ref, gpre_ref, gpost_ref, convw_ref, win_ref, wao_ref, wco_ref, wout_ref,
                         kdup, vdup, q4s, ubuf, attn_scr, x1_scr, klast, vlast, ulast, s, *, tm, layer):
    x = x_ref[...]
    h = _rms(x, gpre_ref[...]).astype(_BF)
    lo = lax.broadcasted_iota(jnp.int32, (tm, _LANES), 1) < _HD

    hr = tm // 2
    kv = jnp.concatenate([_proj(h[:hr], win_ref, _OFF_KV, 2 * _KVD), _proj(h[hr:], win_ref, _OFF_KV, 2 * _KVD)],
                         axis=0)
    k = kv[:, :_KVD]
    v = kv[:, _KVD:]
    for val, dst in ((k, kdup), (v, vdup)):
        d0, d1 = _dup_halves(val, lo)
        dst[0, _WIN:_WIN + tm, :] = d0.astype(_BF)
        dst[1, _WIN:_WIN + tm, :] = d1.astype(_BF)

    q = _proj(h, win_ref, _OFF_Q, _ATT) * (_HD ** -0.5)
    for c in range(4):
        qc = q[:, c * _LANES:(c + 1) * _LANES]
        q4s[c, 0] = jnp.where(lo, qc, 0.0).astype(_BF)
        q4s[c, 1] = jnp.where(lo, 0.0, qc).astype(_BF)

    qi = lax.broadcasted_iota(jnp.int32, (_WIN, 2 * _WIN), 0)
    kj = lax.broadcasted_iota(jnp.int32, (_WIN, 2 * _WIN), 1)
    band = _band_mask(qi, kj)
    neg = jnp.float32(-jnp.inf)
    bias = jnp.where(band, 0.0, neg)
    bias_first = jnp.where(band & ((kj >= _WIN) | (s > 0)), 0.0, neg)
    lo_q = lax.broadcasted_iota(jnp.int32, (_WIN, _LANES), 1) < _HD

    def scores_of(qb):
        r0 = qb * _WIN
        out = []
        for g in range(2):
            q4 = jnp.concatenate([q4s[2 * g, 0, r0:r0 + _WIN, :], q4s[2 * g, 1, r0:r0 + _WIN, :],
                                  q4s[2 * g + 1, 0, r0:r0 + _WIN, :], q4s[2 * g + 1, 1, r0:r0 + _WIN, :]], axis=0)
            out.append(_dot_nt(q4, kdup[g, r0:r0 + 2 * _WIN, :]))
        return out

    def finish_block(qb, scores):
        r0 = qb * _WIN
        b = bias_first if qb == 0 else bias
        for g in range(2):
            sinks = [sink_ref[layer, 4 * g + j] for j in range(4)]
            o = _softmax_pv(scores[g], vdup[g, r0:r0 + 2 * _WIN, :], b, sinks, _WIN)
            attn_scr[r0:r0 + _WIN, (2 * g) * _LANES:(2 * g + 1) * _LANES] = jnp.where(lo_q, o[0], o[1]).astype(_BF)
            attn_scr[r0:r0 + _WIN, (2 * g + 1) * _LANES:(2 * g + 2) * _LANES] = (
                jnp.where(lo_q, o[2], o[3]).astype(_BF))

    def conv_input():
        u = _proj(h, win_ref, _OFF_C, _CONV) * _proj(h, win_ref, _OFF_U, _CONV)
        ubuf[_SUB:_SUB + tm, :] = u
        return u

    fillers = [conv_input,
               lambda: jax.nn.sigmoid(_proj(h, win_ref, _OFF_GA, _D)),
               lambda: jax.nn.sigmoid(_proj(h, win_ref, _OFF_GC, _D)),
               lambda: _proj(h, win_ref, _OFF_B, _CONV)]
    filled, pending = [], []
    for qb in range(tm // _WIN):
        pending.append((qb, scores_of(qb)))
        if qb < len(fillers):
            filled.append(fillers[qb]())
        if len(pending) > 1:
            finish_block(*pending.pop(0))
    filled += [f() for f in fillers[len(filled):]]
    u, gate_attn, gate_conv, b_gate = filled

    cw = _conv_taps(convw_ref, layer)
    z = cw[0:1] * ubuf[_SUB - 2:_SUB - 2 + tm, :]
    z = z + cw[1:2] * ubuf[_SUB - 1:_SUB - 1 + tm, :]
    z = z + cw[2:3] * u
    conv_b = jnp.dot((b_gate * z).astype(_BF), wco_ref[...], preferred_element_type=_F32)
    for block in pending:
        finish_block(*block)

    kdup[:, 0:_WIN, :] = kdup[:, tm:tm + _WIN, :]
    vdup[:, 0:_WIN, :] = vdup[:, tm:tm + _WIN, :]

    mixed = gate_attn * jnp.dot(attn_scr[...], wao_ref[...], preferred_element_type=_F32)
    mixed = (mixed + gate_conv * conv_b).astype(_BF)
    yield
    mo = jnp.dot(mixed, wout_ref[...], preferred_element_type=_F32)
    yield
    x1_scr[...] = x + _rms(mo, gpost_ref[...])
    ubuf[0:_SUB, :] = ubuf[tm:tm + _SUB, :]
    klast[...] = k[tm - _WIN:tm, :]
    vlast[...] = v[tm - _WIN:tm, :]
    ulast[...] = u[tm - _SUB:tm, :]


def _mixer_sample_kernel(sink_ref, x_ref, gpre_ref, gpost_ref, convw_ref, ck_ref, cv_ref, st_ref,
                         win_ref, wao_ref, wco_ref, wout_ref, *rest, sb, t, layer, first):
    y_ref, nk_ref, nv_ref, nc_ref, ubuf = rest[0 if first else 3:]
    nk_ref, nv_ref, nc_ref = (_own_layer(r, first) for r in (nk_ref, nv_ref, nc_ref))
    gpre_ref, gpost_ref = _gain_row(gpre_ref, layer), _gain_row(gpost_ref, layer)
    tr = sb * t
    rows = _NH * t
    x = x_ref[...]
    h = _rms(x, gpre_ref[...]).astype(_BF)
    lo = lax.broadcasted_iota(jnp.int32, (tr, _LANES), 1) < _HD

    kv = _proj(h, win_ref, _OFF_KV, 2 * _KVD)
    k = kv[:, :_KVD]
    v = kv[:, _KVD:]
    q = _proj(h, win_ref, _OFF_Q, _ATT) * (_HD ** -0.5)
    pieces = []
    for hd in range(_NH):
        qc = q[:, (hd // 2) * _LANES:(hd // 2 + 1) * _LANES]
        if hd % 2 != hd // (_NH // 2):
            qc = pltpu.roll(qc, _HD, axis=1)
        piece = jnp.where(lo, qc, 0.0) if hd < _NH // 2 else jnp.where(lo, 0.0, qc)
        pieces.append(piece.reshape(sb, t, _LANES))
    q3 = jnp.concatenate(pieces, axis=1).astype(_BF)

    pad = jnp.zeros((sb, _BF_ROWS - t, _LANES), _F32)
    k_new = jnp.concatenate([k.reshape(sb, t, _KVD), pad], axis=1).astype(_BF)
    v_new = jnp.concatenate([v.reshape(sb, t, _KVD), pad], axis=1).astype(_BF)

    s_old = jnp.einsum('bqf,bfp->bqp', q3, ck_ref[...].astype(_BF), preferred_element_type=_F32)
    s_new = jnp.einsum('bqf,btf->bqt', q3, k_new, preferred_element_type=_F32)
    neg = jnp.float32(-jnp.inf)
    qi_old = lax.broadcasted_iota(jnp.int32, (rows, _WIN), 0) & (t - 1)
    kj_old = lax.broadcasted_iota(jnp.int32, (rows, _WIN), 1)
    qi_new = lax.broadcasted_iota(jnp.int32, (rows, _BF_ROWS), 0) & (t - 1)
    kj_new = lax.broadcasted_iota(jnp.int32, (rows, _BF_ROWS), 1) + _WIN
    s_old = s_old + jnp.where(_band_mask(qi_old, kj_old), 0.0, neg)[None]
    s_new = s_new + jnp.where(_band_mask(qi_new, kj_new), 0.0, neg)[None]
    sink_col = jnp.concatenate([jnp.full((t, 1), sink_ref[layer, hd], _F32) for hd in range(_NH)], axis=0)[None]
    m = jnp.maximum(jnp.maximum(jnp.max(s_old, axis=-1, keepdims=True), jnp.max(s_new, axis=-1, keepdims=True)),
                    sink_col)
    e_old = jnp.exp(s_old - m)
    e_new = jnp.exp(s_new - m)
    rden = 1.0 / (jnp.sum(e_old, axis=-1, keepdims=True) + jnp.sum(e_new, axis=-1, keepdims=True)
                  + jnp.exp(sink_col - m))
    o = (jnp.einsum('bqp,bfp->bqf', e_old.astype(_BF), cv_ref[...].astype(_BF), preferred_element_type=_F32)
         + jnp.einsum('bqt,btf->bqf', e_new.astype(_BF), v_new, preferred_element_type=_F32)) * rden
    chunks = []
    for c in range(_NH // 2):
        halves = []
        for hd in (2 * c, 2 * c + 1):
            oh = o[:, hd * t:(hd + 1) * t, :].reshape(tr, _LANES)
            if hd % 2 != hd // (_NH // 2):
                oh = pltpu.roll(oh, _HD, axis=1)
            halves.append(oh)
        chunks.append(jnp.where(lo, halves[0], halves[1]))
    attn = jnp.concatenate(chunks, axis=1)

    k_t = k.T
    v_t = v.T
    keep = lax.broadcasted_iota(jnp.int32, (_KVD, _WIN), 1) < _WIN - t
    per_chunk = _LANES // t
    for b in range(sb):
        c0 = (b // per_chunk) * _LANES
        shift_new = (_WIN - t - (b % per_chunk) * t) % _LANES
        for src_ref, new_t, dst_ref in ((ck_ref, k_t, nk_ref), (cv_ref, v_t, nv_ref)):
            old = pltpu.roll(src_ref[b], _WIN - t, axis=1)
            new = new_t[:, c0:c0 + _LANES]
            if shift_new:
                new = pltpu.roll(new, shift_new, axis=1)
            dst_ref[b] = jnp.where(keep, old, new)

    u = _proj(h, win_ref, _OFF_C, _CONV) * _proj(h, win_ref, _OFF_U, _CONV)
    u3 = u.reshape(sb, t, _CONV)
    ubuf[:, _SUB - 2:_SUB, :] = st_ref[...]
    ubuf[:, _SUB:_SUB + t, :] = u3
    us2 = ubuf[:, _SUB - 2:_SUB - 2 + t, :].reshape(tr, _CONV)
    us1 = ubuf[:, _SUB - 1:_SUB - 1 + t, :].reshape(tr, _CONV)
    nc_ref[...] = u3[:, t - 2:t, :]
    y_ref[...] = _mix_and_project(x, h, attn.astype(_BF), us2, us1, u,
                                  win_ref, _conv_taps(convw_ref, layer), wao_ref, wco_ref, wout_ref, gpost_ref)


def _mlp_kernel(x_ref, gpre_ref, gpost_ref, wup_ref, wdown_ref, y_ref, *, layer):
    _run_interleaved(_mlp_stages(x_ref, y_ref, _gain_row(gpre_ref, layer), _gain_row(gpost_ref, layer),
                                 wup_ref, wdown_ref))


def _whole_spec(a):
    return pl.BlockSpec(a.shape, lambda *_: (0,) * a.ndim)


def _weight_spec(w, slot):
    return pl.BlockSpec((None,) + w.shape[1:], lambda *_: (slot, 0, 0), pipeline_mode=pl.Buffered(1))


_SMEM_SPEC = pl.BlockSpec(memory_space=pltpu.SMEM)
_ANY_SPEC = pl.BlockSpec(memory_space=pl.ANY)


def _stacked_out_spec(depth, layer, first, block, index_map):
    lead, l0 = (depth, 0) if first else (None, layer)
    return pl.BlockSpec((lead,) + block, lambda *idx: (l0,) + index_map(*idx))


def _alias_args(prev, n_inputs):
    if prev is None:
        return [], [], {}
    return list(prev), [_ANY_SPEC] * len(prev), {n_inputs + i: 1 + i for i in range(len(prev))}


def _prompt_layer(x, layer, prev, sinks, gpre, gpost, gmpre, gmpost, convw, weights, weights_f32):
    n, seq, _ = x.shape
    depth = sinks.shape[0]
    tm = _PROMPT_TM
    first = prev is None
    own_cast = weights is None
    has_next = layer + 1 < depth
    assert seq % tm == 0 and tm % _WIN == 0 and seq >= _WIN
    tps = seq // tm
    n_tiles = n * tps

    def tile_of(t):
        return (t // tps, t % tps, 0)

    mixer_j = lambda i: jnp.minimum(i, n_tiles - 1)
    mixer_tile = lambda i: tile_of(mixer_j(i))
    mlp_tile = lambda i: tile_of(jnp.maximum(i - 1, 0))
    per_b = lambda i: (mixer_j(i) // tps, 0, 0)

    cast_in, cast_in_specs, cast_out_specs, cast_out_shapes = [], [], [], []
    for w in (weights_f32 if has_next else ()):
        rows, cols = w.shape[1:]
        assert rows % (n_tiles * _BF_ROWS) == 0
        chunk = rows // n_tiles
        cast_in.append(w)
        cast_in_specs.append(pl.BlockSpec((None, chunk, cols), lambda i: (layer + 1, mixer_j(i), 0)))
        cast_out_specs.append(pl.BlockSpec((None, chunk, cols), lambda i: (0, mixer_j(i), 0)))
        cast_out_shapes.append(jax.ShapeDtypeStruct((1, rows, cols), _BF))

    own_in, own_out_shapes, own_scratch = [], [], []
    if own_cast:
        stage_cols = _D
        for w in weights_f32:
            rows, cols = w.shape[1:]
            assert rows % _STAGE_ROWS == 0 and cols % _LANES == 0 and (cols % stage_cols) % _LANES == 0
            own_in.append(w)
            own_out_shapes.append(jax.ShapeDtypeStruct((1, rows, cols), _BF))
            own_scratch.append(pltpu.VMEM((rows, cols), _BF))
        assert tm % _STAGE_ROWS == 0
        own_scratch += [pltpu.SemaphoreType.DMA((2 * tm // _STAGE_ROWS,)), pltpu.SemaphoreType.DMA((_N_WEIGHTS,))]

    weight_in = [] if own_cast else list(weights[0])
    weight_specs = [_ANY_SPEC] * len(weight_in)
    if not own_cast:
        assert weights[1] == 0 and all(w.shape[0] == 1 for w in weight_in)
        own_scratch = ([pltpu.VMEM(w.shape[1:], _BF) for w in weight_in]
                       + [pltpu.SemaphoreType.DMA((_N_WEIGHTS,))])
    inputs = [sinks, x, gpre, gpost, gmpre, gmpost, convw, *weight_in, *own_in, *cast_in]
    alias_in, alias_specs, aliases = _alias_args(prev, len(inputs))
    out = pl.pallas_call(
        functools.partial(_prompt_layer_kernel, tm=tm, tps=tps, layer=layer, first=first, own_cast=own_cast,
                          n_cast=len(cast_in)),
        grid=(n_tiles + 1,),
        in_specs=[_SMEM_SPEC,
                  pl.BlockSpec((None, tm, _D), mixer_tile)]
                 + [_whole_spec(a) for a in (gpre, gpost, gmpre, gmpost, convw)]
                 + weight_specs + [_ANY_SPEC] * len(own_in) + cast_in_specs + alias_specs,
        out_specs=[pl.BlockSpec((1, tm, _D), mlp_tile),
                   _stacked_out_spec(depth, layer, first, (None, _KVD, _WIN), per_b),
                   _stacked_out_spec(depth, layer, first, (None, _KVD, _WIN), per_b),
                   _stacked_out_spec(depth, layer, first, (None, 2, _CONV), per_b)]
                  + cast_out_specs + [_ANY_SPEC] * len(own_out_shapes),
        out_shape=[jax.ShapeDtypeStruct((n, seq, _D), _F32),
                   jax.ShapeDtypeStruct((depth, n, _KVD, _WIN), _F32),
                   jax.ShapeDtypeStruct((depth, n, _KVD, _WIN), _F32),
                   jax.ShapeDtypeStruct((depth, n, 2, _CONV), _F32)] + cast_out_shapes + own_out_shapes,
        scratch_shapes=own_scratch + [
                        pltpu.VMEM((2, _WIN + tm, _LANES), _BF),
                        pltpu.VMEM((2, _WIN + tm, _LANES), _BF),
                        pltpu.VMEM((4, 2, tm, _LANES), _BF),
                        pltpu.VMEM((_SUB + tm, _CONV), _F32),
                        pltpu.VMEM((tm, _ATT), _BF),
                        pltpu.VMEM((tm, _D), _F32),
                        pltpu.VMEM((_WIN, _KVD), _F32),
                        pltpu.VMEM((_WIN, _KVD), _F32),
                        pltpu.VMEM((_SUB, _CONV), _F32)],
        input_output_aliases=aliases,
        compiler_params=pltpu.CompilerParams(dimension_semantics=("arbitrary",),
                                             vmem_limit_bytes=_VMEM_LIMIT),
        name="prompt_layer",
    )(*inputs, *alias_in)
    n_next = len(cast_in)
    return out[0], tuple(out[1:4]), tuple(out[4:4 + n_next]), tuple(out[4 + n_next:])


def _mixer_sample(x, layer, prev, ck, cv, st, sinks, gpre, gpost, convw, slot, win, wao, wco, wout):
    n, t, _ = x.shape
    depth = sinks.shape[0]
    sb = _SAMPLE_SB
    first = prev is None
    assert n % sb == 0 and t == _SUB and ck.shape == (depth, n, _KVD, _WIN)
    tr = sb * t
    rows = lambda i: (i, 0)
    seqs = lambda i: (i, 0, 0)
    cache_spec = pl.BlockSpec((None, sb, _KVD, _WIN), lambda i: (layer, i, 0, 0))
    state_spec = pl.BlockSpec((None, sb, 2, _CONV), lambda i: (layer, i, 0, 0))
    inputs = [sinks, x.reshape(n * t, _D), gpre, gpost, convw, ck, cv, st, win, wao, wco, wout]
    alias_in, alias_specs, aliases = _alias_args(prev, len(inputs))
    out = pl.pallas_call(
        functools.partial(_mixer_sample_kernel, sb=sb, t=t, layer=layer, first=first),
        grid=(n // sb,),
        in_specs=[_SMEM_SPEC,
                  pl.BlockSpec((tr, _D), rows),
                  _whole_spec(gpre), _whole_spec(gpost), _whole_spec(convw),
                  cache_spec, cache_spec, state_spec]
                 + [_weight_spec(w, slot) for w in (win, wao, wco, wout)] + alias_specs,
        out_specs=[pl.BlockSpec((tr, _D), rows),
                   _stacked_out_spec(depth, layer, first, (sb, _KVD, _WIN), seqs),
                   _stacked_out_spec(depth, layer, first, (sb, _KVD, _WIN), seqs),
                   _stacked_out_spec(depth, layer, first, (sb, 2, _CONV), seqs)],
        out_shape=[jax.ShapeDtypeStruct((n * t, _D), _F32),
                   jax.ShapeDtypeStruct((depth, n, _KVD, _WIN), _F32),
                   jax.ShapeDtypeStruct((depth, n, _KVD, _WIN), _F32),
                   jax.ShapeDtypeStruct((depth, n, 2, _CONV), _F32)],
        scratch_shapes=[pltpu.VMEM((sb, 2 * _SUB, _CONV), _F32)],
        input_output_aliases=aliases,
        compiler_params=pltpu.CompilerParams(dimension_semantics=("arbitrary",),
                                             vmem_limit_bytes=_VMEM_LIMIT),
        name="mixer_sample",
    )(*inputs, *alias_in)
    return out[0].reshape(n, t, _D), tuple(out[1:])


def _mlp(x, layer, gpre, gpost, slot, wup, wdown):
    shape = x.shape
    x2 = x.reshape(-1, _D)
    nt = x2.shape[0]
    tm = _MLP_TM
    assert nt % tm == 0
    rows = lambda i: (i, 0)
    y = pl.pallas_call(
        functools.partial(_mlp_kernel, layer=layer),
        grid=(nt // tm,),
        in_specs=[pl.BlockSpec((tm, _D), rows), _whole_spec(gpre), _whole_spec(gpost),
                  _weight_spec(wup, slot), _weight_spec(wdown, slot)],
        out_specs=pl.BlockSpec((tm, _D), rows),
        out_shape=jax.ShapeDtypeStruct((nt, _D), _F32),
        compiler_params=pltpu.CompilerParams(dimension_semantics=("parallel",),
                                             vmem_limit_bytes=_VMEM_LIMIT),
        name="mlp",
    )(x2, gpre, gpost, wup, wdown)
    return y.reshape(shape)


def kernel(x_prompt, x_sample, cache_k, cache_v, state_conv, g_mix_pre, g_mix_post, g_mlp_pre, g_mlp_post,
           w_in, attn_sinks, conv_w, w_attn_o, w_conv_o, w_out, w_up, w_down):
    depth = w_in.shape[0]
    n_dec = x_sample.shape[0]
    feature_major = lambda c: jnp.transpose(c, (0, 1, 3, 4, 2)).reshape(depth, n_dec, _KVD, _WIN)
    ck, cv = feature_major(cache_k), feature_major(cache_v)
    weights_f32 = (w_in, w_attn_o, w_conv_o, w_out, w_up, w_down)
    gains = [g_mix_pre, g_mix_post, g_mlp_pre, g_mlp_post]
    conv_w = jnp.transpose(conv_w, (1, 0, 2))

    weights = None
    yp, ys = x_prompt, x_sample
    caches_p = caches_s = None
    for l in range(depth):
        yp, caches_p, next_cast, own = _prompt_layer(yp, l, caches_p, attn_sinks, *gains, conv_w, weights,
                                                     weights_f32)
        if weights is None:
            weights = (own, 0)
        ys, caches_s = _mixer_sample(ys, l, caches_s, ck, cv, state_conv, attn_sinks, gains[0], gains[1], conv_w,
                                     weights[1], *weights[0][:4])
        ys = _mlp(ys, l, gains[2], gains[3], weights[1], *weights[0][4:])
        weights = (next_cast, 0)

    position_major = lambda a: jnp.transpose(a.reshape(a.shape[:2] + (2, _HD, _WIN)), (0, 1, 4, 2, 3))
    return (yp, ys, position_major(caches_p[0]), position_major(caches_p[1]), caches_p[2],
            position_major(caches_s[0]), position_major(caches_s[1]), caches_s[2])
```

```python
import functools

import jax
import jax.numpy as jnp
from jax import lax
from jax.experimental import pallas as pl
from jax.experimental.pallas import tpu as pltpu

_D = 1024
_HD = 64
_NH = 8
_ATT = _NH * _HD
_KVD = 128
_CONV = 512
_DFF = 4096
_WIN = 128
_EPS = 1e-6
_LANES = 128
_SUB = 8
_BF_ROWS = 16

_OFF_Q = 0
_OFF_KV = _ATT
_OFF_B = _OFF_KV + 2 * _KVD
_OFF_C = _OFF_B + _CONV
_OFF_U = _OFF_C + _CONV
_OFF_GA = _OFF_U + _CONV
_OFF_GC = _OFF_GA + _D
_IN_DIM = _OFF_GC + _D

_PROMPT_TM = 512
_SAMPLE_SB = 32
_MLP_TM = 512
_VMEM_LIMIT = 62 * 1024 * 1024

_BF = jnp.bfloat16
_F32 = jnp.float32


def _rms(x, g):
    return x * lax.rsqrt(jnp.mean(x * x, axis=-1, keepdims=True) + _EPS) * g


def _gain_row(g_ref, layer):
    return g_ref.at[pl.ds(layer, 1)]


def _conv_taps(convw_ref, layer):
    return jnp.concatenate([convw_ref[k, pl.ds(layer, 1), :] for k in range(convw_ref.shape[0])], axis=0)


def _proj(h_bf, win_ref, off, width):
    return jnp.dot(h_bf, win_ref[:, off:off + width], preferred_element_type=_F32)


def _dup_halves(val, lo):
    rolled = pltpu.roll(val, _HD, axis=1)
    return jnp.where(lo, val, rolled), jnp.where(lo, rolled, val)


def _band_mask(qi, kj):
    return (kj > qi) & (kj <= qi + _WIN)


def _dot_nt(a, b):
    return lax.dot_general(a, b, (((1,), (1,)), ((), ())), preferred_element_type=_F32)


def _own_layer(ref, first):
    if not first:
        return ref
    if ref.shape[0] > 1:
        ref[1:] = jnp.zeros((ref.shape[0] - 1,) + ref.shape[1:], ref.dtype)
    return ref.at[0]


def _softmax_pv(s, vd, bias, sinks, nq):
    es, rden = [], []
    for j in range(4):
        sj = s[j * nq:(j + 1) * nq] + bias
        m = jnp.maximum(jnp.max(sj, axis=-1, keepdims=True), sinks[j])
        e = jnp.exp(sj - m)
        den = jnp.sum(e, axis=-1, keepdims=True) + jnp.exp(sinks[j] - m)
        es.append(e)
        rden.append(1.0 / den)
    e_all = jnp.concatenate(es, axis=0).astype(_BF)
    o = jnp.dot(e_all, vd, preferred_element_type=_F32)
    return [o[j * nq:(j + 1) * nq] * rden[j] for j in range(4)]


def _mix_and_project(x, h, attn_bf, u_shift2, u_shift1, u, win_ref, cw, wao_ref, wco_ref, wout_ref,
                     gpost_ref):
    attn_b = jnp.dot(attn_bf, wao_ref[...], preferred_element_type=_F32)
    mixed = jax.nn.sigmoid(_proj(h, win_ref, _OFF_GA, _D)) * attn_b
    z = cw[0:1] * u_shift2
    z = z + cw[1:2] * u_shift1
    z = z + cw[2:3] * u
    bz = (_proj(h, win_ref, _OFF_B, _CONV) * z).astype(_BF)
    conv_b = jnp.dot(bz, wco_ref[...], preferred_element_type=_F32)
    mixed = mixed + jax.nn.sigmoid(_proj(h, win_ref, _OFF_GC, _D)) * conv_b
    mo = jnp.dot(mixed.astype(_BF), wout_ref[...], preferred_element_type=_F32)
    return x + _rms(mo, gpost_ref[...])


def _run_interleaved(*stages):
    stages = list(stages)
    while stages:
        for g in list(stages):
            try:
                next(g)
            except StopIteration:
                stages.remove(g)


def _mlp_stages(x_ref, y_ref, gpre_ref, gpost_ref, wup_ref, wdown_ref):
    x = x_ref[...]
    y_ref[...] = x
    hm = _rms(x, gpre_ref[...]).astype(_BF)
    yield
    acc = None
    for j in range(_DFF // _D):
        hj = jnp.dot(hm, wup_ref[:, j * _D:(j + 1) * _D], preferred_element_type=_F32)
        yield
        hj = jnp.square(jnp.maximum(hj, 0.0)).astype(_BF)
        part = jnp.dot(hj, wdown_ref[j * _D:(j + 1) * _D, :], preferred_element_type=_F32)
        acc = part if acc is None else acc + part
        yield
    y_ref[...] = y_ref[...] + _rms(acc, gpost_ref[...])


_N_WEIGHTS = 6
_STAGE_ROWS = 128
_STAGE_COLS = 512


def _n_stage_slots(tm):
    return (tm // _STAGE_ROWS) * ((2 * _D + _CONV) // _STAGE_COLS)


def _prompt_layer_kernel(sink_ref, x_ref, gpre_ref, gpost_ref, gmpre_ref, gmpost_ref, convw_ref,
                         *rest, tm, tps, layer, first, own_cast, n_cast):
    w_hbm, rest = rest[:_N_WEIGHTS], rest[_N_WEIGHTS:]
    cast_src, rest = rest[:n_cast], rest[n_cast + (0 if first else 3):]
    y_blk, nk_ref, nv_ref, nc_ref = rest[:4]
    y_ref = y_blk.at[0]
    cast_dst, rest = rest[4:4 + n_cast], rest[4 + n_cast:]
    if own_cast:
        own_out, weights = rest[:_N_WEIGHTS], rest[_N_WEIGHTS:2 * _N_WEIGHTS]
        stage_sem, out_sem = rest[2 * _N_WEIGHTS:2 * _N_WEIGHTS + 2]
        rest = rest[2 * _N_WEIGHTS + 2:]
    else:
        weights, load_sem, rest = rest[:_N_WEIGHTS], rest[_N_WEIGHTS], rest[_N_WEIGHTS + 1:]
    win_ref, wao_ref, wco_ref, wout_ref, wup_ref, wdown_ref = weights
    n_mixer_weights = 4

    def weight_in_copies():
        return [pltpu.make_async_copy(src.at[0], dst, load_sem.at[k])
                for k, (src, dst) in enumerate(zip(w_hbm, weights))]
    kdup, vdup, q4s, ubuf, attn_scr, x1_scr, klast, vlast, ulast = rest
    j = pl.program_id(0)
    gpre_ref, gpost_ref, gmpre_ref, gmpost_ref = (_gain_row(g, layer)
                                                  for g in (gpre_ref, gpost_ref, gmpre_ref, gmpost_ref))

    def weight_out_copies():
        return [pltpu.make_async_copy(src, dst.at[0], out_sem.at[k])
                for k, (src, dst) in enumerate(zip(weights, own_out))]

    def load_own_weights():
        slots = [(buf, r0, c0)
                 for buf, n_cols in ((x1_scr, _D), (y_blk, _D), (ubuf, _CONV))
                 for r0 in range(0, tm, _STAGE_ROWS)
                 for c0 in range(0, n_cols, _STAGE_COLS)]
        n_slots = len(slots)
        assert n_slots == _n_stage_slots(tm)
        blocks = [(k, r0, c0, min(_STAGE_COLS, w_hbm[k].shape[2] - c0))
                  for k in range(_N_WEIGHTS)
                  for r0 in range(0, w_hbm[k].shape[1], _STAGE_ROWS)
                  for c0 in range(0, w_hbm[k].shape[2], _STAGE_COLS)]

        def slot_index(t, width):
            buf, r0, c0 = slots[t % n_slots]
            index = (pl.ds(r0, _STAGE_ROWS), pl.ds(c0, width))
            return buf, ((0,) + index if buf is y_blk else index)

        def copy_in(t):
            k, r0, c0, width = blocks[t]
            buf, index = slot_index(t, width)
            return pltpu.make_async_copy(w_hbm[k].at[layer, pl.ds(r0, _STAGE_ROWS), pl.ds(c0, width)],
                                         buf.at[index], stage_sem.at[t % n_slots])

        for t in range(min(n_slots, len(blocks))):
            copy_in(t).start()
        for t, (k, r0, c0, width) in enumerate(blocks):
            copy_in(t).wait()
            buf, index = slot_index(t, width)
            weights[k][r0:r0 + _STAGE_ROWS, c0:c0 + width] = buf[index].astype(_BF)
            if t + n_slots < len(blocks):
                copy_in(t + n_slots).start()
        for copy in weight_out_copies():
            copy.start()

    def cast_next_layer():
        for src, dst in zip(cast_src, cast_dst):
            dst[...] = src[...].astype(_BF)

    n_tiles = pl.num_programs(0) - 1
    s = j % tps

    def start_sequence():
        @pl.when(s == 0)
        def _():
            kdup[:, 0:_WIN, :] = jnp.zeros((2, _WIN, _LANES), _BF)
            vdup[:, 0:_WIN, :] = jnp.zeros((2, _WIN, _LANES), _BF)
            ubuf[0:_SUB, :] = jnp.zeros((_SUB, _CONV), _F32)

    def end_sequence():
        @pl.when(s == tps - 1)
        def _():
            _own_layer(nk_ref, first)[...] = klast[...].T
            _own_layer(nv_ref, first)[...] = vlast[...].T
            _own_layer(nc_ref, first)[...] = ulast[_SUB - 2:_SUB, :]

    def mlp(src):
        return _mlp_stages(src, y_ref, gmpre_ref, gmpost_ref, wup_ref, wdown_ref)

    def mixer():
        return _mixer_prompt_stages(sink_ref, x_ref, gpre_ref, gpost_ref, convw_ref, win_ref, wao_ref, wco_ref,
                                    wout_ref, kdup, vdup, q4s, ubuf, attn_scr, x1_scr, klast, vlast, ulast, s,
                                    tm=tm, layer=layer)

    if not own_cast:
        @pl.when(j == 1)
        def _():
            for copy in weight_in_copies()[n_mixer_weights:]:
                copy.wait()

    @pl.when(j == 0)
    def _():
        if own_cast:
            load_own_weights()
        else:
            copies = weight_in_copies()
            for copy in copies:
                copy.start()
            for copy in copies[:n_mixer_weights]:
                copy.wait()
        start_sequence()
        cast_next_layer()
        _run_interleaved(mixer())
        end_sequence()

    @pl.when((j > 0) & (j < n_tiles))
    def _():
        start_sequence()
        mlp_stages, mixer_stages = mlp(x1_scr), mixer()
        for _ in range(2 * (_DFF // _D) - 1):
            next(mlp_stages)
        cast_next_layer()
        next(mixer_stages)
        next(mlp_stages)
        next(mixer_stages)
        _run_interleaved(mlp_stages, mixer_stages)
        end_sequence()

    @pl.when(j == n_tiles)
    def _():
        cast_next_layer()
        _run_interleaved(mlp(x1_scr))
        if own_cast:
            for copy in weight_out_copies():
                copy.wait()


def _mixer_prompt_stages(sink_ref, x_ref, gpre_ref, gpost_ref, convw_ref, win_ref, wao_ref, wco_ref, wout_ref,
                         kdup, vdup, q4s, ubuf, attn_scr, x1_scr, klast, vlast, ulast, s, *, tm, layer):
    x = x_ref[...]
    h = _rms(x, gpre_ref[...]).astype(_BF)
    lo = lax.broadcasted_iota(jnp.int32, (tm, _LANES), 1) < _HD

    hr = tm // 2
    kv = jnp.concatenate([_proj(h[:hr], win_ref, _OFF_KV, 2 * _KVD), _proj(h[hr:], win_ref, _OFF_KV, 2 * _KVD)],
                         axis=0)
    k = kv[:, :_KVD]
    v = kv[:, _KVD:]
    for val, dst in ((k, kdup), (v, vdup)):
        d0, d1 = _dup_halves(val, lo)
        dst[0, _WIN:_WIN + tm, :] = d0.astype(_BF)
        dst[1, _WIN:_WIN + tm, :] = d1.astype(_BF)

    q = _proj(h, win_ref, _OFF_Q, _ATT) * (_HD ** -0.5)
    for c in range(4):
        qc = q[:, c * _LANES:(c + 1) * _LANES]
        q4s[c, 0] = jnp.where(lo, qc, 0.0).astype(_BF)
        q4s[c, 1] = jnp.where(lo, 0.0, qc).astype(_BF)

    qi = lax.broadcasted_iota(jnp.int32, (_WIN, 2 * _WIN), 0)
    kj = lax.broadcasted_iota(jnp.int32, (_WIN, 2 * _WIN), 1)
    band = _band_mask(qi, kj)
    neg = jnp.float32(-jnp.inf)
    bias = jnp.where(band, 0.0, neg)
    bias_first = jnp.where(band & ((kj >= _WIN) | (s > 0)), 0.0, neg)
    lo_q = lax.broadcasted_iota(jnp.int32, (_WIN, _LANES), 1) < _HD

    def scores_of(qb):
        r0 = qb * _WIN
        out = []
        for g in range(2):
            q4 = jnp.concatenate([q4s[2 * g, 0, r0:r0 + _WIN, :], q4s[2 * g, 1, r0:r0 + _WIN, :],
                                  q4s[2 * g + 1, 0, r0:r0 + _WIN, :], q4s[2 * g + 1, 1, r0:r0 + _WIN, :]], axis=0)
            out.append(_dot_nt(q4, kdup[g, r0:r0 + 2 * _WIN, :]))
        return out

    def finish_block(qb, scores):
        r0 = qb * _WIN
        b = bias_first if qb == 0 else bias
        for g in range(2):
            sinks = [sink_ref[layer, 4 * g + j] for j in range(4)]
            o = _softmax_pv(scores[g], vdup[g, r0:r0 + 2 * _WIN, :], b, sinks, _WIN)
            attn_scr[r0:r0 + _WIN, (2 * g) * _LANES:(2 * g + 1) * _LANES] = jnp.where(lo_q, o[0], o[1]).astype(_BF)
            attn_scr[r0:r0 + _WIN, (2 * g + 1) * _LANES:(2 * g + 2) * _LANES] = (
                jnp.where(lo_q, o[2], o[3]).astype(_BF))

    def conv_input():
        u = _proj(h, win_ref, _OFF_C, _CONV) * _proj(h, win_ref, _OFF_U, _CONV)
        ubuf[_SUB:_SUB + tm, :] = u
        return u

    fillers = [conv_input,
               lambda: jax.nn.sigmoid(_proj(h, win_ref, _OFF_GA, _D)),
               lambda: jax.nn.sigmoid(_proj(h, win_ref, _OFF_GC, _D)),
               lambda: _proj(h, win_ref, _OFF_B, _CONV)]
    filled, pending = [], []
    for qb in range(tm // _WIN):
        pending.append((qb, scores_of(qb)))
        if qb < len(fillers):
            filled.append(fillers[qb]())
        if len(pending) > 1:
            finish_block(*pending.pop(0))
    filled += [f() for f in fillers[len(filled):]]
    u, gate_attn, gate_conv, b_gate = filled

    cw = _conv_taps(convw_ref, layer)
    z = cw[0:1] * ubuf[_SUB - 2:_SUB - 2 + tm, :]
    z = z + cw[1:2] * ubuf[_SUB - 1:_SUB - 1 + tm, :]
    z = z + cw[2:3] * u
    conv_b = jnp.dot((b_gate * z).astype(_BF), wco_ref[...], preferred_element_type=_F32)
    for block in pending:
        finish_block(*block)

    kdup[:, 0:_WIN, :] = kdup[:, tm:tm + _WIN, :]
    vdup[:, 0:_WIN, :] = vdup[:, tm:tm + _WIN, :]

    mixed = gate_attn * jnp.dot(attn_scr[...], wao_ref[...], preferred_element_type=_F32)
    mixed = (mixed + gate_conv * conv_b).astype(_BF)
    yield
    mo = jnp.dot(mixed, wout_ref[...], preferred_element_type=_F32)
    yield
    x1_scr[...] = x + _rms(mo, gpost_ref[...])
    ubuf[0:_SUB, :] = ubuf[tm:tm + _SUB, :]
    klast[...] = k[tm - _WIN:tm, :]
    vlast[...] = v[tm - _WIN:tm, :]
    ulast[...] = u[tm - _SUB:tm, :]


def _mixer_sample_kernel(sink_ref, x_ref, gpre_ref, gpost_ref, convw_ref, ck_ref, cv_ref, st_ref,
                         win_ref, wao_ref, wco_ref, wout_ref, *rest, sb, t, layer, first):
    y_ref, nk_ref, nv_ref, nc_ref, ubuf = rest[0 if first else 3:]
    nk_ref, nv_ref, nc_ref = (_own_layer(r, first) for r in (nk_ref, nv_ref, nc_ref))
    gpre_ref, gpost_ref = _gain_row(gpre_ref, layer), _gain_row(gpost_ref, layer)
    tr = sb * t
    rows = _NH * t
    x = x_ref[...]
    h = _rms(x, gpre_ref[...]).astype(_BF)
    lo = lax.broadcasted_iota(jnp.int32, (tr, _LANES), 1) < _HD

    kv = _proj(h, win_ref, _OFF_KV, 2 * _KVD)
    k = kv[:, :_KVD]
    v = kv[:, _KVD:]
    q = _proj(h, win_ref, _OFF_Q, _ATT) * (_HD ** -0.5)
    pieces = []
    for hd in range(_NH):
        qc = q[:, (hd // 2) * _LANES:(hd // 2 + 1) * _LANES]
        if hd % 2 != hd // (_NH // 2):
            qc = pltpu.roll(qc, _HD, axis=1)
        piece = jnp.where(lo, qc, 0.0) if hd < _NH // 2 else jnp.where(lo, 0.0, qc)
        pieces.append(piece.reshape(sb, t, _LANES))
    q3 = jnp.concatenate(pieces, axis=1).astype(_BF)

    pad = jnp.zeros((sb, _BF_ROWS - t, _LANES), _F32)
    k_new = jnp.concatenate([k.reshape(sb, t, _KVD), pad], axis=1).astype(_BF)
    v_new = jnp.concatenate([v.reshape(sb, t, _KVD), pad], axis=1).astype(_BF)

    s_old = jnp.einsum('bqf,bfp->bqp', q3, ck_ref[...].astype(_BF), preferred_element_type=_F32)
    s_new = jnp.einsum('bqf,btf->bqt', q3, k_new, preferred_element_type=_F32)
    neg = jnp.float32(-jnp.inf)
    qi_old = lax.broadcasted_iota(jnp.int32, (rows, _WIN), 0) & (t - 1)
    kj_old = lax.broadcasted_iota(jnp.int32, (rows, _WIN), 1)
    qi_new = lax.broadcasted_iota(jnp.int32, (rows, _BF_ROWS), 0) & (t - 1)
    kj_new = lax.broadcasted_iota(jnp.int32, (rows, _BF_ROWS), 1) + _WIN
    s_old = s_old + jnp.where(_band_mask(qi_old, kj_old), 0.0, neg)[None]
    s_new = s_new + jnp.where(_band_mask(qi_new, kj_new), 0.0, neg)[None]
    sink_col = jnp.concatenate([jnp.full((t, 1), sink_ref[layer, hd], _F32) for hd in range(_NH)], axis=0)[None]
    m = jnp.maximum(jnp.maximum(jnp.max(s_old, axis=-1, keepdims=True), jnp.max(s_new, axis=-1, keepdims=True)),
                    sink_col)
    e_old = jnp.exp(s_old - m)
    e_new = jnp.exp(s_new - m)
    rden = 1.0 / (jnp.sum(e_old, axis=-1, keepdims=True) + jnp.sum(e_new, axis=-1, keepdims=True)
                  + jnp.exp(sink_col - m))
    o = (jnp.einsum('bqp,bfp->bqf', e_old.astype(_BF), cv_ref[...].astype(_BF), preferred_element_type=_F32)
         + jnp.einsum('bqt,btf->bqf', e_new.astype(_BF), v_new, preferred_element_type=_F32)) * rden
    chunks = []
    for c in range(_NH // 2):
        halves = []
        for hd in (2 * c, 2 * c + 1):
            oh = o[:, hd * t:(hd + 1) * t, :].reshape(tr, _LANES)
            if hd % 2 != hd // (_NH // 2):
                oh = pltpu.roll(oh, _HD, axis=1)
            halves.append(oh)
        chunks.append(jnp.where(lo, halves[0], halves[1]))
    attn = jnp.concatenate(chunks, axis=1)

    k_t = k.T
    v_t = v.T
    keep = lax.broadcasted_iota(jnp.int32, (_KVD, _WIN), 1) < _WIN - t
    per_chunk = _LANES // t
    for b in range(sb):
        c0 = (b // per_chunk) * _LANES
        shift_new = (_WIN - t - (b % per_chunk) * t) % _LANES
        for src_ref, new_t, dst_ref in ((ck_ref, k_t, nk_ref), (cv_ref, v_t, nv_ref)):
            old = pltpu.roll(src_ref[b], _WIN - t, axis=1)
            new = new_t[:, c0:c0 + _LANES]
            if shift_new:
                new = pltpu.roll(new, shift_new, axis=1)
            dst_ref[b] = jnp.where(keep, old, new)

    u = _proj(h, win_ref, _OFF_C, _CONV) * _proj(h, win_ref, _OFF_U, _CONV)
    u3 = u.reshape(sb, t, _CONV)
    ubuf[:, _SUB - 2:_SUB, :] = st_ref[...]
    ubuf[:, _SUB:_SUB + t, :] = u3
    us2 = ubuf[:, _SUB - 2:_SUB - 2 + t, :].reshape(tr, _CONV)
    us1 = ubuf[:, _SUB - 1:_SUB - 1 + t, :].reshape(tr, _CONV)
    nc_ref[...] = u3[:, t - 2:t, :]
    y_ref[...] = _mix_and_project(x, h, attn.astype(_BF), us2, us1, u,
                                  win_ref, _conv_taps(convw_ref, layer), wao_ref, wco_ref, wout_ref, gpost_ref)


def _mlp_kernel(x_ref, gpre_ref, gpost_ref, wup_ref, wdown_ref, y_ref, *, layer):
    _run_interleaved(_mlp_stages(x_ref, y_ref, _gain_row(gpre_ref, layer), _gain_row(gpost_ref, layer),
                                 wup_ref, wdown_ref))


def _whole_spec(a):
    return pl.BlockSpec(a.shape, lambda *_: (0,) * a.ndim)


def _weight_spec(w, slot):
    return pl.BlockSpec((None,) + w.shape[1:], lambda *_: (slot, 0, 0), pipeline_mode=pl.Buffered(1))


_SMEM_SPEC = pl.BlockSpec(memory_space=pltpu.SMEM)
_ANY_SPEC = pl.BlockSpec(memory_space=pl.ANY)


def _stacked_out_spec(depth, layer, first, block, index_map):
    lead, l0 = (depth, 0) if first else (None, layer)
    return pl.BlockSpec((lead,) + block, lambda *idx: (l0,) + index_map(*idx))


def _alias_args(prev, n_inputs):
    if prev is None:
        return [], [], {}
    return list(prev), [_ANY_SPEC] * len(prev), {n_inputs + i: 1 + i for i in range(len(prev))}


def _prompt_layer(x, layer, prev, sinks, gpre, gpost, gmpre, gmpost, convw, weights, weights_f32):
    n, seq, _ = x.shape
    depth = sinks.shape[0]
    tm = _PROMPT_TM
    first = prev is None
    own_cast = weights is None
    has_next = layer + 1 < depth
    assert seq % tm == 0 and tm % _WIN == 0 and seq >= _WIN
    tps = seq // tm
    n_tiles = n * tps

    def tile_of(t):
        return (t // tps, t % tps, 0)

    mixer_j = lambda i: jnp.minimum(i, n_tiles - 1)
    mixer_tile = lambda i: tile_of(mixer_j(i))
    mlp_tile = lambda i: tile_of(jnp.maximum(i - 1, 0))
    per_b = lambda i: (mixer_j(i) // tps, 0, 0)

    cast_in, cast_in_specs, cast_out_specs, cast_out_shapes = [], [], [], []
    for w in (weights_f32 if has_next else ()):
        rows, cols = w.shape[1:]
        assert rows % (n_tiles * _BF_ROWS) == 0
        chunk = rows // n_tiles
        cast_in.append(w)
        cast_in_specs.append(pl.BlockSpec((None, chunk, cols), lambda i: (layer + 1, mixer_j(i), 0)))
        cast_out_specs.append(pl.BlockSpec((None, chunk, cols), lambda i: (0, mixer_j(i), 0)))
        cast_out_shapes.append(jax.ShapeDtypeStruct((1, rows, cols), _BF))

    own_in, own_out_shapes, own_scratch = [], [], []
    if own_cast:
        for w in weights_f32:
            rows, cols = w.shape[1:]
            assert rows % _STAGE_ROWS == 0 and cols % _LANES == 0 and (cols % _STAGE_COLS) % _LANES == 0
            own_in.append(w)
            own_out_shapes.append(jax.ShapeDtypeStruct((1, rows, cols), _BF))
            own_scratch.append(pltpu.VMEM((rows, cols), _BF))
        assert tm % _STAGE_ROWS == 0
        own_scratch += [pltpu.SemaphoreType.DMA((_n_stage_slots(tm),)), pltpu.SemaphoreType.DMA((_N_WEIGHTS,))]

    weight_in = [] if own_cast else list(weights[0])
    weight_specs = [_ANY_SPEC] * len(weight_in)
    if not own_cast:
        assert weights[1] == 0 and all(w.shape[0] == 1 for w in weight_in)
        own_scratch = ([pltpu.VMEM(w.shape[1:], _BF) for w in weight_in]
                       + [pltpu.SemaphoreType.DMA((_N_WEIGHTS,))])
    inputs = [sinks, x, gpre, gpost, gmpre, gmpost, convw, *weight_in, *own_in, *cast_in]
    alias_in, alias_specs, aliases = _alias_args(prev, len(inputs))
    out = pl.pallas_call(
        functools.partial(_prompt_layer_kernel, tm=tm, tps=tps, layer=layer, first=first, own_cast=own_cast,
                          n_cast=len(cast_in)),
        grid=(n_tiles + 1,),
        in_specs=[_SMEM_SPEC,
                  pl.BlockSpec((None, tm, _D), mixer_tile)]
                 + [_whole_spec(a) for a in (gpre, gpost, gmpre, gmpost, convw)]
                 + weight_specs + [_ANY_SPEC] * len(own_in) + cast_in_specs + alias_specs,
        out_specs=[pl.BlockSpec((1, tm, _D), mlp_tile),
                   _stacked_out_spec(depth, layer, first, (None, _KVD, _WIN), per_b),
                   _stacked_out_spec(depth, layer, first, (None, _KVD, _WIN), per_b),
                   _stacked_out_spec(depth, layer, first, (None, 2, _CONV), per_b)]
                  + cast_out_specs + [_ANY_SPEC] * len(own_out_shapes),
        out_shape=[jax.ShapeDtypeStruct((n, seq, _D), _F32),
                   jax.ShapeDtypeStruct((depth, n, _KVD, _WIN), _F32),
                   jax.ShapeDtypeStruct((depth, n, _KVD, _WIN), _F32),
                   jax.ShapeDtypeStruct((depth, n, 2, _CONV), _F32)] + cast_out_shapes + own_out_shapes,
        scratch_shapes=own_scratch + [
                        pltpu.VMEM((2, _WIN + tm, _LANES), _BF),
                        pltpu.VMEM((2, _WIN + tm, _LANES), _BF),
                        pltpu.VMEM((4, 2, tm, _LANES), _BF),
                        pltpu.VMEM((_SUB + tm, _CONV), _F32),
                        pltpu.VMEM((tm, _ATT), _BF),
                        pltpu.VMEM((tm, _D), _F32),
                        pltpu.VMEM((_WIN, _KVD), _F32),
                        pltpu.VMEM((_WIN, _KVD), _F32),
                        pltpu.VMEM((_SUB, _CONV), _F32)],
        input_output_aliases=aliases,
        compiler_params=pltpu.CompilerParams(dimension_semantics=("arbitrary",),
                                             vmem_limit_bytes=_VMEM_LIMIT),
        name="prompt_layer",
    )(*inputs, *alias_in)
    n_next = len(cast_in)
    return out[0], tuple(out[1:4]), tuple(out[4:4 + n_next]), tuple(out[4 + n_next:])


def _mixer_sample(x, layer, prev, ck, cv, st, sinks, gpre, gpost, convw, slot, win, wao, wco, wout):
    n, t, _ = x.shape
    depth = sinks.shape[0]
    sb = _SAMPLE_SB
    first = prev is None
    assert n % sb == 0 and t == _SUB and ck.shape == (depth, n, _KVD, _WIN)
    tr = sb * t
    rows = lambda i: (i, 0)
    seqs = lambda i: (i, 0, 0)
    cache_spec = pl.BlockSpec((None, sb, _KVD, _WIN), lambda i: (layer, i, 0, 0))
    state_spec = pl.BlockSpec((None, sb, 2, _CONV), lambda i: (layer, i, 0, 0))
    inputs = [sinks, x.reshape(n * t, _D), gpre, gpost, convw, ck, cv, st, win, wao, wco, wout]
    alias_in, alias_specs, aliases = _alias_args(prev, len(inputs))
    out = pl.pallas_call(
        functools.partial(_mixer_sample_kernel, sb=sb, t=t, layer=layer, first=first),
        grid=(n // sb,),
        in_specs=[_SMEM_SPEC,
                  pl.BlockSpec((tr, _D), rows),
                  _whole_spec(gpre), _whole_spec(gpost), _whole_spec(convw),
                  cache_spec, cache_spec, state_spec]
                 + [_weight_spec(w, slot) for w in (win, wao, wco, wout)] + alias_specs,
        out_specs=[pl.BlockSpec((tr, _D), rows),
                   _stacked_out_spec(depth, layer, first, (sb, _KVD, _WIN), seqs),
                   _stacked_out_spec(depth, layer, first, (sb, _KVD, _WIN), seqs),
                   _stacked_out_spec(depth, layer, first, (sb, 2, _CONV), seqs)],
        out_shape=[jax.ShapeDtypeStruct((n * t, _D), _F32),
                   jax.ShapeDtypeStruct((depth, n, _KVD, _WIN), _F32),
                   jax.ShapeDtypeStruct((depth, n, _KVD, _WIN), _F32),
                   jax.ShapeDtypeStruct((depth, n, 2, _CONV), _F32)],
        scratch_shapes=[pltpu.VMEM((sb, 2 * _SUB, _CONV), _F32)],
        input_output_aliases=aliases,
        compiler_params=pltpu.CompilerParams(dimension_semantics=("arbitrary",),
                                             vmem_limit_bytes=_VMEM_LIMIT),
        name="mixer_sample",
    )(*inputs, *alias_in)
    return out[0].reshape(n, t, _D), tuple(out[1:])


def _mlp(x, layer, gpre, gpost, slot, wup, wdown):
    shape = x.shape
    x2 = x.reshape(-1, _D)
    nt = x2.shape[0]
    tm = _MLP_TM
    assert nt % tm == 0
    rows = lambda i: (i, 0)
    y = pl.pallas_call(
        functools.partial(_mlp_kernel, layer=layer),
        grid=(nt // tm,),
        in_specs=[pl.BlockSpec((tm, _D), rows), _whole_spec(gpre), _whole_spec(gpost),
                  _weight_spec(wup, slot), _weight_spec(wdown, slot)],
        out_specs=pl.BlockSpec((tm, _D), rows),
        out_shape=jax.ShapeDtypeStruct((nt, _D), _F32),
        compiler_params=pltpu.CompilerParams(dimension_semantics=("parallel",),
                                             vmem_limit_bytes=_VMEM_LIMIT),
        name="mlp",
    )(x2, gpre, gpost, wup, wdown)
    return y.reshape(shape)


def kernel(x_prompt, x_sample, cache_k, cache_v, state_conv, g_mix_pre, g_mix_post, g_mlp_pre, g_mlp_post,
           w_in, attn_sinks, conv_w, w_attn_o, w_conv_o, w_out, w_up, w_down):
    depth = w_in.shape[0]
    n_dec = x_sample.shape[0]
    feature_major = lambda c: jnp.transpose(c, (0, 1, 3, 4, 2)).reshape(depth, n_dec, _KVD, _WIN)
    ck, cv = feature_major(cache_k), feature_major(cache_v)
    weights_f32 = (w_in, w_attn_o, w_conv_o, w_out, w_up, w_down)
    gains = [g_mix_pre, g_mix_post, g_mlp_pre, g_mlp_post]
    conv_w = jnp.transpose(conv_w, (1, 0, 2))

    weights = None
    yp, ys = x_prompt, x_sample
    caches_p = caches_s = None
    for l in range(depth):
        yp, caches_p, next_cast, own = _prompt_layer(yp, l, caches_p, attn_sinks, *gains, conv_w, weights,
                                                     weights_f32)
        if weights is None:
            weights = (own, 0)
        ys, caches_s = _mixer_sample(ys, l, caches_s, ck, cv, state_conv, attn_sinks, gains[0], gains[1], conv_w,
                                     weights[1], *weights[0][:4])
        ys = _mlp(ys, l, gains[2], gains[3], weights[1], *weights[0][4:])
        weights = (next_cast, 0)

    position_major = lambda a: jnp.transpose(a.reshape(a.shape[:2] + (2, _HD, _WIN)), (0, 1, 4, 2, 3))
    return (yp, ys, position_major(caches_p[0]), position_major(caches_p[1]), caches_p[2],
            position_major(caches_s[0]), position_major(caches_s[1]), caches_s[2])
```
